```python
import math
import jax, jax.numpy as jnp
from jax import lax
import numpy as np

D_MODEL = 1024
BATCH = 1
SEQ = 16384
DEPTH = 1

HEAD_DIM = 64
N_DIFF_HEADS = 4
DIFF_V_DIM = 2 * HEAD_DIM
N_SB_HEADS = 8
DIFF_QK_WIDTH = N_DIFF_HEADS * 2 * HEAD_DIM
DIFF_V_WIDTH = N_DIFF_HEADS * DIFF_V_DIM
SB_WIDTH = N_SB_HEADS * HEAD_DIM
MIX_WIDTH = DIFF_V_WIDTH + SB_WIDTH
QKV_WIDTH = 2 * DIFF_QK_WIDTH + DIFF_V_WIDTH + 3 * SB_WIDTH
QKV_SPLITS = (DIFF_QK_WIDTH, 2 * DIFF_QK_WIDTH, 2 * DIFF_QK_WIDTH + DIFF_V_WIDTH,
              2 * DIFF_QK_WIDTH + DIFF_V_WIDTH + SB_WIDTH,
              2 * DIFF_QK_WIDTH + DIFF_V_WIDTH + 2 * SB_WIDTH)
D_FF = 2816
CONV_WIDTH = 3
N_BUCKETS = 32
MAX_DISTANCE = 128
Q_BLOCK = 128
RMS_EPS = 1e-6

kernel_name = 'hybrid_diffattn_stickbreak_convffn_block'


def _rmsnorm(x, g):
    xf = x.astype(jnp.float32)
    y = xf * lax.rsqrt(jnp.mean(xf * xf, axis=-1, keepdims=True) + RMS_EPS)
    return (y * g.astype(jnp.float32)).astype(x.dtype)


def _lambda_init(layer):
    return 0.8 - 0.6 * math.exp(-0.3 * layer)


def _relative_bucket(rel):
    n = jnp.maximum(rel, 0)
    max_exact = N_BUCKETS // 2
    nf = jnp.maximum(n, 1).astype(jnp.float32)
    large = max_exact + (jnp.log(nf / max_exact) / math.log(MAX_DISTANCE / max_exact)
                         * (N_BUCKETS - max_exact)).astype(jnp.int32)
    large = jnp.minimum(large, N_BUCKETS - 1)
    return jnp.where(n < max_exact, n, large)


def _diff_attention(q, k, v, lam, rel_bias):
    B, H, _, S, d = q.shape
    nb = S // Q_BLOCK
    qb = q.reshape(B, H, 2, nb, Q_BLOCK, d).transpose(3, 0, 1, 2, 4, 5)
    starts = jnp.arange(nb, dtype=jnp.int32) * Q_BLOCK
    k_pos = jnp.arange(S, dtype=jnp.int32)
    scale = d ** -0.5
    table = rel_bias.astype(jnp.float32)

    def one_block(args):
        q_blk, t0 = args
        rel = (t0 + jnp.arange(Q_BLOCK, dtype=jnp.int32))[:, None] - k_pos[None, :]
        bias = jnp.transpose(table[_relative_bucket(rel)], (2, 0, 1))
        logits = jnp.einsum('bhmqd,bhmkd->bhmqk', q_blk, k).astype(jnp.float32) * scale
        logits = logits + bias[None, :, None]
        logits = jnp.where(rel >= 0, logits, -jnp.inf)
        p = jax.nn.softmax(logits, axis=-1)
        w = p[:, :, 0] - lam * p[:, :, 1]
        return jnp.einsum('bhqk,bhkd->bhqd', w.astype(v.dtype), v)

    out = lax.map(one_block, (qb, starts))
    return out.transpose(1, 2, 0, 3, 4).reshape(B, H, S, v.shape[-1])


def _stick_breaking_attention(q, k, v):
    B, H, S, d = q.shape
    nb = S // Q_BLOCK
    qb = q.reshape(B, H, nb, Q_BLOCK, d).transpose(2, 0, 1, 3, 4)
    starts = jnp.arange(nb, dtype=jnp.int32) * Q_BLOCK
    k_pos = jnp.arange(S, dtype=jnp.int32)
    scale = d ** -0.5

    def one_block(args):
        q_blk, t0 = args
        rel = (t0 + jnp.arange(Q_BLOCK, dtype=jnp.int32))[:, None] - k_pos[None, :]
        strict = rel > 0
        z = jnp.einsum('bhqd,bhkd->bhqk', q_blk, k).astype(jnp.float32) * scale
        log_beta = jax.nn.log_sigmoid(z)
        log_1mb = jnp.where(strict, jax.nn.log_sigmoid(-z), 0.0)
        between = lax.cumsum(log_1mb, axis=3, reverse=True) - log_1mb
        a = jnp.where(strict, jnp.exp(log_beta + between), 0.0)
        return jnp.einsum('bhqk,bhkd->bhqd', a.astype(v.dtype), v)

    out = lax.map(one_block, (qb, starts))
    return out.transpose(1, 2, 0, 3, 4).reshape(B, H, S, d)


def _causal_depthwise_conv(u, w, b):
    S = u.shape[1]
    up = jnp.pad(u, ((0, 0), (CONV_WIDTH - 1, 0), (0, 0)))
    y = b
    for i in range(CONV_WIDTH):
        y = y + up[:, i:i + S] * w[i]
    return y


def setup_inputs(seed: int = 0) -> dict:
    key = jax.random.key(seed)
    ks = jax.random.split(key, 18)

    def nrm(k, shape, scale):
        return jax.random.normal(k, shape, jnp.float32) * scale

    def gain(k, shape):
        return 1.0 + 0.05 * jax.random.normal(k, shape, jnp.float32)

    return {
        'x': nrm(ks[0], (BATCH, SEQ, D_MODEL), 1.0),
        'attn_pre_norm': gain(ks[1], (DEPTH, D_MODEL)),
        'w_qkv': nrm(ks[2], (DEPTH, D_MODEL, QKV_WIDTH), D_MODEL ** -0.5),
        'lambda_q1': nrm(ks[3], (DEPTH, HEAD_DIM), 0.1),
        'lambda_k1': nrm(ks[4], (DEPTH, HEAD_DIM), 0.1),
        'lambda_q2': nrm(ks[5], (DEPTH, HEAD_DIM), 0.1),
        'lambda_k2': nrm(ks[6], (DEPTH, HEAD_DIM), 0.1),
        'diff_subln': gain(ks[7], (DEPTH, DIFF_V_DIM)),
        'sb_norm': gain(ks[8], (DEPTH, HEAD_DIM)),
        'rel_bias': nrm(ks[9], (N_BUCKETS, N_DIFF_HEADS), 0.5),
        'w_o': nrm(ks[10], (DEPTH, MIX_WIDTH, D_MODEL), MIX_WIDTH ** -0.5),
        'attn_post_norm': gain(ks[11], (DEPTH, D_MODEL)),
        'ffn_pre_norm': gain(ks[12], (DEPTH, D_MODEL)),
        'w_up': nrm(ks[13], (DEPTH, D_MODEL, 2 * D_FF), D_MODEL ** -0.5),
        'conv_w': nrm(ks[14], (DEPTH, CONV_WIDTH, 2 * D_FF), CONV_WIDTH ** -0.5),
        'conv_b': nrm(ks[15], (DEPTH, 2 * D_FF), 0.01),
        'w_down': nrm(ks[16], (DEPTH, D_FF, D_MODEL), D_FF ** -0.5),
        'ffn_post_norm': gain(ks[17], (DEPTH, D_MODEL)),
    }


def reference(x, attn_pre_norm, w_qkv, lambda_q1, lambda_k1, lambda_q2, lambda_k2,
              diff_subln, sb_norm, rel_bias, w_o, attn_post_norm, ffn_pre_norm,
              w_up, conv_w, conv_b, w_down, ffn_post_norm):
    B, S, _ = x.shape
    for l in range(DEPTH):
        lam_init = _lambda_init(l)
        h = _rmsnorm(x, attn_pre_norm[l])
        qkv = h @ w_qkv[l]
        dq, dk, dv, sq, sk, sv = jnp.split(qkv, QKV_SPLITS, axis=-1)
        dq = dq.reshape(B, S, N_DIFF_HEADS, 2, HEAD_DIM).transpose(0, 2, 3, 1, 4)
        dk = dk.reshape(B, S, N_DIFF_HEADS, 2, HEAD_DIM).transpose(0, 2, 3, 1, 4)
        dv = dv.reshape(B, S, N_DIFF_HEADS, DIFF_V_DIM).transpose(0, 2, 1, 3)
        lam = (jnp.exp(jnp.sum(lambda_q1[l].astype(jnp.float32) * lambda_k1[l].astype(jnp.float32)))
               - jnp.exp(jnp.sum(lambda_q2[l].astype(jnp.float32) * lambda_k2[l].astype(jnp.float32)))
               + lam_init)
        diff_o = _diff_attention(dq, dk, dv, lam, rel_bias)
        diff_o = _rmsnorm(diff_o, diff_subln[l]) * (1.0 - lam_init)
        diff_o = diff_o.transpose(0, 2, 1, 3).reshape(B, S, DIFF_V_WIDTH)

        sq = sq.reshape(B, S, N_SB_HEADS, HEAD_DIM).transpose(0, 2, 1, 3)
        sk = sk.reshape(B, S, N_SB_HEADS, HEAD_DIM).transpose(0, 2, 1, 3)
        sv = sv.reshape(B, S, N_SB_HEADS, HEAD_DIM).transpose(0, 2, 1, 3)
        sb_o = _rmsnorm(_stick_breaking_attention(sq, sk, sv), sb_norm[l])
        sb_o = sb_o.transpose(0, 2, 1, 3).reshape(B, S, SB_WIDTH)

        mix = jnp.concatenate([diff_o, sb_o], axis=-1) @ w_o[l]
        x = x + _rmsnorm(mix, attn_post_norm[l])

        h = _rmsnorm(x, ffn_pre_norm[l])
        u = _causal_depthwise_conv(h @ w_up[l], conv_w[l], conv_b[l])
        gate, val = jnp.split(u, 2, axis=-1)
        y = (jax.nn.gelu(gate, approximate=True) * val) @ w_down[l]
        x = x + _rmsnorm(y, ffn_post_norm[l])
    return x
```

```python
import functools
import math

import jax
import jax.numpy as jnp
from jax import lax
from jax.experimental import pallas as pl
from jax.experimental.pallas import tpu as pltpu

F32 = jnp.float32
BF16 = jnp.bfloat16

D_MODEL = 1024
HEAD_DIM = 64
N_DIFF_HEADS = 4
N_SB_HEADS = 8
DIFF_WIDTH = N_DIFF_HEADS * 2 * HEAD_DIM
SB_WIDTH = N_SB_HEADS * HEAD_DIM
D_FF = 2816
CONV_WIDTH = 3
N_BUCKETS = 32
MAX_DISTANCE = 128
RMS_EPS = 1e-6
LAMBDA_INIT = 0.8 - 0.6 * math.exp(-0.3 * 0)

LOG2E = 1.4426950408889634
Q_SCALE = HEAD_DIM ** -0.5 * LOG2E
MASK_VALUE = -1e30

LANES = 128
PROJ_ROWS = 512
DIFF_BLOCK = 512
SB_Q_BLOCK = 512
SB_K_BLOCK = 256
OUT_ROWS = 512
FFN_ROWS = 512
FFN_CHUNK = D_FF // 2
HALO_ROWS = 8
VMEM_LIMIT = 48 * 1024 * 1024


def _rms(x, g):
    return x * lax.rsqrt(jnp.mean(x * x, axis=-1, keepdims=True) + RMS_EPS) * g


def _proj_kernel(x_ref, g_ref, w_ref, wkt_ref, qd_ref, vd_ref, qs_ref, vs_ref, kdt_ref, kst_ref):
    hb = _rms(x_ref[...], g_ref[...]).astype(BF16)
    o = jnp.dot(hb, w_ref[...], preferred_element_type=F32)
    qd_ref[...] = (o[:, 0:512] * Q_SCALE).astype(BF16)
    vd_ref[...] = o[:, 512:1024].astype(BF16)
    qs_ref[...] = (o[:, 1024:1536] * Q_SCALE).astype(BF16)
    vs_ref[...] = o[:, 1536:2048].astype(BF16)
    kt = lax.dot_general(wkt_ref[...], hb, (((1,), (1,)), ((), ())), preferred_element_type=F32)
    kdt_ref[...] = kt[0:512].astype(BF16)
    kst_ref[...] = kt[512:1024].astype(BF16)


def _proj(x2, g, w_cat, wkt_cat):
    s = x2.shape[0]
    rows = PROJ_ROWS
    row_spec = lambda w: pl.BlockSpec((rows, w), lambda i: (i, 0))
    full = lambda a: pl.BlockSpec(a.shape, lambda i: (0, 0))
    out_rows = jax.ShapeDtypeStruct((s, 512), BF16)
    out_t = jax.ShapeDtypeStruct((512, s), BF16)
    return pl.pallas_call(
        _proj_kernel,
        grid=(s // rows,),
        in_specs=[row_spec(D_MODEL), full(g), full(w_cat), full(wkt_cat)],
        out_specs=[row_spec(512)] * 4 + [pl.BlockSpec((512, rows), lambda i: (0, i))] * 2,
        out_shape=[out_rows] * 4 + [out_t] * 2,
        compiler_params=pltpu.CompilerParams(
            dimension_semantics=("arbitrary",), vmem_limit_bytes=VMEM_LIMIT),
        name="proj",
    )(x2, g, w_cat, wkt_cat)


def _bias_kernel(tab_ref, o_ref):
    h = pl.program_id(0)
    w = pl.program_id(1)
    t = DIFF_BLOCK
    rel = (lax.broadcasted_iota(jnp.int32, (t, t), 0) - lax.broadcasted_iota(jnp.int32, (t, t), 1)
           + w * t)
    n = jnp.maximum(rel, 0)
    max_exact = N_BUCKETS // 2
    nf = jnp.maximum(n, 1).astype(F32)
    large = max_exact + (jnp.log(nf / max_exact) / math.log(MAX_DISTANCE / max_exact)
                         * (N_BUCKETS - max_exact)).astype(jnp.int32)
    large = jnp.minimum(large, N_BUCKETS - 1)
    bucket = jnp.where(n < max_exact, n, large)
    b = jnp.zeros((t, t), F32)
    for k in range(N_BUCKETS):
        b = jnp.where(bucket == k, tab_ref[k, h], b)
    b = (b - tab_ref[N_BUCKETS - 1, h]) * LOG2E
    o_ref[0, 0] = jnp.where(rel >= 0, b, MASK_VALUE)


def _bias_tiles(rel_bias):
    t = DIFF_BLOCK
    return pl.pallas_call(
        _bias_kernel,
        grid=(N_DIFF_HEADS, 2),
        in_specs=[pl.BlockSpec(memory_space=pltpu.SMEM)],
        out_specs=pl.BlockSpec((1, 1, t, t), lambda h, w: (h, w, 0, 0)),
        out_shape=jax.ShapeDtypeStruct((N_DIFF_HEADS, 2, t, t), F32),
        compiler_params=pltpu.CompilerParams(
            dimension_semantics=("arbitrary", "arbitrary"), vmem_limit_bytes=VMEM_LIMIT),
        name="bias",
    )(rel_bias)


def _diff_kernel(lam_ref, q_ref, kt_ref, v_ref, bias_ref, g_ref, o_ref, m_sc, l_sc, acc_sc):
    i = pl.program_id(1)
    t = DIFF_BLOCK
    lane = lax.broadcasted_iota(jnp.int32, (t, LANES), 1)
    q = q_ref[...]
    zero = jnp.zeros_like(q)
    qm = (jnp.where(lane < HEAD_DIM, q, zero), jnp.where(lane >= HEAD_DIM, q, zero))

    m_sc[...] = jnp.full(m_sc.shape, MASK_VALUE, F32)
    l_sc[...] = jnp.zeros(l_sc.shape, F32)
    acc_sc[...] = jnp.zeros(acc_sc.shape, F32)

    def step(j, bias):
        start = pl.multiple_of(j * t, t)
        kt = kt_ref[:, pl.ds(start, t)]
        v = v_ref[pl.ds(start, t), :]
        for mp in range(2):
            s = jnp.dot(qm[mp], kt, preferred_element_type=F32)
            if bias is not None:
                s = s + bias
            m_prev = m_sc[mp]
            m_new = jnp.maximum(m_prev, jnp.max(s, axis=1, keepdims=True))
            alpha = jnp.exp2(m_prev - m_new)
            p = jnp.exp2(s - pltpu.repeat(m_new, t // LANES, axis=1))
            l_sc[mp] = alpha * l_sc[mp] + jnp.sum(p, axis=1, keepdims=True)
            acc_sc[mp] = alpha * acc_sc[mp] + jnp.dot(p.astype(BF16), v, preferred_element_type=F32)
            m_sc[mp] = m_new

    def far_body(j, carry):
        step(j, None)
        return carry

    lax.fori_loop(0, i - 1, far_body, 0)

    @pl.when(i >= 1)
    def _():
        step(i - 1, bias_ref[0, 1])

    step(i, bias_ref[0, 0])

    lp = lam_ref[...]
    lam = (jnp.exp(jnp.sum(lp[0:1] * lp[1:2], axis=1, keepdims=True))
           - jnp.exp(jnp.sum(lp[2:3] * lp[3:4], axis=1, keepdims=True)) + LAMBDA_INIT)
    o = acc_sc[0] / l_sc[0] - lam * (acc_sc[1] / l_sc[1])
    o_ref[...] = (_rms(o, g_ref[...]) * (1.0 - LAMBDA_INIT)).astype(o_ref.dtype)


def _diff_attention(lam_params, qd, kdt, vd, bias, g):
    s = qd.shape[0]
    t = DIFF_BLOCK
    return pl.pallas_call(
        _diff_kernel,
        grid=(N_DIFF_HEADS, s // t),
        in_specs=[
            pl.BlockSpec(lam_params.shape, lambda h, i: (0, 0)),
            pl.BlockSpec((t, LANES), lambda h, i: (i, h)),
            pl.BlockSpec((LANES, s), lambda h, i: (h, 0)),
            pl.BlockSpec((s, LANES), lambda h, i: (0, h)),
            pl.BlockSpec((1, 2, t, t), lambda h, i: (h, 0, 0, 0)),
            pl.BlockSpec(g.shape, lambda h, i: (0, 0)),
        ],
        out_specs=pl.BlockSpec((t, LANES), lambda h, i: (i, h)),
        out_shape=jax.ShapeDtypeStruct((s, DIFF_WIDTH), BF16),
        scratch_shapes=[pltpu.VMEM((2, t, LANES), F32)] * 3,
        compiler_params=pltpu.CompilerParams(
            dimension_semantics=("arbitrary", "arbitrary"), vmem_limit_bytes=VMEM_LIMIT),
        name="diff_attn",
    )(lam_params, qd, kdt, vd, bias, g)


def _sb_kernel(q_ref, kt_ref, v_ref, g_ref, o_ref, carry_sc, acc_sc):
    i = pl.program_id(1)
    tq, tk = SB_Q_BLOCK, SB_K_BLOCK
    lane_q = lax.broadcasted_iota(jnp.int32, (tq, LANES), 1)
    q = q_ref[...]
    zero_q = jnp.zeros_like(q)
    qm = (jnp.where(lane_q < HEAD_DIM, q, zero_q), jnp.where(lane_q >= HEAD_DIM, q, zero_q))
    lane_k = lax.broadcasted_iota(jnp.int32, (tk, LANES), 1)
    upper = (lax.broadcasted_iota(jnp.int32, (tk, tk), 0)
             > lax.broadcasted_iota(jnp.int32, (tk, tk), 1)).astype(BF16)

    carry_sc[...] = jnp.zeros(carry_sc.shape, F32)
    acc_sc[...] = jnp.zeros(acc_sc.shape, F32)

    def step(jb, masked):
        start = pl.multiple_of(jb * tk, tk)
        kt = kt_ref[:, pl.ds(start, tk)]
        v = v_ref[pl.ds(start, tk), :]
        zero_v = jnp.zeros_like(v)
        vm = (jnp.where(lane_k < HEAD_DIM, v, zero_v), jnp.where(lane_k >= HEAD_DIM, v, zero_v))
        if masked:
            strict = (lax.broadcasted_iota(jnp.int32, (tq, tk), 0) + i * tq
                      > lax.broadcasted_iota(jnp.int32, (tq, tk), 1) + jb * tk)
        pv = None
        for hh in range(2):
            z = jnp.dot(qm[hh], kt, preferred_element_type=F32)
            sp = jnp.log2(1.0 + jnp.exp2(-jnp.abs(z)))
            log_beta = jnp.minimum(z, 0.0) - sp
            log_1mb = log_beta - z
            if masked:
                log_1mb = jnp.where(strict, log_1mb, 0.0)
            hi = log_1mb.astype(BF16)
            lo = (log_1mb - hi.astype(F32)).astype(BF16)
            between = (jnp.dot(hi, upper, preferred_element_type=F32)
                       + jnp.dot(lo, upper, preferred_element_type=F32))
            carry = carry_sc[hh]
            a = jnp.exp2(log_beta + between + pltpu.repeat(carry, tk // LANES, axis=1))
            if masked:
                a = jnp.where(strict, a, 0.0)
            contrib = jnp.dot(a.astype(BF16), vm[hh], preferred_element_type=F32)
            pv = contrib if pv is None else pv + contrib
            carry_sc[hh] = carry + jnp.sum(log_1mb, axis=1, keepdims=True)
        acc_sc[...] += pv

    n_diag = tq // tk
    for d in range(n_diag - 1, -1, -1):
        step(i * n_diag + d, True)

    def far_body(r, c):
        step(i * n_diag - 1 - r, False)
        return c

    lax.fori_loop(0, i * n_diag, far_body, 0)

    o = acc_sc[...]
    sq = o * o
    ss0 = jnp.sum(jnp.where(lane_q < HEAD_DIM, sq, 0.0), axis=1, keepdims=True)
    ss1 = jnp.sum(jnp.where(lane_q >= HEAD_DIM, sq, 0.0), axis=1, keepdims=True)
    ms = jnp.where(lane_q < HEAD_DIM, ss0, ss1) * (1.0 / HEAD_DIM)
    o_ref[...] = (o * lax.rsqrt(ms + RMS_EPS) * g_ref[...]).astype(o_ref.dtype)


def _sb_attention(qs, kst, vs, g2):
    s = qs.shape[0]
    tq = SB_Q_BLOCK
    return pl.pallas_call(
        _sb_kernel,
        grid=(N_SB_HEADS // 2, s // tq),
        in_specs=[
            pl.BlockSpec((tq, LANES), lambda h, i: (i, h)),
            pl.BlockSpec((LANES, s), lambda h, i: (h, 0)),
            pl.BlockSpec((s, LANES), lambda h, i: (0, h)),
            pl.BlockSpec(g2.shape, lambda h, i: (0, 0)),
        ],
        out_specs=pl.BlockSpec((tq, LANES), lambda h, i: (i, h)),
        out_shape=jax.ShapeDtypeStruct((s, SB_WIDTH), BF16),
        scratch_shapes=[pltpu.VMEM((2, tq, LANES), F32), pltpu.VMEM((tq, LANES), F32)],
        compiler_params=pltpu.CompilerParams(
            dimension_semantics=("arbitrary", "arbitrary"), vmem_limit_bytes=VMEM_LIMIT),
        name="sb_attn",
    )(qs, kst, vs, g2)


def _out_kernel(d_ref, s_ref, wd_ref, ws_ref, x_ref, gpost_ref, gpre_ref, x1_ref, h2_ref):
    mix = (jnp.dot(d_ref[...], wd_ref[...], preferred_element_type=F32)
           + jnp.dot(s_ref[...], ws_ref[...], preferred_element_type=F32))
    x1 = x_ref[...] + _rms(mix, gpost_ref[...])
    x1_ref[...] = x1
    h2_ref[...] = _rms(x1, gpre_ref[...]).astype(h2_ref.dtype)


def _out_proj(diff_o, sb_o, w_od, w_os, x2, g_post, g_pre):
    s = x2.shape[0]
    rows = OUT_ROWS
    row_spec = lambda w: pl.BlockSpec((rows, w), lambda i: (i, 0))
    full = lambda a: pl.BlockSpec(a.shape, lambda i: (0, 0))
    return pl.pallas_call(
        _out_kernel,
        grid=(s // rows,),
        in_specs=[row_spec(512), row_spec(512), full(w_od), full(w_os), row_spec(D_MODEL),
                  full(g_post), full(g_pre)],
        out_specs=[row_spec(D_MODEL), row_spec(D_MODEL)],
        out_shape=[jax.ShapeDtypeStruct((s, D_MODEL), F32), jax.ShapeDtypeStruct((s, D_MODEL), BF16)],
        compiler_params=pltpu.CompilerParams(
            dimension_semantics=("arbitrary",), vmem_limit_bytes=VMEM_LIMIT),
        name="out_proj",
    )(diff_o, sb_o, w_od, w_os, x2, g_post, g_pre)


def _gelu_tanh(x):
    return 0.5 * x * (1.0 + jnp.tanh(math.sqrt(2.0 / math.pi) * (x + 0.044715 * (x * x * x))))


def _ffn_kernel(h_ref, halo_ref, wg_ref, wv_ref, cwg_ref, cwv_ref, cbg_ref, cbv_ref, wd_ref,
                x1_ref, g_ref, o_ref, ug_sc, uv_sc, acc_sc):
    i = pl.program_id(0)
    f = pl.program_id(1)
    rows = FFN_ROWS
    h = h_ref[...]
    halo = jnp.where(i > 0, halo_ref[...], jnp.zeros_like(halo_ref[...]))

    def conv(w_ref, cw_ref, cb_ref, u_sc):
        u_sc[0:HALO_ROWS, :] = jnp.dot(halo, w_ref[...], preferred_element_type=F32)
        u_sc[HALO_ROWS:, :] = jnp.dot(h, w_ref[...], preferred_element_type=F32)
        cw = cw_ref[...]
        y = cb_ref[...]
        for k in range(CONV_WIDTH):
            off = HALO_ROWS - (CONV_WIDTH - 1) + k
            y = y + u_sc[off:off + rows, :] * cw[k:k + 1]
        return y

    gate = conv(wg_ref, cwg_ref, cbg_ref, ug_sc)
    val = conv(wv_ref, cwv_ref, cbv_ref, uv_sc)
    act = (_gelu_tanh(gate) * val).astype(BF16)
    part = jnp.dot(act, wd_ref[...], preferred_element_type=F32)

    @pl.when(f == 0)
    def _():
        acc_sc[...] = part

    @pl.when(f > 0)
    def _():
        acc_sc[...] += part

    @pl.when(f == pl.num_programs(1) - 1)
    def _():
        o_ref[...] = x1_ref[...] + _rms(acc_sc[...], g_ref[...])


def _ffn(h2, w_up, conv_w, conv_b, w_down, x1, g_post):
    s = h2.shape[0]
    rows, chunk = FFN_ROWS, FFN_CHUNK
    n_chunk = D_FF // chunk
    halo_blocks = rows // HALO_ROWS
    row_spec = pl.BlockSpec((rows, D_MODEL), lambda i, f: (i, 0))
    halo_spec = pl.BlockSpec((HALO_ROWS, D_MODEL),
                             lambda i, f: (jnp.maximum(i * halo_blocks - 1, 0), 0))
    gate_cols = lambda r: pl.BlockSpec((r, chunk), lambda i, f: (0, f))
    val_cols = lambda r: pl.BlockSpec((r, chunk), lambda i, f: (0, f + n_chunk))
    return pl.pallas_call(
        _ffn_kernel,
        grid=(s // rows, n_chunk),
        in_specs=[row_spec, halo_spec, gate_cols(D_MODEL), val_cols(D_MODEL),
                  gate_cols(CONV_WIDTH), val_cols(CONV_WIDTH), gate_cols(1), val_cols(1),
                  pl.BlockSpec((chunk, D_MODEL), lambda i, f: (f, 0)),
                  row_spec, pl.BlockSpec(g_post.shape, lambda i, f: (0, 0))],
        out_specs=row_spec,
        out_shape=jax.ShapeDtypeStruct((s, D_MODEL), F32),
        scratch_shapes=[pltpu.VMEM((HALO_ROWS + rows, chunk), F32)] * 2
        + [pltpu.VMEM((rows, D_MODEL), F32)],
        compiler_params=pltpu.CompilerParams(
            dimension_semantics=("arbitrary", "arbitrary"), vmem_limit_bytes=VMEM_LIMIT),
        name="ffn",
    )(h2, h2, w_up, w_up, conv_w, conv_w, conv_b, conv_b, w_down, x1, g_post)


def kernel(x, attn_pre_norm, w_qkv, lambda_q1, lambda_k1, lambda_q2, lambda_k2, diff_subln, sb_norm,
           rel_bias, w_o, attn_post_norm, ffn_pre_norm, w_up, conv_w, conv_b, w_down, ffn_post_norm):
    b, s, _ = x.shape
    assert b == 1 and attn_pre_norm.shape[0] == 1, "single sequence, single layer"
    assert s % max(PROJ_ROWS, DIFF_BLOCK, SB_Q_BLOCK, OUT_ROWS, FFN_ROWS) == 0
    x2 = x[0]
    row = lambda a: a.reshape(1, -1).astype(F32)

    wq = w_qkv[0].astype(BF16)
    w_cat = jnp.concatenate([wq[:, 0:512], wq[:, 1024:1536], wq[:, 1536:2048], wq[:, 2560:3072]], axis=1)
    wkt_cat = jnp.concatenate([wq[:, 512:1024], wq[:, 2048:2560]], axis=1).T
    qd, vd, qs, vs, kdt, kst = _proj(x2, row(attn_pre_norm), w_cat, wkt_cat)

    bias = _bias_tiles(rel_bias.astype(F32))
    lam_params = jnp.concatenate([lambda_q1, lambda_k1, lambda_q2, lambda_k2], axis=0).astype(F32)
    diff_o = _diff_attention(lam_params, qd, kdt, vd, bias, row(diff_subln))
    sb_o = _sb_attention(qs, kst, vs, row(jnp.concatenate([sb_norm[0], sb_norm[0]])))

    wo = w_o[0].astype(BF16)
    x1, h2 = _out_proj(diff_o, sb_o, wo[:DIFF_WIDTH], wo[DIFF_WIDTH:], x2,
                       row(attn_post_norm), row(ffn_pre_norm))
    out = _ffn(h2, w_up[0].astype(BF16), conv_w[0].astype(F32), conv_b.astype(F32),
               w_down[0].astype(BF16), x1, row(ffn_post_norm))
    return out[None]
```

```python
import functools
import math

import jax
import jax.numpy as jnp
from jax import lax
from jax.experimental import pallas as pl
from jax.experimental.pallas import tpu as pltpu

F32 = jnp.float32
BF16 = jnp.bfloat16

D_MODEL = 1024
HEAD_DIM = 64
N_DIFF_HEADS = 4
N_SB_HEADS = 8
DIFF_WIDTH = N_DIFF_HEADS * 2 * HEAD_DIM
SB_WIDTH = N_SB_HEADS * HEAD_DIM
D_FF = 2816
CONV_WIDTH = 3
N_BUCKETS = 32
MAX_DISTANCE = 128
RMS_EPS = 1e-6
LAMBDA_INIT = 0.8 - 0.6 * math.exp(-0.3 * 0)

LOG2E = 1.4426950408889634
Q_SCALE = HEAD_DIM ** -0.5 * LOG2E
MASK_VALUE = -1e30

LANES = 128
PROJ_ROWS = 512
DIFF_BLOCK = 512
SB_Q_BLOCK = 512
SB_K_BLOCK = 256
OUT_ROWS = 512
FFN_ROWS = 512
FFN_CHUNK = D_FF // 2
HALO_ROWS = 8
VMEM_LIMIT = 48 * 1024 * 1024


def _bits(x):
    return lax.bitcast_convert_type(x, jnp.uint32)


def _from_bits(u):
    return lax.bitcast_convert_type(u, F32)


def _rms(x, g):
    return x * lax.rsqrt(jnp.mean(x * x, axis=-1, keepdims=True) + RMS_EPS) * g


def _proj_kernel(x_ref, g_ref, w_ref, wkt_ref, qd_ref, vd_ref, qs_ref, vs_ref, kdt_ref, kst_ref):
    hb = _rms(x_ref[...], g_ref[...]).astype(BF16)
    o = jnp.dot(hb, w_ref[...], preferred_element_type=F32)
    qd_ref[...] = (o[:, 0:512] * Q_SCALE).astype(BF16)
    vd_ref[...] = o[:, 512:1024].astype(BF16)
    qs_ref[...] = (o[:, 1024:1536] * Q_SCALE).astype(BF16)
    vs_ref[...] = o[:, 1536:2048].astype(BF16)
    kt = lax.dot_general(wkt_ref[...], hb, (((1,), (1,)), ((), ())), preferred_element_type=F32)
    kdt_ref[...] = kt[0:512].astype(BF16)
    kst_ref[...] = kt[512:1024].astype(BF16)


def _proj(x2, g, w_cat, wkt_cat):
    s = x2.shape[0]
    rows = PROJ_ROWS
    row_spec = lambda w: pl.BlockSpec((rows, w), lambda i: (i, 0))
    full = lambda a: pl.BlockSpec(a.shape, lambda i: (0, 0))
    out_rows = jax.ShapeDtypeStruct((s, 512), BF16)
    out_t = jax.ShapeDtypeStruct((512, s), BF16)
    return pl.pallas_call(
        _proj_kernel,
        grid=(s // rows,),
        in_specs=[row_spec(D_MODEL), full(g), full(w_cat), full(wkt_cat)],
        out_specs=[row_spec(512)] * 4 + [pl.BlockSpec((512, rows), lambda i: (0, i))] * 2,
        out_shape=[out_rows] * 4 + [out_t] * 2,
        compiler_params=pltpu.CompilerParams(
            dimension_semantics=("arbitrary",), vmem_limit_bytes=VMEM_LIMIT),
        name="proj",
    )(x2, g, w_cat, wkt_cat)


def _bias_kernel(tab_ref, o_ref):
    h = pl.program_id(0)
    w = pl.program_id(1)
    t = DIFF_BLOCK
    rel = (lax.broadcasted_iota(jnp.int32, (t, t), 0) - lax.broadcasted_iota(jnp.int32, (t, t), 1)
           + w * t)
    n = jnp.maximum(rel, 0)
    max_exact = N_BUCKETS // 2
    nf = jnp.maximum(n, 1).astype(F32)
    large = max_exact + (jnp.log(nf / max_exact) / math.log(MAX_DISTANCE / max_exact)
                         * (N_BUCKETS - max_exact)).astype(jnp.int32)
    large = jnp.minimum(large, N_BUCKETS - 1)
    bucket = jnp.where(n < max_exact, n, large)
    b = jnp.zeros((t, t), F32)
    for k in range(N_BUCKETS):
        b = jnp.where(bucket == k, tab_ref[k, h], b)
    b = (b - tab_ref[N_BUCKETS - 1, h]) * LOG2E
    o_ref[0, 0] = jnp.where(rel >= 0, b, MASK_VALUE)


def _bias_tiles(rel_bias):
    t = DIFF_BLOCK
    return pl.pallas_call(
        _bias_kernel,
        grid=(N_DIFF_HEADS, 2),
        in_specs=[pl.BlockSpec(memory_space=pltpu.SMEM)],
        out_specs=pl.BlockSpec((1, 1, t, t), lambda h, w: (h, w, 0, 0)),
        out_shape=jax.ShapeDtypeStruct((N_DIFF_HEADS, 2, t, t), F32),
        compiler_params=pltpu.CompilerParams(
            dimension_semantics=("arbitrary", "arbitrary"), vmem_limit_bytes=VMEM_LIMIT),
        name="bias",
    )(rel_bias)


def _diff_kernel(lam_ref, q_ref, kt_ref, v_ref, bias_ref, g_ref, o_ref, m_sc, l_sc, acc_sc):
    i = pl.program_id(1)
    t = DIFF_BLOCK
    lane = lax.broadcasted_iota(jnp.int32, (t, LANES), 1)
    q = q_ref[...]
    zero = jnp.zeros_like(q)
    qm = (jnp.where(lane < HEAD_DIM, q, zero), jnp.where(lane >= HEAD_DIM, q, zero))

    m_sc[...] = jnp.full(m_sc.shape, MASK_VALUE, F32)
    l_sc[...] = jnp.zeros(l_sc.shape, F32)
    acc_sc[...] = jnp.zeros(acc_sc.shape, F32)

    def step(j, bias):
        start = pl.multiple_of(j * t, t)
        kt = kt_ref[:, pl.ds(start, t)]
        v = v_ref[pl.ds(start, t), :]
        for mp in range(2):
            s = jnp.dot(qm[mp], kt, preferred_element_type=F32)
            if bias is not None:
                s = s + bias
            m_prev = m_sc[mp]
            m_new = jnp.maximum(m_prev, jnp.max(s, axis=1, keepdims=True))
            alpha = jnp.exp2(m_prev - m_new)
            p = jnp.exp2(s - pltpu.repeat(m_new, t // LANES, axis=1))
            l_sc[mp] = alpha * l_sc[mp] + jnp.sum(p, axis=1, keepdims=True)
            acc_sc[mp] = alpha * acc_sc[mp] + jnp.dot(p.astype(BF16), v, preferred_element_type=F32)
            m_sc[mp] = m_new

    def far_body(j, carry):
        step(j, None)
        return carry

    lax.fori_loop(0, i - 1, far_body, 0)

    @pl.when(i >= 1)
    def _():
        step(i - 1, bias_ref[0, 1])

    step(i, bias_ref[0, 0])

    lp = lam_ref[...]
    lam = (jnp.exp(jnp.sum(lp[0:1] * lp[1:2], axis=1, keepdims=True))
           - jnp.exp(jnp.sum(lp[2:3] * lp[3:4], axis=1, keepdims=True)) + LAMBDA_INIT)
    o = acc_sc[0] / l_sc[0] - lam * (acc_sc[1] / l_sc[1])
    o_ref[...] = (_rms(o, g_ref[...]) * (1.0 - LAMBDA_INIT)).astype(o_ref.dtype)


def _diff_attention(lam_params, qd, kdt, vd, bias, g):
    s = qd.shape[0]
    t = DIFF_BLOCK
    return pl.pallas_call(
        _diff_kernel,
        grid=(N_DIFF_HEADS, s // t),
        in_specs=[
            pl.BlockSpec(lam_params.shape, lambda h, i: (0, 0)),
            pl.BlockSpec((t, LANES), lambda h, i: (i, h)),
            pl.BlockSpec((LANES, s), lambda h, i: (h, 0)),
            pl.BlockSpec((s, LANES), lambda h, i: (0, h)),
            pl.BlockSpec((1, 2, t, t), lambda h, i: (h, 0, 0, 0)),
            pl.BlockSpec(g.shape, lambda h, i: (0, 0)),
        ],
        out_specs=pl.BlockSpec((t, LANES), lambda h, i: (i, h)),
        out_shape=jax.ShapeDtypeStruct((s, DIFF_WIDTH), BF16),
        scratch_shapes=[pltpu.VMEM((2, t, LANES), F32)] * 3,
        compiler_params=pltpu.CompilerParams(
            dimension_semantics=("arbitrary", "arbitrary"), vmem_limit_bytes=VMEM_LIMIT),
        name="diff_attn",
    )(lam_params, qd, kdt, vd, bias, g)


def _sb_kernel(q_ref, kt_ref, v_ref, g_ref, o_ref, carry_sc, acc_sc):
    i = pl.program_id(1)
    tq, tk = SB_Q_BLOCK, SB_K_BLOCK
    lane_q = lax.broadcasted_iota(jnp.int32, (tq, LANES), 1)
    q = q_ref[...]
    zero_q = jnp.zeros_like(q)
    qm = (jnp.where(lane_q < HEAD_DIM, q, zero_q), jnp.where(lane_q >= HEAD_DIM, q, zero_q))
    lane_k = lax.broadcasted_iota(jnp.int32, (tk, LANES), 1)
    lower2 = (lax.rem(lax.broadcasted_iota(jnp.int32, (2 * tk, tk), 0), tk)
              >= lax.broadcasted_iota(jnp.int32, (2 * tk, tk), 1)).astype(BF16)

    carry_sc[...] = jnp.zeros(carry_sc.shape, F32)
    acc_sc[...] = jnp.zeros(acc_sc.shape, F32)

    def step(jb, masked):
        start = pl.multiple_of(jb * tk, tk)
        kt = kt_ref[:, pl.ds(start, tk)]
        v = v_ref[pl.ds(start, tk), :]
        zero_v = jnp.zeros_like(v)
        v2 = jnp.concatenate([jnp.where(lane_k < HEAD_DIM, v, zero_v),
                              jnp.where(lane_k >= HEAD_DIM, v, zero_v)], axis=0)
        if masked:
            strict = (lax.broadcasted_iota(jnp.int32, (tq, tk), 0) + i * tq
                      > lax.broadcasted_iota(jnp.int32, (tq, tk), 1) + jb * tk)
        probs = []
        for hh in range(2):
            z = jnp.dot(qm[hh], kt, preferred_element_type=F32)
            neg_abs = _from_bits(_bits(z) | jnp.uint32(0x80000000))
            nl = jnp.maximum(z, 0.0) + jnp.log(1.0 + jnp.exp2(neg_abs)) * LOG2E
            if masked:
                nl = jnp.where(strict, nl, 0.0)
            hi = _from_bits(_bits(nl) & jnp.uint32(0xFFFF0000))
            hl = jnp.concatenate([hi.astype(BF16), (nl - hi).astype(BF16)], axis=1)
            csum = jnp.dot(hl, lower2, preferred_element_type=F32)
            carry = carry_sc[hh]
            a = jnp.exp2((z - pltpu.repeat(carry, tk // LANES, axis=1)) - csum)
            if masked:
                a = jnp.where(strict, a, 0.0)
            probs.append(a.astype(BF16))
            carry_sc[hh] = carry + csum[:, 0:1]
        acc_sc[...] += jnp.dot(jnp.concatenate(probs, axis=1), v2, preferred_element_type=F32)

    n_diag = tq // tk
    for d in range(n_diag - 1, -1, -1):
        step(i * n_diag + d, True)

    def far_body(r, c):
        for d in range(n_diag):
            step((i - r) * n_diag - 1 - d, False)
        return c

    lax.fori_loop(0, i, far_body, 0)

    o = acc_sc[...]
    sq = o * o
    ss0 = jnp.sum(jnp.where(lane_q < HEAD_DIM, sq, 0.0), axis=1, keepdims=True)
    ss1 = jnp.sum(jnp.where(lane_q >= HEAD_DIM, sq, 0.0), axis=1, keepdims=True)
    ms = jnp.where(lane_q < HEAD_DIM, ss0, ss1) * (1.0 / HEAD_DIM)
    o_ref[...] = (o * lax.rsqrt(ms + RMS_EPS) * g_ref[...]).astype(o_ref.dtype)


def _sb_attention(qs, kst, vs, g2):
    s = qs.shape[0]
    tq = SB_Q_BLOCK
    return pl.pallas_call(
        _sb_kernel,
        grid=(N_SB_HEADS // 2, s // tq),
        in_specs=[
            pl.BlockSpec((tq, LANES), lambda h, i: (i, h)),
            pl.BlockSpec((LANES, s), lambda h, i: (h, 0)),
            pl.BlockSpec((s, LANES), lambda h, i: (0, h)),
            pl.BlockSpec(g2.shape, lambda h, i: (0, 0)),
        ],
        out_specs=pl.BlockSpec((tq, LANES), lambda h, i: (i, h)),
        out_shape=jax.ShapeDtypeStruct((s, SB_WIDTH), BF16),
        scratch_shapes=[pltpu.VMEM((2, tq, LANES), F32), pltpu.VMEM((tq, LANES), F32)],
        compiler_params=pltpu.CompilerParams(
            dimension_semantics=("arbitrary", "arbitrary"), vmem_limit_bytes=VMEM_LIMIT),
        name="sb_attn",
    )(qs, kst, vs, g2)


def _out_kernel(d_ref, s_ref, wd_ref, ws_ref, x_ref, gpost_ref, gpre_ref, x1_ref, h2_ref):
    mix = (jnp.dot(d_ref[...], wd_ref[...], preferred_element_type=F32)
           + jnp.dot(s_ref[...], ws_ref[...], preferred_element_type=F32))
    x1 = x_ref[...] + _rms(mix, gpost_ref[...])
    x1_ref[...] = x1
    h2_ref[...] = _rms(x1, gpre_ref[...]).astype(h2_ref.dtype)


def _out_proj(diff_o, sb_o, w_od, w_os, x2, g_post, g_pre):
    s = x2.shape[0]
    rows = OUT_ROWS
    row_spec = lambda w: pl.BlockSpec((rows, w), lambda i: (i, 0))
    full = lambda a: pl.BlockSpec(a.shape, lambda i: (0, 0))
    return pl.pallas_call(
        _out_kernel,
        grid=(s // rows,),
        in_specs=[row_spec(512), row_spec(512), full(w_od), full(w_os), row_spec(D_MODEL),
                  full(g_post), full(g_pre)],
        out_specs=[row_spec(D_MODEL), row_spec(D_MODEL)],
        out_shape=[jax.ShapeDtypeStruct((s, D_MODEL), F32), jax.ShapeDtypeStruct((s, D_MODEL), BF16)],
        compiler_params=pltpu.CompilerParams(
            dimension_semantics=("arbitrary",), vmem_limit_bytes=VMEM_LIMIT),
        name="out_proj",
    )(diff_o, sb_o, w_od, w_os, x2, g_post, g_pre)


def _gelu_tanh(x):
    return 0.5 * x * (1.0 + jnp.tanh(math.sqrt(2.0 / math.pi) * (x + 0.044715 * (x * x * x))))


def _ffn_kernel(h_ref, halo_ref, wg_ref, wv_ref, cwg_ref, cwv_ref, cbg_ref, cbv_ref, wd_ref,
                x1_ref, g_ref, o_ref, ug_sc, uv_sc, acc_sc):
    i = pl.program_id(0)
    f = pl.program_id(1)
    rows = FFN_ROWS
    h = h_ref[...]
    halo = jnp.where(i > 0, halo_ref[...], jnp.zeros_like(halo_ref[...]))

    def conv(w_ref, cw_ref, cb_ref, u_sc):
        u_sc[0:HALO_ROWS, :] = jnp.dot(halo, w_ref[...], preferred_element_type=F32)
        u_sc[HALO_ROWS:, :] = jnp.dot(h, w_ref[...], preferred_element_type=F32)
        cw = cw_ref[...]
        y = cb_ref[...]
        for k in range(CONV_WIDTH):
            off = HALO_ROWS - (CONV_WIDTH - 1) + k
            y = y + u_sc[off:off + rows, :] * cw[k:k + 1]
        return y

    gate = conv(wg_ref, cwg_ref, cbg_ref, ug_sc)
    val = conv(wv_ref, cwv_ref, cbv_ref, uv_sc)
    act = (_gelu_tanh(gate) * val).astype(BF16)
    part = jnp.dot(act, wd_ref[...], preferred_element_type=F32)

    @pl.when(f == 0)
    def _():
        acc_sc[...] = part

    @pl.when(f > 0)
    def _():
        acc_sc[...] += part

    @pl.when(f == pl.num_programs(1) - 1)
    def _():
        o_ref[...] = x1_ref[...] + _rms(acc_sc[...], g_ref[...])


def _ffn(h2, w_up, conv_w, conv_b, w_down, x1, g_post):
    s = h2.shape[0]
    rows, chunk = FFN_ROWS, FFN_CHUNK
    n_chunk = D_FF // chunk
    halo_blocks = rows // HALO_ROWS
    row_spec = pl.BlockSpec((rows, D_MODEL), lambda i, f: (i, 0))
    halo_spec = pl.BlockSpec((HALO_ROWS, D_MODEL),
                             lambda i, f: (jnp.maximum(i * halo_blocks - 1, 0), 0))
    gate_cols = lambda r: pl.BlockSpec((r, chunk), lambda i, f: (0, f))
    val_cols = lambda r: pl.BlockSpec((r, chunk), lambda i, f: (0, f + n_chunk))
    return pl.pallas_call(
        _ffn_kernel,
        grid=(s // rows, n_chunk),
        in_specs=[row_spec, halo_spec, gate_cols(D_MODEL), val_cols(D_MODEL),
                  gate_cols(CONV_WIDTH), val_cols(CONV_WIDTH), gate_cols(1), val_cols(1),
                  pl.BlockSpec((chunk, D_MODEL), lambda i, f: (f, 0)),
                  row_spec, pl.BlockSpec(g_post.shape, lambda i, f: (0, 0))],
        out_specs=row_spec,
        out_shape=jax.ShapeDtypeStruct((s, D_MODEL), F32),
        scratch_shapes=[pltpu.VMEM((HALO_ROWS + rows, chunk), F32)] * 2
        + [pltpu.VMEM((rows, D_MODEL), F32)],
        compiler_params=pltpu.CompilerParams(
            dimension_semantics=("arbitrary", "arbitrary"), vmem_limit_bytes=VMEM_LIMIT),
        name="ffn",
    )(h2, h2, w_up, w_up, conv_w, conv_w, conv_b, conv_b, w_down, x1, g_post)


def kernel(x, attn_pre_norm, w_qkv, lambda_q1, lambda_k1, lambda_q2, lambda_k2, diff_subln, sb_norm,
           rel_bias, w_o, attn_post_norm, ffn_pre_norm, w_up, conv_w, conv_b, w_down, ffn_post_norm):
    b, s, _ = x.shape
    assert b == 1 and attn_pre_norm.shape[0] == 1, "single sequence, single layer"
    assert s % max(PROJ_ROWS, DIFF_BLOCK, SB_Q_BLOCK, OUT_ROWS, FFN_ROWS) == 0
    x2 = x[0]
    row = lambda a: a.reshape(1, -1).astype(F32)

    wq = w_qkv[0].astype(BF16)
    w_cat = jnp.concatenate([wq[:, 0:512], wq[:, 1024:1536], wq[:, 1536:2048], wq[:, 2560:3072]], axis=1)
    wkt_cat = jnp.concatenate([wq[:, 512:1024], wq[:, 2048:2560]], axis=1).T
    qd, vd, qs, vs, kdt, kst = _proj(x2, row(attn_pre_norm), w_cat, wkt_cat)

    bias = _bias_tiles(rel_bias.astype(F32))
    lam_params = jnp.concatenate([lambda_q1, lambda_k1, lambda_q2, lambda_k2], axis=0).astype(F32)
    diff_o = _diff_attention(lam_params, qd, kdt, vd, bias, row(diff_subln))
    sb_o = _sb_attention(qs, kst, vs, row(jnp.concatenate([sb_norm[0], sb_norm[0]])))

    wo = w_o[0].astype(BF16)
    x1, h2 = _out_proj(diff_o, sb_o, wo[:DIFF_WIDTH], wo[DIFF_WIDTH:], x2,
                       row(attn_post_norm), row(ffn_pre_norm))
    out = _ffn(h2, w_up[0].astype(BF16), conv_w[0].astype(F32), conv_b.astype(F32),
               w_down[0].astype(BF16), x1, row(ffn_post_norm))
    return out[None]
```

```python
import functools
import math

import jax
import jax.numpy as jnp
from jax import lax
from jax.experimental import pallas as pl
from jax.experimental.pallas import tpu as pltpu

F32 = jnp.float32
BF16 = jnp.bfloat16

D_MODEL = 1024
HEAD_DIM = 64
N_DIFF_HEADS = 4
N_SB_HEADS = 8
DIFF_WIDTH = N_DIFF_HEADS * 2 * HEAD_DIM
SB_WIDTH = N_SB_HEADS * HEAD_DIM
D_FF = 2816
CONV_WIDTH = 3
N_BUCKETS = 32
MAX_DISTANCE = 128
RMS_EPS = 1e-6
LAMBDA_INIT = 0.8 - 0.6 * math.exp(-0.3 * 0)

LOG2E = 1.4426950408889634
Q_SCALE = HEAD_DIM ** -0.5 * LOG2E
MASK_VALUE = -1e30

LANES = 128
PROJ_ROWS = 512
DIFF_BLOCK = 512
SB_Q_BLOCK = 512
SB_K_BLOCK = 256
OUT_ROWS = 512
FFN_ROWS = 512
FFN_CHUNK = D_FF // 2
HALO_ROWS = 8
VMEM_LIMIT = 48 * 1024 * 1024


def _bits(x):
    return lax.bitcast_convert_type(x, jnp.uint32)


def _from_bits(u):
    return lax.bitcast_convert_type(u, F32)


def _tile_lanes(x, k):
    return jnp.concatenate([x] * k, axis=1) if k > 1 else x


def _rms(x, g):
    return x * lax.rsqrt(jnp.mean(x * x, axis=-1, keepdims=True) + RMS_EPS) * g


def _proj_kernel(x_ref, g_ref, w_ref, wkt_ref, qd_ref, vd_ref, qs_ref, vs_ref, kdt_ref, kst_ref):
    hb = _rms(x_ref[...], g_ref[...]).astype(BF16)
    o = jnp.dot(hb, w_ref[...], preferred_element_type=F32)
    qd_ref[...] = (o[:, 0:512] * Q_SCALE).astype(BF16)
    vd_ref[...] = o[:, 512:1024].astype(BF16)
    qs_ref[...] = (o[:, 1024:1536] * Q_SCALE).astype(BF16)
    vs_ref[...] = o[:, 1536:2048].astype(BF16)
    kt = lax.dot_general(wkt_ref[...], hb, (((1,), (1,)), ((), ())), preferred_element_type=F32)
    kdt_ref[...] = kt[0:512].astype(BF16)
    kst_ref[...] = kt[512:1024].astype(BF16)


def _proj(x2, g, w_cat, wkt_cat):
    s = x2.shape[0]
    rows = PROJ_ROWS
    row_spec = lambda w: pl.BlockSpec((rows, w), lambda i: (i, 0))
    full = lambda a: pl.BlockSpec(a.shape, lambda i: (0, 0))
    out_rows = jax.ShapeDtypeStruct((s, 512), BF16)
    out_t = jax.ShapeDtypeStruct((512, s), BF16)
    return pl.pallas_call(
        _proj_kernel,
        grid=(s // rows,),
        in_specs=[row_spec(D_MODEL), full(g), full(w_cat), full(wkt_cat)],
        out_specs=[row_spec(512)] * 4 + [pl.BlockSpec((512, rows), lambda i: (0, i))] * 2,
        out_shape=[out_rows] * 4 + [out_t] * 2,
        compiler_params=pltpu.CompilerParams(
            dimension_semantics=("arbitrary",), vmem_limit_bytes=VMEM_LIMIT),
        name="proj",
    )(x2, g, w_cat, wkt_cat)


def _bias_kernel(tab_ref, o_ref):
    h = pl.program_id(0)
    w = pl.program_id(1)
    t = DIFF_BLOCK
    rel = (lax.broadcasted_iota(jnp.int32, (t, t), 0) - lax.broadcasted_iota(jnp.int32, (t, t), 1)
           + w * t)
    n = jnp.maximum(rel, 0)
    max_exact = N_BUCKETS // 2
    nf = jnp.maximum(n, 1).astype(F32)
    large = max_exact + (jnp.log(nf / max_exact) / math.log(MAX_DISTANCE / max_exact)
                         * (N_BUCKETS - max_exact)).astype(jnp.int32)
    large = jnp.minimum(large, N_BUCKETS - 1)
    bucket = jnp.where(n < max_exact, n, large)
    b = jnp.zeros((t, t), F32)
    for k in range(N_BUCKETS):
        b = jnp.where(bucket == k, tab_ref[k, h], b)
    b = (b - tab_ref[N_BUCKETS - 1, h]) * LOG2E
    o_ref[0, 0] = jnp.where(rel >= 0, b, MASK_VALUE)


def _bias_tiles(rel_bias):
    t = DIFF_BLOCK
    return pl.pallas_call(
        _bias_kernel,
        grid=(N_DIFF_HEADS, 2),
        in_specs=[pl.BlockSpec(memory_space=pltpu.SMEM)],
        out_specs=pl.BlockSpec((1, 1, t, t), lambda h, w: (h, w, 0, 0)),
        out_shape=jax.ShapeDtypeStruct((N_DIFF_HEADS, 2, t, t), F32),
        compiler_params=pltpu.CompilerParams(
            dimension_semantics=("arbitrary", "arbitrary"), vmem_limit_bytes=VMEM_LIMIT),
        name="bias",
    )(rel_bias)


def _diff_kernel(lam_ref, q_ref, kt_ref, v_ref, bias_ref, g_ref, o_ref, m_sc, l_sc, acc_sc):
    i = pl.program_id(1)
    t = DIFF_BLOCK
    lane = lax.broadcasted_iota(jnp.int32, (t, LANES), 1)
    q = q_ref[...]
    zero = jnp.zeros_like(q)
    qm = (jnp.where(lane < HEAD_DIM, q, zero), jnp.where(lane >= HEAD_DIM, q, zero))

    m_sc[...] = jnp.full(m_sc.shape, MASK_VALUE, F32)
    l_sc[...] = jnp.zeros(l_sc.shape, F32)
    acc_sc[...] = jnp.zeros(acc_sc.shape, F32)

    def step(j, bias):
        start = pl.multiple_of(j * t, t)
        kt = kt_ref[:, pl.ds(start, t)]
        v = v_ref[pl.ds(start, t), :]
        for mp in range(2):
            s = jnp.dot(qm[mp], kt, preferred_element_type=F32)
            if bias is not None:
                s = s + bias
            m_prev = m_sc[mp]
            m_new = jnp.maximum(m_prev, jnp.max(s, axis=1, keepdims=True))
            alpha = jnp.exp2(m_prev - m_new)
            p = jnp.exp2(s - _tile_lanes(m_new, t // LANES))
            l_sc[mp] = alpha * l_sc[mp] + jnp.sum(p, axis=1, keepdims=True)
            acc_sc[mp] = alpha * acc_sc[mp] + jnp.dot(p.astype(BF16), v, preferred_element_type=F32)
            m_sc[mp] = m_new

    def far_body(r, carry):
        step(2 * r, None)
        step(2 * r + 1, None)
        return carry

    n_far = jnp.maximum(i - 1, 0)
    lax.fori_loop(0, n_far // 2, far_body, 0)

    @pl.when(n_far % 2 == 1)
    def _():
        step(n_far - 1, None)

    @pl.when(i >= 1)
    def _():
        step(i - 1, bias_ref[0, 1])

    step(i, bias_ref[0, 0])

    lp = lam_ref[...]
    lam = (jnp.exp(jnp.sum(lp[0:1] * lp[1:2], axis=1, keepdims=True))
           - jnp.exp(jnp.sum(lp[2:3] * lp[3:4], axis=1, keepdims=True)) + LAMBDA_INIT)
    o = acc_sc[0] / l_sc[0] - lam * (acc_sc[1] / l_sc[1])
    o_ref[...] = (_rms(o, g_ref[...]) * (1.0 - LAMBDA_INIT)).astype(o_ref.dtype)


def _diff_attention(lam_params, qd, kdt, vd, bias, g):
    s = qd.shape[0]
    t = DIFF_BLOCK
    return pl.pallas_call(
        _diff_kernel,
        grid=(N_DIFF_HEADS, s // t),
        in_specs=[
            pl.BlockSpec(lam_params.shape, lambda h, i: (0, 0)),
            pl.BlockSpec((t, LANES), lambda h, i: (i, h)),
            pl.BlockSpec((LANES, s), lambda h, i: (h, 0)),
            pl.BlockSpec((s, LANES), lambda h, i: (0, h)),
            pl.BlockSpec((1, 2, t, t), lambda h, i: (h, 0, 0, 0)),
            pl.BlockSpec(g.shape, lambda h, i: (0, 0)),
        ],
        out_specs=pl.BlockSpec((t, LANES), lambda h, i: (i, h)),
        out_shape=jax.ShapeDtypeStruct((s, DIFF_WIDTH), BF16),
        scratch_shapes=[pltpu.VMEM((2, t, LANES), F32)] * 3,
        compiler_params=pltpu.CompilerParams(
            dimension_semantics=("arbitrary", "arbitrary"), vmem_limit_bytes=VMEM_LIMIT),
        name="diff_attn",
    )(lam_params, qd, kdt, vd, bias, g)


def _sb_kernel(q_ref, kt_ref, v_ref, g_ref, o_ref, carry_sc, acc_sc):
    i = pl.program_id(1)
    tq, tk = SB_Q_BLOCK, SB_K_BLOCK
    lane_q = lax.broadcasted_iota(jnp.int32, (tq, LANES), 1)
    q = q_ref[...]
    zero_q = jnp.zeros_like(q)
    qm = (jnp.where(lane_q < HEAD_DIM, q, zero_q), jnp.where(lane_q >= HEAD_DIM, q, zero_q))
    lane_k = lax.broadcasted_iota(jnp.int32, (tk, LANES), 1)
    lower = (lax.broadcasted_iota(jnp.int32, (tk, tk), 0)
             >= lax.broadcasted_iota(jnp.int32, (tk, tk), 1)).astype(BF16)

    carry_sc[...] = jnp.zeros(carry_sc.shape, F32)
    acc_sc[...] = jnp.zeros(acc_sc.shape, F32)

    def step(jb, masked):
        start = pl.multiple_of(jb * tk, tk)
        kt = kt_ref[:, pl.ds(start, tk)]
        v = v_ref[pl.ds(start, tk), :]
        zero_v = jnp.zeros_like(v)
        v2 = jnp.concatenate([jnp.where(lane_k < HEAD_DIM, v, zero_v),
                              jnp.where(lane_k >= HEAD_DIM, v, zero_v)], axis=0)
        if masked:
            strict = (lax.broadcasted_iota(jnp.int32, (tq, tk), 0) + i * tq
                      > lax.broadcasted_iota(jnp.int32, (tq, tk), 1) + jb * tk)
        probs = []
        for hh in range(2):
            z = jnp.dot(qm[hh], kt, preferred_element_type=F32)
            neg_abs = _from_bits(_bits(z) | jnp.uint32(0x80000000))
            nl = jnp.maximum(z, 0.0) + jnp.log(1.0 + jnp.exp2(neg_abs)) * LOG2E
            if masked:
                nl = jnp.where(strict, nl, 0.0)
            csum = jnp.dot(nl.astype(BF16), lower, preferred_element_type=F32)
            carry = carry_sc[hh]
            a = jnp.exp2((z - _tile_lanes(carry, tk // LANES)) - csum)
            if masked:
                a = jnp.where(strict, a, 0.0)
            probs.append(a.astype(BF16))
            carry_sc[hh] = carry + csum[:, 0:1]
        acc_sc[...] += jnp.dot(jnp.concatenate(probs, axis=1), v2, preferred_element_type=F32)

    n_diag = tq // tk
    for d in range(n_diag - 1, -1, -1):
        step(i * n_diag + d, True)

    def far_body(r, c):
        for d in range(n_diag):
            step((i - r) * n_diag - 1 - d, False)
        return c

    lax.fori_loop(0, i, far_body, 0)

    o = acc_sc[...]
    sq = o * o
    ss0 = jnp.sum(jnp.where(lane_q < HEAD_DIM, sq, 0.0), axis=1, keepdims=True)
    ss1 = jnp.sum(jnp.where(lane_q >= HEAD_DIM, sq, 0.0), axis=1, keepdims=True)
    ms = jnp.where(lane_q < HEAD_DIM, ss0, ss1) * (1.0 / HEAD_DIM)
    o_ref[...] = (o * lax.rsqrt(ms + RMS_EPS) * g_ref[...]).astype(o_ref.dtype)


def _sb_attention(qs, kst, vs, g2):
    s = qs.shape[0]
    tq = SB_Q_BLOCK
    return pl.pallas_call(
        _sb_kernel,
        grid=(N_SB_HEADS // 2, s // tq),
        in_specs=[
            pl.BlockSpec((tq, LANES), lambda h, i: (i, h)),
            pl.BlockSpec((LANES, s), lambda h, i: (h, 0)),
            pl.BlockSpec((s, LANES), lambda h, i: (0, h)),
            pl.BlockSpec(g2.shape, lambda h, i: (0, 0)),
        ],
        out_specs=pl.BlockSpec((tq, LANES), lambda h, i: (i, h)),
        out_shape=jax.ShapeDtypeStruct((s, SB_WIDTH), BF16),
        scratch_shapes=[pltpu.VMEM((2, tq, LANES), F32), pltpu.VMEM((tq, LANES), F32)],
        compiler_params=pltpu.CompilerParams(
            dimension_semantics=("arbitrary", "arbitrary"), vmem_limit_bytes=VMEM_LIMIT),
        name="sb_attn",
    )(qs, kst, vs, g2)


def _out_kernel(d_ref, s_ref, wd_ref, ws_ref, x_ref, gpost_ref, gpre_ref, x1_ref, h2_ref):
    mix = (jnp.dot(d_ref[...], wd_ref[...], preferred_element_type=F32)
           + jnp.dot(s_ref[...], ws_ref[...], preferred_element_type=F32))
    x1 = x_ref[...] + _rms(mix, gpost_ref[...])
    x1_ref[...] = x1
    h2_ref[...] = _rms(x1, gpre_ref[...]).astype(h2_ref.dtype)


def _out_proj(diff_o, sb_o, w_od, w_os, x2, g_post, g_pre):
    s = x2.shape[0]
    rows = OUT_ROWS
    row_spec = lambda w: pl.BlockSpec((rows, w), lambda i: (i, 0))
    full = lambda a: pl.BlockSpec(a.shape, lambda i: (0, 0))
    return pl.pallas_call(
        _out_kernel,
        grid=(s // rows,),
        in_specs=[row_spec(512), row_spec(512), full(w_od), full(w_os), row_spec(D_MODEL),
                  full(g_post), full(g_pre)],
        out_specs=[row_spec(D_MODEL), row_spec(D_MODEL)],
        out_shape=[jax.ShapeDtypeStruct((s, D_MODEL), F32), jax.ShapeDtypeStruct((s, D_MODEL), BF16)],
        compiler_params=pltpu.CompilerParams(
            dimension_semantics=("arbitrary",), vmem_limit_bytes=VMEM_LIMIT),
        name="out_proj",
    )(diff_o, sb_o, w_od, w_os, x2, g_post, g_pre)


def _gelu_tanh(x):
    return 0.5 * x * (1.0 + jnp.tanh(math.sqrt(2.0 / math.pi) * (x + 0.044715 * (x * x * x))))


def _ffn_kernel(h_ref, halo_ref, wg_ref, wv_ref, cwg_ref, cwv_ref, cbg_ref, cbv_ref, wd_ref,
                x1_ref, g_ref, o_ref, ug_sc, uv_sc, acc_sc):
    i = pl.program_id(0)
    f = pl.program_id(1)
    rows = FFN_ROWS
    h = h_ref[...]
    halo = jnp.where(i > 0, halo_ref[...], jnp.zeros_like(halo_ref[...]))

    def conv(w_ref, cw_ref, cb_ref, u_sc):
        u_sc[0:HALO_ROWS, :] = jnp.dot(halo, w_ref[...], preferred_element_type=F32)
        u_sc[HALO_ROWS:, :] = jnp.dot(h, w_ref[...], preferred_element_type=F32)
        cw = cw_ref[...]
        y = cb_ref[...]
        for k in range(CONV_WIDTH):
            off = HALO_ROWS - (CONV_WIDTH - 1) + k
            y = y + u_sc[off:off + rows, :] * cw[k:k + 1]
        return y

    gate = conv(wg_ref, cwg_ref, cbg_ref, ug_sc)
    val = conv(wv_ref, cwv_ref, cbv_ref, uv_sc)
    act = (_gelu_tanh(gate) * val).astype(BF16)
    part = jnp.dot(act, wd_ref[...], preferred_element_type=F32)

    @pl.when(f == 0)
    def _():
        acc_sc[...] = part

    @pl.when(f > 0)
    def _():
        acc_sc[...] += part

    @pl.when(f == pl.num_programs(1) - 1)
    def _():
        o_ref[...] = x1_ref[...] + _rms(acc_sc[...], g_ref[...])


def _ffn(h2, w_up, conv_w, conv_b, w_down, x1, g_post):
    s = h2.shape[0]
    rows, chunk = FFN_ROWS, FFN_CHUNK
    n_chunk = D_FF // chunk
    halo_blocks = rows // HALO_ROWS
    row_spec = pl.BlockSpec((rows, D_MODEL), lambda i, f: (i, 0))
    halo_spec = pl.BlockSpec((HALO_ROWS, D_MODEL),
                             lambda i, f: (jnp.maximum(i * halo_blocks - 1, 0), 0))
    gate_cols = lambda r: pl.BlockSpec((r, chunk), lambda i, f: (0, f))
    val_cols = lambda r: pl.BlockSpec((r, chunk), lambda i, f: (0, f + n_chunk))
    return pl.pallas_call(
        _ffn_kernel,
        grid=(s // rows, n_chunk),
        in_specs=[row_spec, halo_spec, gate_cols(D_MODEL), val_cols(D_MODEL),
                  gate_cols(CONV_WIDTH), val_cols(CONV_WIDTH), gate_cols(1), val_cols(1),
                  pl.BlockSpec((chunk, D_MODEL), lambda i, f: (f, 0)),
                  row_spec, pl.BlockSpec(g_post.shape, lambda i, f: (0, 0))],
        out_specs=row_spec,
        out_shape=jax.ShapeDtypeStruct((s, D_MODEL), F32),
        scratch_shapes=[pltpu.VMEM((HALO_ROWS + rows, chunk), F32)] * 2
        + [pltpu.VMEM((rows, D_MODEL), F32)],
        compiler_params=pltpu.CompilerParams(
            dimension_semantics=("arbitrary", "arbitrary"), vmem_limit_bytes=VMEM_LIMIT),
        name="ffn",
    )(h2, h2, w_up, w_up, conv_w, conv_w, conv_b, conv_b, w_down, x1, g_post)


def kernel(x, attn_pre_norm, w_qkv, lambda_q1, lambda_k1, lambda_q2, lambda_k2, diff_subln, sb_norm,
           rel_bias, w_o, attn_post_norm, ffn_pre_norm, w_up, conv_w, conv_b, w_down, ffn_post_norm):
    b, s, _ = x.shape
    assert b == 1 and attn_pre_norm.shape[0] == 1, "single sequence, single layer"
    assert s % max(PROJ_ROWS, DIFF_BLOCK, SB_Q_BLOCK, OUT_ROWS, FFN_ROWS) == 0
    x2 = x[0]
    row = lambda a: a.reshape(1, -1).astype(F32)

    wq = w_qkv[0].astype(BF16)
    w_cat = jnp.concatenate([wq[:, 0:512], wq[:, 1024:1536], wq[:, 1536:2048], wq[:, 2560:3072]], axis=1)
    wkt_cat = jnp.concatenate([wq[:, 512:1024], wq[:, 2048:2560]], axis=1).T
    qd, vd, qs, vs, kdt, kst = _proj(x2, row(attn_pre_norm), w_cat, wkt_cat)

    bias = _bias_tiles(rel_bias.astype(F32))
    lam_params = jnp.concatenate([lambda_q1, lambda_k1, lambda_q2, lambda_k2], axis=0).astype(F32)
    diff_o = _diff_attention(lam_params, qd, kdt, vd, bias, row(diff_subln))
    sb_o = _sb_attention(qs, kst, vs, row(jnp.concatenate([sb_norm[0], sb_norm[0]])))

    wo = w_o[0].astype(BF16)
    x1, h2 = _out_proj(diff_o, sb_o, wo[:DIFF_WIDTH], wo[DIFF_WIDTH:], x2,
                       row(attn_post_norm), row(ffn_pre_norm))
    out = _ffn(h2, w_up[0].astype(BF16), conv_w[0].astype(F32), conv_b.astype(F32),
               w_down[0].astype(BF16), x1, row(ffn_post_norm))
    return out[None]
```

```python
import functools
import math

import jax
import jax.numpy as jnp
from jax import lax
from jax.experimental import pallas as pl
from jax.experimental.pallas import tpu as pltpu

F32 = jnp.float32
BF16 = jnp.bfloat16

D_MODEL = 1024
HEAD_DIM = 64
N_DIFF_HEADS = 4
N_SB_HEADS = 8
DIFF_WIDTH = N_DIFF_HEADS * 2 * HEAD_DIM
SB_WIDTH = N_SB_HEADS * HEAD_DIM
D_FF = 2816
CONV_WIDTH = 3
N_BUCKETS = 32
MAX_DISTANCE = 128
RMS_EPS = 1e-6
LAMBDA_INIT = 0.8 - 0.6 * math.exp(-0.3 * 0)

LOG2E = 1.4426950408889634
Q_SCALE = HEAD_DIM ** -0.5 * LOG2E
MASK_VALUE = -1e30
EXP2_UNDERFLOW = 160.0

LANES = 128
PROJ_ROWS = 512
DIFF_BLOCK = 512
SB_Q_BLOCK = 512
SB_K_BLOCK = 256
OUT_ROWS = 512
FFN_ROWS = 512
FFN_CHUNK = D_FF // 2
HALO_ROWS = 8
VMEM_LIMIT = 48 * 1024 * 1024


def _bits(x):
    return lax.bitcast_convert_type(x, jnp.uint32)


def _from_bits(u):
    return lax.bitcast_convert_type(u, F32)


def _tile_lanes(x, k):
    return jnp.concatenate([x] * k, axis=1) if k > 1 else x


def _rms(x, g):
    return x * lax.rsqrt(jnp.mean(x * x, axis=-1, keepdims=True) + RMS_EPS) * g


def _proj_kernel(x_ref, g_ref, w_ref, wkt_ref, qd_ref, vd_ref, qs_ref, vs_ref, kdt_ref, kst_ref):
    hb = _rms(x_ref[...], g_ref[...]).astype(BF16)
    o = jnp.dot(hb, w_ref[...], preferred_element_type=F32)
    qd_ref[...] = (o[:, 0:512] * Q_SCALE).astype(BF16)
    vd_ref[...] = o[:, 512:1024].astype(BF16)
    qs_ref[...] = (o[:, 1024:1536] * Q_SCALE).astype(BF16)
    vs_ref[...] = o[:, 1536:2048].astype(BF16)
    kt = lax.dot_general(wkt_ref[...], hb, (((1,), (1,)), ((), ())), preferred_element_type=F32)
    kdt_ref[...] = kt[0:512].astype(BF16)
    kst_ref[...] = kt[512:1024].astype(BF16)


def _proj(x2, g, w_cat, wkt_cat):
    s = x2.shape[0]
    rows = PROJ_ROWS
    row_spec = lambda w: pl.BlockSpec((rows, w), lambda i: (i, 0))
    full = lambda a: pl.BlockSpec(a.shape, lambda i: (0, 0))
    out_rows = jax.ShapeDtypeStruct((s, 512), BF16)
    out_t = jax.ShapeDtypeStruct((512, s), BF16)
    return pl.pallas_call(
        _proj_kernel,
        grid=(s // rows,),
        in_specs=[row_spec(D_MODEL), full(g), full(w_cat), full(wkt_cat)],
        out_specs=[row_spec(512)] * 4 + [pl.BlockSpec((512, rows), lambda i: (0, i))] * 2,
        out_shape=[out_rows] * 4 + [out_t] * 2,
        compiler_params=pltpu.CompilerParams(
            dimension_semantics=("arbitrary",), vmem_limit_bytes=VMEM_LIMIT),
        name="proj",
    )(x2, g, w_cat, wkt_cat)


def _bias_kernel(tab_ref, o_ref):
    h = pl.program_id(0)
    w = pl.program_id(1)
    t = DIFF_BLOCK
    rel = (lax.broadcasted_iota(jnp.int32, (t, t), 0) - lax.broadcasted_iota(jnp.int32, (t, t), 1)
           + w * t)
    n = jnp.maximum(rel, 0)
    max_exact = N_BUCKETS // 2
    nf = jnp.maximum(n, 1).astype(F32)
    large = max_exact + (jnp.log(nf / max_exact) / math.log(MAX_DISTANCE / max_exact)
                         * (N_BUCKETS - max_exact)).astype(jnp.int32)
    large = jnp.minimum(large, N_BUCKETS - 1)
    bucket = jnp.where(n < max_exact, n, large)
    b = jnp.zeros((t, t), F32)
    for k in range(N_BUCKETS):
        b = jnp.where(bucket == k, tab_ref[k, h], b)
    b = (b - tab_ref[N_BUCKETS - 1, h]) * LOG2E
    o_ref[0, 0] = jnp.where(rel >= 0, b, MASK_VALUE)


def _bias_tiles(rel_bias):
    t = DIFF_BLOCK
    return pl.pallas_call(
        _bias_kernel,
        grid=(N_DIFF_HEADS, 2),
        in_specs=[pl.BlockSpec(memory_space=pltpu.SMEM)],
        out_specs=pl.BlockSpec((1, 1, t, t), lambda h, w: (h, w, 0, 0)),
        out_shape=jax.ShapeDtypeStruct((N_DIFF_HEADS, 2, t, t), F32),
        compiler_params=pltpu.CompilerParams(
            dimension_semantics=("arbitrary", "arbitrary"), vmem_limit_bytes=VMEM_LIMIT),
        name="bias",
    )(rel_bias)


def _diff_kernel(lam_ref, q_ref, kt_ref, v_ref, bias_ref, g_ref, o_ref, m_sc, l_sc, acc_sc):
    i = pl.program_id(1)
    t = DIFF_BLOCK
    lane = lax.broadcasted_iota(jnp.int32, (t, LANES), 1)
    q = q_ref[...]
    zero = jnp.zeros_like(q)
    qm = (jnp.where(lane < HEAD_DIM, q, zero), jnp.where(lane >= HEAD_DIM, q, zero))

    m_sc[...] = jnp.full(m_sc.shape, MASK_VALUE, F32)
    l_sc[...] = jnp.zeros(l_sc.shape, F32)
    acc_sc[...] = jnp.zeros(acc_sc.shape, F32)

    def step(j, bias):
        start = pl.multiple_of(j * t, t)
        kt = kt_ref[:, pl.ds(start, t)]
        v = v_ref[pl.ds(start, t), :]
        for mp in range(2):
            s = jnp.dot(qm[mp], kt, preferred_element_type=F32)
            if bias is not None:
                s = s + bias
            m_prev = m_sc[mp]
            m_new = jnp.maximum(m_prev, jnp.max(s, axis=1, keepdims=True))
            alpha = jnp.exp2(m_prev - m_new)
            p = jnp.exp2(s - _tile_lanes(m_new, t // LANES))
            l_sc[mp] = alpha * l_sc[mp] + jnp.sum(p, axis=1, keepdims=True)
            acc_sc[mp] = alpha * acc_sc[mp] + jnp.dot(p.astype(BF16), v, preferred_element_type=F32)
            m_sc[mp] = m_new

    def far_body(r, carry):
        step(2 * r, None)
        step(2 * r + 1, None)
        return carry

    n_far = jnp.maximum(i - 1, 0)
    lax.fori_loop(0, n_far // 2, far_body, 0)

    @pl.when(n_far % 2 == 1)
    def _():
        step(n_far - 1, None)

    @pl.when(i >= 1)
    def _():
        step(i - 1, bias_ref[0, 1])

    step(i, bias_ref[0, 0])

    lp = lam_ref[...]
    lam = (jnp.exp(jnp.sum(lp[0:1] * lp[1:2], axis=1, keepdims=True))
           - jnp.exp(jnp.sum(lp[2:3] * lp[3:4], axis=1, keepdims=True)) + LAMBDA_INIT)
    o = acc_sc[0] / l_sc[0] - lam * (acc_sc[1] / l_sc[1])
    o_ref[...] = (_rms(o, g_ref[...]) * (1.0 - LAMBDA_INIT)).astype(o_ref.dtype)


def _diff_attention(lam_params, qd, kdt, vd, bias, g):
    s = qd.shape[0]
    t = DIFF_BLOCK
    return pl.pallas_call(
        _diff_kernel,
        grid=(N_DIFF_HEADS, s // t),
        in_specs=[
            pl.BlockSpec(lam_params.shape, lambda h, i: (0, 0)),
            pl.BlockSpec((t, LANES), lambda h, i: (i, h)),
            pl.BlockSpec((LANES, s), lambda h, i: (h, 0)),
            pl.BlockSpec((s, LANES), lambda h, i: (0, h)),
            pl.BlockSpec((1, 2, t, t), lambda h, i: (h, 0, 0, 0)),
            pl.BlockSpec(g.shape, lambda h, i: (0, 0)),
        ],
        out_specs=pl.BlockSpec((t, LANES), lambda h, i: (i, h)),
        out_shape=jax.ShapeDtypeStruct((s, DIFF_WIDTH), BF16),
        scratch_shapes=[pltpu.VMEM((2, t, LANES), F32)] * 3,
        compiler_params=pltpu.CompilerParams(
            dimension_semantics=("arbitrary", "arbitrary"), vmem_limit_bytes=VMEM_LIMIT),
        name="diff_attn",
    )(lam_params, qd, kdt, vd, bias, g)


def _sb_kernel(q_ref, kt_ref, v_ref, g_ref, o_ref, carry_sc, acc_sc, knorm_sc):
    i = pl.program_id(1)
    tq, tk = SB_Q_BLOCK, SB_K_BLOCK
    lane_q = lax.broadcasted_iota(jnp.int32, (tq, LANES), 1)
    q = q_ref[...]
    zero_q = jnp.zeros_like(q)
    qm = (jnp.where(lane_q < HEAD_DIM, q, zero_q), jnp.where(lane_q >= HEAD_DIM, q, zero_q))
    lane_k = lax.broadcasted_iota(jnp.int32, (tk, LANES), 1)
    lower = (lax.broadcasted_iota(jnp.int32, (tk, tk), 0)
             >= lax.broadcasted_iota(jnp.int32, (tk, tk), 1)).astype(BF16)

    carry_sc[...] = jnp.zeros(carry_sc.shape, F32)
    acc_sc[...] = jnp.zeros(acc_sc.shape, F32)

    @pl.when(i == 0)
    def _():
        def norm_body(c, m):
            k = kt_ref[:, pl.ds(pl.multiple_of(c * tq, tq), tq)].astype(F32)
            sq = k * k
            return jnp.maximum(m, jnp.maximum(jnp.sum(sq[:HEAD_DIM], axis=0, keepdims=True),
                                              jnp.sum(sq[HEAD_DIM:], axis=0, keepdims=True)))

        m = lax.fori_loop(0, kt_ref.shape[1] // tq, norm_body, jnp.zeros((1, tq), F32))
        knorm_sc[...] = jnp.broadcast_to(jnp.max(m, axis=1, keepdims=True), knorm_sc.shape)

    qsq = q.astype(F32) * q.astype(F32)
    qn0 = jnp.sum(jnp.where(lane_q < HEAD_DIM, qsq, 0.0), axis=1, keepdims=True)
    qn1 = jnp.sum(jnp.where(lane_q >= HEAD_DIM, qsq, 0.0), axis=1, keepdims=True)
    qnorm = jnp.max(jnp.maximum(qn0, qn1), axis=0, keepdims=True)
    thresh = jnp.sqrt(qnorm * knorm_sc[0:1, 0:1]) * (1.0 + 2.0 ** -10) + EXP2_UNDERFLOW

    def step(jb, masked):
        start = pl.multiple_of(jb * tk, tk)
        kt = kt_ref[:, pl.ds(start, tk)]
        v = v_ref[pl.ds(start, tk), :]
        zero_v = jnp.zeros_like(v)
        v2 = jnp.concatenate([jnp.where(lane_k < HEAD_DIM, v, zero_v),
                              jnp.where(lane_k >= HEAD_DIM, v, zero_v)], axis=0)
        if masked:
            strict = (lax.broadcasted_iota(jnp.int32, (tq, tk), 0) + i * tq
                      > lax.broadcasted_iota(jnp.int32, (tq, tk), 1) + jb * tk)
        probs = []
        for hh in range(2):
            z = jnp.dot(qm[hh], kt, preferred_element_type=F32)
            neg_abs = _from_bits(_bits(z) | jnp.uint32(0x80000000))
            nl = jnp.maximum(z, 0.0) + jnp.log(1.0 + jnp.exp2(neg_abs)) * LOG2E
            if masked:
                nl = jnp.where(strict, nl, 0.0)
            csum = jnp.dot(nl.astype(BF16), lower, preferred_element_type=F32)
            carry = carry_sc[hh]
            a = jnp.exp2((z - _tile_lanes(carry, tk // LANES)) - csum)
            if masked:
                a = jnp.where(strict, a, 0.0)
            probs.append(a.astype(BF16))
            carry_sc[hh] = carry + csum[:, 0:1]
        acc_sc[...] += jnp.dot(jnp.concatenate(probs, axis=1), v2, preferred_element_type=F32)

    n_diag = tq // tk
    for d in range(n_diag - 1, -1, -1):
        step(i * n_diag + d, True)

    def far_cond(state):
        r, all_zero = state
        return jnp.logical_and(r < i, all_zero == 0)

    def far_body(state):
        r, _ = state
        for d in range(n_diag):
            step((i - r) * n_diag - 1 - d, False)
        slack = jnp.min(jnp.minimum(carry_sc[0], carry_sc[1]) - thresh)
        return r + 1, (slack >= 0.0).astype(jnp.int32)

    lax.while_loop(far_cond, far_body, (jnp.int32(0), jnp.int32(0)))

    o = acc_sc[...]
    sq = o * o
    ss0 = jnp.sum(jnp.where(lane_q < HEAD_DIM, sq, 0.0), axis=1, keepdims=True)
    ss1 = jnp.sum(jnp.where(lane_q >= HEAD_DIM, sq, 0.0), axis=1, keepdims=True)
    ms = jnp.where(lane_q < HEAD_DIM, ss0, ss1) * (1.0 / HEAD_DIM)
    o_ref[...] = (o * lax.rsqrt(ms + RMS_EPS) * g_ref[...]).astype(o_ref.dtype)


def _sb_attention(qs, kst, vs, g2):
    s = qs.shape[0]
    tq = SB_Q_BLOCK
    return pl.pallas_call(
        _sb_kernel,
        grid=(N_SB_HEADS // 2, s // tq),
        in_specs=[
            pl.BlockSpec((tq, LANES), lambda h, i: (i, h)),
            pl.BlockSpec((LANES, s), lambda h, i: (h, 0)),
            pl.BlockSpec((s, LANES), lambda h, i: (0, h)),
            pl.BlockSpec(g2.shape, lambda h, i: (0, 0)),
        ],
        out_specs=pl.BlockSpec((tq, LANES), lambda h, i: (i, h)),
        out_shape=jax.ShapeDtypeStruct((s, SB_WIDTH), BF16),
        scratch_shapes=[pltpu.VMEM((2, tq, LANES), F32), pltpu.VMEM((tq, LANES), F32),
                        pltpu.VMEM((8, LANES), F32)],
        compiler_params=pltpu.CompilerParams(
            dimension_semantics=("arbitrary", "arbitrary"), vmem_limit_bytes=VMEM_LIMIT),
        name="sb_attn",
    )(qs, kst, vs, g2)


def _out_kernel(d_ref, s_ref, wd_ref, ws_ref, x_ref, gpost_ref, gpre_ref, x1_ref, h2_ref):
    mix = (jnp.dot(d_ref[...], wd_ref[...], preferred_element_type=F32)
           + jnp.dot(s_ref[...], ws_ref[...], preferred_element_type=F32))
    x1 = x_ref[...] + _rms(mix, gpost_ref[...])
    x1_ref[...] = x1
    h2_ref[...] = _rms(x1, gpre_ref[...]).astype(h2_ref.dtype)


def _out_proj(diff_o, sb_o, w_od, w_os, x2, g_post, g_pre):
    s = x2.shape[0]
    rows = OUT_ROWS
    row_spec = lambda w: pl.BlockSpec((rows, w), lambda i: (i, 0))
    full = lambda a: pl.BlockSpec(a.shape, lambda i: (0, 0))
    return pl.pallas_call(
        _out_kernel,
        grid=(s // rows,),
        in_specs=[row_spec(512), row_spec(512), full(w_od), full(w_os), row_spec(D_MODEL),
                  full(g_post), full(g_pre)],
        out_specs=[row_spec(D_MODEL), row_spec(D_MODEL)],
        out_shape=[jax.ShapeDtypeStruct((s, D_MODEL), F32), jax.ShapeDtypeStruct((s, D_MODEL), BF16)],
        compiler_params=pltpu.CompilerParams(
            dimension_semantics=("arbitrary",), vmem_limit_bytes=VMEM_LIMIT),
        name="out_proj",
    )(diff_o, sb_o, w_od, w_os, x2, g_post, g_pre)


def _gelu_tanh(x):
    return 0.5 * x * (1.0 + jnp.tanh(math.sqrt(2.0 / math.pi) * (x + 0.044715 * (x * x * x))))


def _ffn_kernel(h_ref, halo_ref, wg_ref, wv_ref, cwg_ref, cwv_ref, cbg_ref, cbv_ref, wd_ref,
                x1_ref, g_ref, o_ref, ug_sc, uv_sc, acc_sc):
    i = pl.program_id(0)
    f = pl.program_id(1)
    rows = FFN_ROWS
    h = h_ref[...]
    halo = jnp.where(i > 0, halo_ref[...], jnp.zeros_like(halo_ref[...]))

    def conv(w_ref, cw_ref, cb_ref, u_sc):
        u_sc[0:HALO_ROWS, :] = jnp.dot(halo, w_ref[...], preferred_element_type=F32)
        u_sc[HALO_ROWS:, :] = jnp.dot(h, w_ref[...], preferred_element_type=F32)
        cw = cw_ref[...]
        y = cb_ref[...]
        for k in range(CONV_WIDTH):
            off = HALO_ROWS - (CONV_WIDTH - 1) + k
            y = y + u_sc[off:off + rows, :] * cw[k:k + 1]
        return y

    gate = conv(wg_ref, cwg_ref, cbg_ref, ug_sc)
    val = conv(wv_ref, cwv_ref, cbv_ref, uv_sc)
    act = (_gelu_tanh(gate) * val).astype(BF16)
    part = jnp.dot(act, wd_ref[...], preferred_element_type=F32)

    @pl.when(f == 0)
    def _():
        acc_sc[...] = part

    @pl.when(f > 0)
    def _():
        acc_sc[...] += part

    @pl.when(f == pl.num_programs(1) - 1)
    def _():
        o_ref[...] = x1_ref[...] + _rms(acc_sc[...], g_ref[...])


def _ffn(h2, w_up, conv_w, conv_b, w_down, x1, g_post):
    s = h2.shape[0]
    rows, chunk = FFN_ROWS, FFN_CHUNK
    n_chunk = D_FF // chunk
    halo_blocks = rows // HALO_ROWS
    row_spec = pl.BlockSpec((rows, D_MODEL), lambda i, f: (i, 0))
    halo_spec = pl.BlockSpec((HALO_ROWS, D_MODEL),
                             lambda i, f: (jnp.maximum(i * halo_blocks - 1, 0), 0))
    gate_cols = lambda r: pl.BlockSpec((r, chunk), lambda i, f: (0, f))
    val_cols = lambda r: pl.BlockSpec((r, chunk), lambda i, f: (0, f + n_chunk))
    return pl.pallas_call(
        _ffn_kernel,
        grid=(s // rows, n_chunk),
        in_specs=[row_spec, halo_spec, gate_cols(D_MODEL), val_cols(D_MODEL),
                  gate_cols(CONV_WIDTH), val_cols(CONV_WIDTH), gate_cols(1), val_cols(1),
                  pl.BlockSpec((chunk, D_MODEL), lambda i, f: (f, 0)),
                  row_spec, pl.BlockSpec(g_post.shape, lambda i, f: (0, 0))],
        out_specs=row_spec,
        out_shape=jax.ShapeDtypeStruct((s, D_MODEL), F32),
        scratch_shapes=[pltpu.VMEM((HALO_ROWS + rows, chunk), F32)] * 2
        + [pltpu.VMEM((rows, D_MODEL), F32)],
        compiler_params=pltpu.CompilerParams(
            dimension_semantics=("arbitrary", "arbitrary"), vmem_limit_bytes=VMEM_LIMIT),
        name="ffn",
    )(h2, h2, w_up, w_up, conv_w, conv_w, conv_b, conv_b, w_down, x1, g_post)


def kernel(x, attn_pre_norm, w_qkv, lambda_q1, lambda_k1, lambda_q2, lambda_k2, diff_subln, sb_norm,
           rel_bias, w_o, attn_post_norm, ffn_pre_norm, w_up, conv_w, conv_b, w_down, ffn_post_norm):
    b, s, _ = x.shape
    assert b == 1 and attn_pre_norm.shape[0] == 1, "single sequence, single layer"
    assert s % max(PROJ_ROWS, DIFF_BLOCK, SB_Q_BLOCK, OUT_ROWS, FFN_ROWS) == 0
    x2 = x[0]
    row = lambda a: a.reshape(1, -1).astype(F32)

    wq = w_qkv[0].astype(BF16)
    w_cat = jnp.concatenate([wq[:, 0:512], wq[:, 1024:1536], wq[:, 1536:2048], wq[:, 2560:3072]], axis=1)
    wkt_cat = jnp.concatenate([wq[:, 512:1024], wq[:, 2048:2560]], axis=1).T
    qd, vd, qs, vs, kdt, kst = _proj(x2, row(attn_pre_norm), w_cat, wkt_cat)

    bias = _bias_tiles(rel_bias.astype(F32))
    lam_params = jnp.concatenate([lambda_q1, lambda_k1, lambda_q2, lambda_k2], axis=0).astype(F32)
    diff_o = _diff_attention(lam_params, qd, kdt, vd, bias, row(diff_subln))
    sb_o = _sb_attention(qs, kst, vs, row(jnp.concatenate([sb_norm[0], sb_norm[0]])))

    wo = w_o[0].astype(BF16)
    x1, h2 = _out_proj(diff_o, sb_o, wo[:DIFF_WIDTH], wo[DIFF_WIDTH:], x2,
                       row(attn_post_norm), row(ffn_pre_norm))
    out = _ffn(h2, w_up[0].astype(BF16), conv_w[0].astype(F32), conv_b.astype(F32),
               w_down[0].astype(BF16), x1, row(ffn_post_norm))
    return out[None]
```

```python
import functools
import math

import jax
import jax.numpy as jnp
from jax import lax
from jax.experimental import pallas as pl
from jax.experimental.pallas import tpu as pltpu

F32 = jnp.float32
BF16 = jnp.bfloat16

D_MODEL = 1024
HEAD_DIM = 64
N_DIFF_HEADS = 4
N_SB_HEADS = 8
DIFF_WIDTH = N_DIFF_HEADS * 2 * HEAD_DIM
SB_WIDTH = N_SB_HEADS * HEAD_DIM
D_FF = 2816
CONV_WIDTH = 3
N_BUCKETS = 32
MAX_DISTANCE = 128
RMS_EPS = 1e-6
LAMBDA_INIT = 0.8 - 0.6 * math.exp(-0.3 * 0)

LOG2E = 1.4426950408889634
Q_SCALE = HEAD_DIM ** -0.5 * LOG2E
MASK_VALUE = -1e30
EXP2_UNDERFLOW = 160.0
SOFTMAX_MIN_DENOMINATOR = 2.0 ** -60

LANES = 128
PROJ_ROWS = 512
DIFF_BLOCK = 512
SB_Q_BLOCK = 1024
SB_K_BLOCK = 256
OUT_ROWS = 512
FFN_ROWS = 512
FFN_CHUNK = D_FF // 2
HALO_ROWS = 8
VMEM_LIMIT = 48 * 1024 * 1024


def _bits(x):
    return lax.bitcast_convert_type(x, jnp.uint32)


def _from_bits(u):
    return lax.bitcast_convert_type(u, F32)


def _tile_lanes(x, k):
    return jnp.concatenate([x] * k, axis=1) if k > 1 else x


def _rms(x, g):
    return x * lax.rsqrt(jnp.mean(x * x, axis=-1, keepdims=True) + RMS_EPS) * g


def _proj_kernel(x_ref, g_ref, w_ref, wkt_ref, qd_ref, vd_ref, qs_ref, vs_ref, kdt_ref, kst_ref):
    hb = _rms(x_ref[...], g_ref[...]).astype(BF16)
    o = jnp.dot(hb, w_ref[...], preferred_element_type=F32)
    qd_ref[...] = (o[:, 0:512] * Q_SCALE).astype(BF16)
    vd_ref[...] = o[:, 512:1024].astype(BF16)
    qs_ref[...] = (o[:, 1024:1536] * Q_SCALE).astype(BF16)
    vs_ref[...] = o[:, 1536:2048].astype(BF16)
    kt = lax.dot_general(wkt_ref[...], hb, (((1,), (1,)), ((), ())), preferred_element_type=F32)
    kdt_ref[...] = kt[0:512].astype(BF16)
    kst_ref[...] = kt[512:1024].astype(BF16)


def _proj(x2, g, w_cat, wkt_cat):
    s = x2.shape[0]
    rows = PROJ_ROWS
    row_spec = lambda w: pl.BlockSpec((rows, w), lambda i: (i, 0))
    full = lambda a: pl.BlockSpec(a.shape, lambda i: (0, 0))
    out_rows = jax.ShapeDtypeStruct((s, 512), BF16)
    out_t = jax.ShapeDtypeStruct((512, s), BF16)
    return pl.pallas_call(
        _proj_kernel,
        grid=(s // rows,),
        in_specs=[row_spec(D_MODEL), full(g), full(w_cat), full(wkt_cat)],
        out_specs=[row_spec(512)] * 4 + [pl.BlockSpec((512, rows), lambda i: (0, i))] * 2,
        out_shape=[out_rows] * 4 + [out_t] * 2,
        compiler_params=pltpu.CompilerParams(
            dimension_semantics=("arbitrary",), vmem_limit_bytes=VMEM_LIMIT),
        name="proj",
    )(x2, g, w_cat, wkt_cat)


def _bias_kernel(tab_ref, o_ref):
    h = pl.program_id(0)
    w = pl.program_id(1)
    t = DIFF_BLOCK
    rel = (lax.broadcasted_iota(jnp.int32, (t, t), 0) - lax.broadcasted_iota(jnp.int32, (t, t), 1)
           + w * t)
    n = jnp.maximum(rel, 0)
    max_exact = N_BUCKETS // 2
    nf = jnp.maximum(n, 1).astype(F32)
    large = max_exact + (jnp.log(nf / max_exact) / math.log(MAX_DISTANCE / max_exact)
                         * (N_BUCKETS - max_exact)).astype(jnp.int32)
    large = jnp.minimum(large, N_BUCKETS - 1)
    bucket = jnp.where(n < max_exact, n, large)
    b = jnp.zeros((t, t), F32)
    for k in range(N_BUCKETS):
        b = jnp.where(bucket == k, tab_ref[k, h], b)
    b = (b - tab_ref[N_BUCKETS - 1, h]) * LOG2E
    o_ref[0, 0] = jnp.where(rel >= 0, b, MASK_VALUE)


def _bias_tiles(rel_bias):
    t = DIFF_BLOCK
    return pl.pallas_call(
        _bias_kernel,
        grid=(N_DIFF_HEADS, 2),
        in_specs=[pl.BlockSpec(memory_space=pltpu.SMEM)],
        out_specs=pl.BlockSpec((1, 1, t, t), lambda h, w: (h, w, 0, 0)),
        out_shape=jax.ShapeDtypeStruct((N_DIFF_HEADS, 2, t, t), F32),
        compiler_params=pltpu.CompilerParams(
            dimension_semantics=("arbitrary", "arbitrary"), vmem_limit_bytes=VMEM_LIMIT),
        name="bias",
    )(rel_bias)


def _diff_kernel(lam_ref, q_ref, kt_ref, v_ref, bias_ref, g_ref, o_ref,
                 acc_sc, knorm_sc, m_sc, l_sc, slow_acc_sc):
    i = pl.program_id(1)
    t = DIFF_BLOCK
    lane = lax.broadcasted_iota(jnp.int32, (t, LANES), 1)
    q = q_ref[...]
    n_far = jnp.maximum(i - 1, 0)

    @pl.when(i == 0)
    def _():
        def norm_body(c, ms):
            k = kt_ref[:, pl.ds(pl.multiple_of(c * t, t), t)].astype(F32)
            sq = k * k
            return (jnp.maximum(ms[0], jnp.sum(sq[:HEAD_DIM], axis=0, keepdims=True)),
                    jnp.maximum(ms[1], jnp.sum(sq[HEAD_DIM:], axis=0, keepdims=True)))

        zeros = jnp.zeros((1, t), F32)
        ms = lax.fori_loop(0, kt_ref.shape[1] // t, norm_body, (zeros, zeros))
        for mp in range(2):
            knorm_sc[mp] = jnp.broadcast_to(jnp.max(ms[mp], axis=1, keepdims=True), knorm_sc.shape[1:])

    def write_out(o0, o1):
        lp = lam_ref[...]
        lam = (jnp.exp(jnp.sum(lp[0:1] * lp[1:2], axis=1, keepdims=True))
               - jnp.exp(jnp.sum(lp[2:3] * lp[3:4], axis=1, keepdims=True)) + LAMBDA_INIT)
        o_ref[...] = (_rms(o0 - lam * o1, g_ref[...]) * (1.0 - LAMBDA_INIT)).astype(o_ref.dtype)

    zero_q = jnp.zeros_like(q)
    qm = (jnp.where(lane < HEAD_DIM, q, zero_q), jnp.where(lane >= HEAD_DIM, q, zero_q))
    qsq = q.astype(F32) * q.astype(F32)
    bias_max = jnp.max(jnp.max(bias_ref[0, 0], axis=1, keepdims=True), axis=0, keepdims=True)
    bound = []
    for mp in range(2):
        own = (lane < HEAD_DIM) if mp == 0 else (lane >= HEAD_DIM)
        qn = jnp.sum(jnp.where(own, qsq, 0.0), axis=1, keepdims=True)
        b = jnp.sqrt(qn * knorm_sc[mp][0:1, 0:1]) * (1.0 + 2.0 ** -10) + bias_max
        bound.append(jnp.broadcast_to(b, (t, LANES)))

    acc_sc[...] = jnp.zeros(acc_sc.shape, F32)

    def fast_step(start, width, bias):
        kt = kt_ref[:, pl.ds(start, width)]
        v_aug = jnp.concatenate([v_ref[pl.ds(start, width), :], jnp.ones((width, LANES), BF16)], axis=1)
        for mp in range(2):
            s = jnp.dot(qm[mp], kt, preferred_element_type=F32)
            s = s - _tile_lanes(bound[mp], width // LANES)
            if bias is not None:
                s = s + bias
            acc_sc[mp] += jnp.dot(jnp.exp2(s).astype(BF16), v_aug, preferred_element_type=F32)

    def far_body(r, carry):
        fast_step(pl.multiple_of(r * 2 * t, 2 * t), 2 * t, None)
        return carry

    lax.fori_loop(0, n_far // 2, far_body, 0)

    @pl.when(n_far % 2 == 1)
    def _():
        fast_step(pl.multiple_of((n_far - 1) * t, t), t, None)

    @pl.when(i >= 1)
    def _():
        fast_step(pl.multiple_of((i - 1) * t, t), t, bias_ref[0, 1])

    fast_step(pl.multiple_of(i * t, t), t, bias_ref[0, 0])

    l0, l1 = acc_sc[0][:, LANES:], acc_sc[1][:, LANES:]
    write_out(acc_sc[0][:, :LANES] / l0, acc_sc[1][:, :LANES] / l1)

    @pl.when(jnp.logical_not(jnp.min(jnp.minimum(l0, l1)) >= SOFTMAX_MIN_DENOMINATOR))
    def _():
        m_sc[...] = jnp.full(m_sc.shape, MASK_VALUE, F32)
        l_sc[...] = jnp.zeros(l_sc.shape, F32)
        slow_acc_sc[...] = jnp.zeros(slow_acc_sc.shape, F32)

        def step(j, bias):
            start = pl.multiple_of(j * t, t)
            kt = kt_ref[:, pl.ds(start, t)]
            v = v_ref[pl.ds(start, t), :]
            for mp in range(2):
                s = jnp.dot(qm[mp], kt, preferred_element_type=F32)
                if bias is not None:
                    s = s + bias
                m_prev = m_sc[mp]
                m_new = jnp.maximum(m_prev, jnp.max(s, axis=1, keepdims=True))
                alpha = jnp.exp2(m_prev - m_new)
                p = jnp.exp2(s - _tile_lanes(m_new, t // LANES))
                l_sc[mp] = alpha * l_sc[mp] + jnp.sum(p, axis=1, keepdims=True)
                slow_acc_sc[mp] = (alpha * slow_acc_sc[mp]
                                   + jnp.dot(p.astype(BF16), v, preferred_element_type=F32))
                m_sc[mp] = m_new

        def slow_body(j, carry):
            step(j, None)
            return carry

        lax.fori_loop(0, n_far, slow_body, 0)

        @pl.when(i >= 1)
        def _():
            step(i - 1, bias_ref[0, 1])

        step(i, bias_ref[0, 0])
        write_out(slow_acc_sc[0] / l_sc[0], slow_acc_sc[1] / l_sc[1])


def _diff_attention(lam_params, qd, kdt, vd, bias, g):
    s = qd.shape[0]
    t = DIFF_BLOCK
    return pl.pallas_call(
        _diff_kernel,
        grid=(N_DIFF_HEADS, s // t),
        in_specs=[
            pl.BlockSpec(lam_params.shape, lambda h, i: (0, 0)),
            pl.BlockSpec((t, LANES), lambda h, i: (i, h)),
            pl.BlockSpec((LANES, s), lambda h, i: (h, 0)),
            pl.BlockSpec((s, LANES), lambda h, i: (0, h)),
            pl.BlockSpec((1, 2, t, t), lambda h, i: (h, 0, 0, 0)),
            pl.BlockSpec(g.shape, lambda h, i: (0, 0)),
        ],
        out_specs=pl.BlockSpec((t, LANES), lambda h, i: (i, h)),
        out_shape=jax.ShapeDtypeStruct((s, DIFF_WIDTH), BF16),
        scratch_shapes=[pltpu.VMEM((2, t, 2 * LANES), F32), pltpu.VMEM((2, 8, LANES), F32)]
        + [pltpu.VMEM((2, t, LANES), F32)] * 3,
        compiler_params=pltpu.CompilerParams(
            dimension_semantics=("arbitrary", "arbitrary"), vmem_limit_bytes=VMEM_LIMIT),
        name="diff_attn",
    )(lam_params, qd, kdt, vd, bias, g)


def _sb_kernel(q_ref, kt_ref, v_ref, g_ref, o_ref, carry_sc, acc_sc, knorm_sc, thresh_sm, done_sm):
    i = pl.program_id(1)
    t = SB_K_BLOCK
    n_sub = SB_Q_BLOCK // t
    lane = lax.broadcasted_iota(jnp.int32, (t, LANES), 1)
    head0 = lane < HEAD_DIM
    lower = (lax.broadcasted_iota(jnp.int32, (t, t), 0)
             >= lax.broadcasted_iota(jnp.int32, (t, t), 1)).astype(BF16)
    strict = lax.broadcasted_iota(jnp.int32, (t, t), 0) > lax.broadcasted_iota(jnp.int32, (t, t), 1)

    @pl.when(i == 0)
    def _():
        def norm_body(c, m):
            k = kt_ref[:, pl.ds(pl.multiple_of(c * t, t), t)].astype(F32)
            sq = k * k
            return jnp.maximum(m, jnp.maximum(jnp.sum(sq[:HEAD_DIM], axis=0, keepdims=True),
                                              jnp.sum(sq[HEAD_DIM:], axis=0, keepdims=True)))

        m = lax.fori_loop(0, kt_ref.shape[1] // t, norm_body, jnp.zeros((1, t), F32))
        knorm_sc[...] = jnp.broadcast_to(jnp.max(m, axis=1, keepdims=True), knorm_sc.shape)

    def load_q(u):
        q = q_ref[pl.ds(pl.multiple_of(u * t, t), t), :]
        zero_q = jnp.zeros_like(q)
        return q, (jnp.where(head0, q, zero_q), jnp.where(head0, zero_q, q))

    def step(u, qm, jb, masked):
        start = pl.multiple_of(jb * t, t)
        kt = kt_ref[:, pl.ds(start, t)]
        v = v_ref[pl.ds(start, t), :]
        zero_v = jnp.zeros_like(v)
        v2 = jnp.concatenate([jnp.where(head0, v, zero_v), jnp.where(head0, zero_v, v)], axis=0)
        probs = []
        for hh in range(2):
            z = jnp.dot(qm[hh], kt, preferred_element_type=F32)
            neg_abs = _from_bits(_bits(z) | jnp.uint32(0x80000000))
            nl = jnp.maximum(z, 0.0) + jnp.log(1.0 + jnp.exp2(neg_abs)) * LOG2E
            if masked:
                nl = jnp.where(strict, nl, 0.0)
            csum = jnp.dot(nl.astype(BF16), lower, preferred_element_type=F32)
            carry = carry_sc[u, hh]
            a = jnp.exp2((z - _tile_lanes(carry, t // LANES)) - csum)
            if masked:
                a = jnp.where(strict, a, 0.0)
            probs.append(a.astype(BF16))
            carry_sc[u, hh] = carry + csum[:, 0:1]
        acc_sc[u] += jnp.dot(jnp.concatenate(probs, axis=1), v2, preferred_element_type=F32)

    carry_sc[...] = jnp.zeros(carry_sc.shape, F32)
    acc_sc[...] = jnp.zeros(acc_sc.shape, F32)

    qs = [load_q(u) for u in range(n_sub)]
    for u in range(n_sub):
        step(u, qs[u][1], i * n_sub + u, True)
    for u in range(1, n_sub):
        step(u, qs[u][1], i * n_sub + u - 1, False)

    def carry_min(u):
        return jnp.min(jnp.minimum(carry_sc[u, 0], carry_sc[u, 1]))

    for u in range(n_sub):
        qsq = qs[u][0].astype(F32) * qs[u][0].astype(F32)
        qn0 = jnp.sum(jnp.where(head0, qsq, 0.0), axis=1, keepdims=True)
        qn1 = jnp.sum(jnp.where(head0, 0.0, qsq), axis=1, keepdims=True)
        qnorm = jnp.max(jnp.maximum(qn0, qn1), axis=0, keepdims=True)
        thresh = jnp.sqrt(qnorm * knorm_sc[0:1, 0:1]) * (1.0 + 2.0 ** -10) + EXP2_UNDERFLOW
        thresh_sm[u] = jnp.max(thresh)
        done_sm[u] = (carry_min(u) >= thresh_sm[u]).astype(jnp.int32)

    def remaining_blocks(u, c):
        _, qm = load_q(u)

        def far_cond(state):
            jb, all_zero = state
            return jnp.logical_and(jb >= 0, all_zero == 0)

        def far_body(state):
            jb, _ = state
            step(u, qm, jb, False)
            return jb - 1, (carry_min(u) >= thresh_sm[u]).astype(jnp.int32)

        first = i * n_sub + u - 1 - jnp.minimum(u, 1)
        lax.while_loop(far_cond, far_body, (first, done_sm[u]))
        return c

    lax.fori_loop(0, n_sub, remaining_blocks, 0)

    for u in range(n_sub):
        o = acc_sc[u]
        sq = o * o
        ss0 = jnp.sum(jnp.where(head0, sq, 0.0), axis=1, keepdims=True)
        ss1 = jnp.sum(jnp.where(head0, 0.0, sq), axis=1, keepdims=True)
        ms = jnp.where(head0, ss0, ss1) * (1.0 / HEAD_DIM)
        o_ref[u * t:(u + 1) * t, :] = (o * lax.rsqrt(ms + RMS_EPS) * g_ref[...]).astype(o_ref.dtype)


def _sb_attention(qs, kst, vs, g2):
    s = qs.shape[0]
    tq, t = SB_Q_BLOCK, SB_K_BLOCK
    return pl.pallas_call(
        _sb_kernel,
        grid=(N_SB_HEADS // 2, s // tq),
        in_specs=[
            pl.BlockSpec((tq, LANES), lambda h, i: (i, h)),
            pl.BlockSpec((LANES, s), lambda h, i: (h, 0)),
            pl.BlockSpec((s, LANES), lambda h, i: (0, h)),
            pl.BlockSpec(g2.shape, lambda h, i: (0, 0)),
        ],
        out_specs=pl.BlockSpec((tq, LANES), lambda h, i: (i, h)),
        out_shape=jax.ShapeDtypeStruct((s, SB_WIDTH), BF16),
        scratch_shapes=[pltpu.VMEM((tq // t, 2, t, LANES), F32), pltpu.VMEM((tq // t, t, LANES), F32),
                        pltpu.VMEM((8, LANES), F32),
                        pltpu.SMEM((tq // t,), F32), pltpu.SMEM((tq // t,), jnp.int32)],
        compiler_params=pltpu.CompilerParams(
            dimension_semantics=("arbitrary", "arbitrary"), vmem_limit_bytes=VMEM_LIMIT),
        name="sb_attn",
    )(qs, kst, vs, g2)


def _out_kernel(d_ref, s_ref, wd_ref, ws_ref, x_ref, gpost_ref, gpre_ref, x1_ref, h2_ref):
    mix = (jnp.dot(d_ref[...], wd_ref[...], preferred_element_type=F32)
           + jnp.dot(s_ref[...], ws_ref[...], preferred_element_type=F32))
    x1 = x_ref[...] + _rms(mix, gpost_ref[...])
    x1_ref[...] = x1
    h2_ref[...] = _rms(x1, gpre_ref[...]).astype(h2_ref.dtype)


def _out_proj(diff_o, sb_o, w_od, w_os, x2, g_post, g_pre):
    s = x2.shape[0]
    rows = OUT_ROWS
    row_spec = lambda w: pl.BlockSpec((rows, w), lambda i: (i, 0))
    full = lambda a: pl.BlockSpec(a.shape, lambda i: (0, 0))
    return pl.pallas_call(
        _out_kernel,
        grid=(s // rows,),
        in_specs=[row_spec(512), row_spec(512), full(w_od), full(w_os), row_spec(D_MODEL),
                  full(g_post), full(g_pre)],
        out_specs=[row_spec(D_MODEL), row_spec(D_MODEL)],
        out_shape=[jax.ShapeDtypeStruct((s, D_MODEL), F32), jax.ShapeDtypeStruct((s, D_MODEL), BF16)],
        compiler_params=pltpu.CompilerParams(
            dimension_semantics=("arbitrary",), vmem_limit_bytes=VMEM_LIMIT),
        name="out_proj",
    )(diff_o, sb_o, w_od, w_os, x2, g_post, g_pre)


def _gelu_tanh(x):
    return 0.5 * x * (1.0 + jnp.tanh(math.sqrt(2.0 / math.pi) * (x + 0.044715 * (x * x * x))))


def _ffn_kernel(h_ref, halo_ref, wg_ref, wv_ref, cwg_ref, cwv_ref, cbg_ref, cbv_ref, wd_ref,
                x1_ref, g_ref, o_ref, ug_sc, uv_sc, acc_sc):
    i = pl.program_id(0)
    f = pl.program_id(1)
    rows = FFN_ROWS
    h = h_ref[...]
    halo = jnp.where(i > 0, halo_ref[...], jnp.zeros_like(halo_ref[...]))

    def conv(w_ref, cw_ref, cb_ref, u_sc):
        u_sc[0:HALO_ROWS, :] = jnp.dot(halo, w_ref[...], preferred_element_type=F32)
        u_sc[HALO_ROWS:, :] = jnp.dot(h, w_ref[...], preferred_element_type=F32)
        cw = cw_ref[...]
        y = cb_ref[...]
        for k in range(CONV_WIDTH):
            off = HALO_ROWS - (CONV_WIDTH - 1) + k
            y = y + u_sc[off:off + rows, :] * cw[k:k + 1]
        return y

    gate = conv(wg_ref, cwg_ref, cbg_ref, ug_sc)
    val = conv(wv_ref, cwv_ref, cbv_ref, uv_sc)
    act = (_gelu_tanh(gate) * val).astype(BF16)
    part = jnp.dot(act, wd_ref[...], preferred_element_type=F32)

    @pl.when(f == 0)
    def _():
        acc_sc[...] = part

    @pl.when(f > 0)
    def _():
        acc_sc[...] += part

    @pl.when(f == pl.num_programs(1) - 1)
    def _():
        o_ref[...] = x1_ref[...] + _rms(acc_sc[...], g_ref[...])


def _ffn(h2, w_up, conv_w, conv_b, w_down, x1, g_post):
    s = h2.shape[0]
    rows, chunk = FFN_ROWS, FFN_CHUNK
    n_chunk = D_FF // chunk
    halo_blocks = rows // HALO_ROWS
    row_spec = pl.BlockSpec((rows, D_MODEL), lambda i, f: (i, 0))
    halo_spec = pl.BlockSpec((HALO_ROWS, D_MODEL),
                             lambda i, f: (jnp.maximum(i * halo_blocks - 1, 0), 0))
    gate_cols = lambda r: pl.BlockSpec((r, chunk), lambda i, f: (0, f))
    val_cols = lambda r: pl.BlockSpec((r, chunk), lambda i, f: (0, f + n_chunk))
    return pl.pallas_call(
        _ffn_kernel,
        grid=(s // rows, n_chunk),
        in_specs=[row_spec, halo_spec, gate_cols(D_MODEL), val_cols(D_MODEL),
                  gate_cols(CONV_WIDTH), val_cols(CONV_WIDTH), gate_cols(1), val_cols(1),
                  pl.BlockSpec((chunk, D_MODEL), lambda i, f: (f, 0)),
                  row_spec, pl.BlockSpec(g_post.shape, lambda i, f: (0, 0))],
        out_specs=row_spec,
        out_shape=jax.ShapeDtypeStruct((s, D_MODEL), F32),
        scratch_shapes=[pltpu.VMEM((HALO_ROWS + rows, chunk), F32)] * 2
        + [pltpu.VMEM((rows, D_MODEL), F32)],
        compiler_params=pltpu.CompilerParams(
            dimension_semantics=("arbitrary", "arbitrary"), vmem_limit_bytes=VMEM_LIMIT),
        name="ffn",
    )(h2, h2, w_up, w_up, conv_w, conv_w, conv_b, conv_b, w_down, x1, g_post)


def kernel(x, attn_pre_norm, w_qkv, lambda_q1, lambda_k1, lambda_q2, lambda_k2, diff_subln, sb_norm,
           rel_bias, w_o, attn_post_norm, ffn_pre_norm, w_up, conv_w, conv_b, w_down, ffn_post_norm):
    b, s, _ = x.shape
    assert b == 1 and attn_pre_norm.shape[0] == 1, "single sequence, single layer"
    assert s % max(PROJ_ROWS, DIFF_BLOCK, SB_Q_BLOCK, OUT_ROWS, FFN_ROWS) == 0
    x2 = x[0]
    row = lambda a: a.reshape(1, -1).astype(F32)

    wq = w_qkv[0].astype(BF16)
    w_cat = jnp.concatenate([wq[:, 0:512], wq[:, 1024:1536], wq[:, 1536:2048], wq[:, 2560:3072]], axis=1)
    wkt_cat = jnp.concatenate([wq[:, 512:1024], wq[:, 2048:2560]], axis=1).T
    qd, vd, qs, vs, kdt, kst = _proj(x2, row(attn_pre_norm), w_cat, wkt_cat)

    bias = _bias_tiles(rel_bias.astype(F32))
    lam_params = jnp.concatenate([lambda_q1, lambda_k1, lambda_q2, lambda_k2], axis=0).astype(F32)
    diff_o = _diff_attention(lam_params, qd, kdt, vd, bias, row(diff_subln))
    sb_o = _sb_attention(qs, kst, vs, row(jnp.concatenate([sb_norm[0], sb_norm[0]])))

    wo = w_o[0].astype(BF16)
    x1, h2 = _out_proj(diff_o, sb_o, wo[:DIFF_WIDTH], wo[DIFF_WIDTH:], x2,
                       row(attn_post_norm), row(ffn_pre_norm))
    out = _ffn(h2, w_up[0].astype(BF16), conv_w[0].astype(F32), conv_b.astype(F32),
               w_down[0].astype(BF16), x1, row(ffn_post_norm))
    return out[None]
```

```python
import functools
import math

import jax
import jax.numpy as jnp
from jax import lax
from jax.experimental import pallas as pl
from jax.experimental.pallas import tpu as pltpu

F32 = jnp.float32
BF16 = jnp.bfloat16

D_MODEL = 1024
HEAD_DIM = 64
N_DIFF_HEADS = 4
N_SB_HEADS = 8
DIFF_WIDTH = N_DIFF_HEADS * 2 * HEAD_DIM
SB_WIDTH = N_SB_HEADS * HEAD_DIM
D_FF = 2816
CONV_WIDTH = 3
N_BUCKETS = 32
MAX_DISTANCE = 128
RMS_EPS = 1e-6
LAMBDA_INIT = 0.8 - 0.6 * math.exp(-0.3 * 0)

LOG2E = 1.4426950408889634
Q_SCALE = HEAD_DIM ** -0.5 * LOG2E
MASK_VALUE = -1e30
EXP2_UNDERFLOW = 160.0
SOFTMAX_MIN_DENOMINATOR = 2.0 ** -60

LANES = 128
PROJ_ROWS = 512
DIFF_BLOCK = 512
SB_Q_BLOCK = 1024
SB_K_BLOCK = 256
OUT_ROWS = 512
FFN_ROWS = 512
FFN_CHUNK = D_FF // 2
HALO_ROWS = 8
VMEM_LIMIT = 48 * 1024 * 1024


def _bits(x):
    return lax.bitcast_convert_type(x, jnp.uint32)


def _from_bits(u):
    return lax.bitcast_convert_type(u, F32)


def _tile_lanes(x, k):
    return jnp.concatenate([x] * k, axis=1) if k > 1 else x


def _rms(x, g):
    return x * lax.rsqrt(jnp.mean(x * x, axis=-1, keepdims=True) + RMS_EPS) * g


def _proj_kernel(x_ref, g_ref, w_ref, wkt_ref, qd_ref, vd_ref, qs_ref, vs_ref, kdt_ref, kst_ref):
    hb = _rms(x_ref[...], g_ref[...]).astype(BF16)
    o = jnp.dot(hb, w_ref[...], preferred_element_type=F32)
    qd_ref[...] = (o[:, 0:512] * Q_SCALE).astype(BF16)
    vd_ref[...] = o[:, 512:1024].astype(BF16)
    qs_ref[...] = (o[:, 1024:1536] * Q_SCALE).astype(BF16)
    vs_ref[...] = o[:, 1536:2048].astype(BF16)
    kt = lax.dot_general(wkt_ref[...], hb, (((1,), (1,)), ((), ())), preferred_element_type=F32)
    kdt_ref[...] = kt[0:512].astype(BF16)
    kst_ref[...] = kt[512:1024].astype(BF16)


def _proj(x2, g, w_cat, wkt_cat):
    s = x2.shape[0]
    rows = PROJ_ROWS
    row_spec = lambda w: pl.BlockSpec((rows, w), lambda i: (i, 0))
    full = lambda a: pl.BlockSpec(a.shape, lambda i: (0, 0))
    out_rows = jax.ShapeDtypeStruct((s, 512), BF16)
    out_t = jax.ShapeDtypeStruct((512, s), BF16)
    return pl.pallas_call(
        _proj_kernel,
        grid=(s // rows,),
        in_specs=[row_spec(D_MODEL), full(g), full(w_cat), full(wkt_cat)],
        out_specs=[row_spec(512)] * 4 + [pl.BlockSpec((512, rows), lambda i: (0, i))] * 2,
        out_shape=[out_rows] * 4 + [out_t] * 2,
        compiler_params=pltpu.CompilerParams(
            dimension_semantics=("arbitrary",), vmem_limit_bytes=VMEM_LIMIT),
        name="proj",
    )(x2, g, w_cat, wkt_cat)


def _bias_kernel(tab_ref, o_ref):
    h = pl.program_id(0)
    w = pl.program_id(1)
    t = DIFF_BLOCK
    rel = (lax.broadcasted_iota(jnp.int32, (t, t), 0) - lax.broadcasted_iota(jnp.int32, (t, t), 1)
           + w * t)
    n = jnp.maximum(rel, 0)
    max_exact = N_BUCKETS // 2
    nf = jnp.maximum(n, 1).astype(F32)
    large = max_exact + (jnp.log(nf / max_exact) / math.log(MAX_DISTANCE / max_exact)
                         * (N_BUCKETS - max_exact)).astype(jnp.int32)
    large = jnp.minimum(large, N_BUCKETS - 1)
    bucket = jnp.where(n < max_exact, n, large)
    b = jnp.zeros((t, t), F32)
    for k in range(N_BUCKETS):
        b = jnp.where(bucket == k, tab_ref[k, h], b)
    b = (b - tab_ref[N_BUCKETS - 1, h]) * LOG2E
    o_ref[0, 0] = jnp.where(rel >= 0, b, MASK_VALUE)


def _bias_tiles(rel_bias):
    t = DIFF_BLOCK
    return pl.pallas_call(
        _bias_kernel,
        grid=(N_DIFF_HEADS, 2),
        in_specs=[pl.BlockSpec(memory_space=pltpu.SMEM)],
        out_specs=pl.BlockSpec((1, 1, t, t), lambda h, w: (h, w, 0, 0)),
        out_shape=jax.ShapeDtypeStruct((N_DIFF_HEADS, 2, t, t), F32),
        compiler_params=pltpu.CompilerParams(
            dimension_semantics=("arbitrary", "arbitrary"), vmem_limit_bytes=VMEM_LIMIT),
        name="bias",
    )(rel_bias)


def _diff_kernel(lam_ref, q_ref, kt_ref, v_ref, bias_ref, g_ref, o_ref,
                 acc_sc, stat_sc, s_sc, m_sc, l_sc, slow_acc_sc):
    i = pl.program_id(1)
    t = DIFF_BLOCK
    lane = lax.broadcasted_iota(jnp.int32, (t, LANES), 1)
    q = q_ref[...]
    n_far = jnp.maximum(i - 1, 0)
    n_pairs = n_far // 2

    @pl.when(i == 0)
    def _():
        def norm_body(c, ms):
            k = kt_ref[:, pl.ds(pl.multiple_of(c * t, t), t)].astype(F32)
            sq = k * k
            return (jnp.maximum(ms[0], jnp.sum(sq[:HEAD_DIM], axis=0, keepdims=True)),
                    jnp.maximum(ms[1], jnp.sum(sq[HEAD_DIM:], axis=0, keepdims=True)))

        zeros = jnp.zeros((1, t), F32)
        ms = lax.fori_loop(0, kt_ref.shape[1] // t, norm_body, (zeros, zeros))
        for mp in range(2):
            stat_sc[mp] = jnp.broadcast_to(jnp.max(ms[mp], axis=1, keepdims=True), stat_sc.shape[1:])
        bias_max = jnp.max(jnp.max(bias_ref[0, 0], axis=1, keepdims=True), axis=0, keepdims=True)
        stat_sc[2] = jnp.broadcast_to(bias_max, stat_sc.shape[1:])

    def write_out(o0, o1):
        lp = lam_ref[...]
        lam = (jnp.exp(jnp.sum(lp[0:1] * lp[1:2], axis=1, keepdims=True))
               - jnp.exp(jnp.sum(lp[2:3] * lp[3:4], axis=1, keepdims=True)) + LAMBDA_INIT)
        o_ref[...] = (_rms(o0 - lam * o1, g_ref[...]) * (1.0 - LAMBDA_INIT)).astype(o_ref.dtype)

    zero_q = jnp.zeros_like(q)
    qm = (jnp.where(lane < HEAD_DIM, q, zero_q), jnp.where(lane >= HEAD_DIM, q, zero_q))
    qsq = q.astype(F32) * q.astype(F32)
    bound = []
    for mp in range(2):
        own = (lane < HEAD_DIM) if mp == 0 else (lane >= HEAD_DIM)
        qn = jnp.sum(jnp.where(own, qsq, 0.0), axis=1, keepdims=True)
        b = jnp.sqrt(qn * stat_sc[mp][0:1, 0:1]) * (1.0 + 2.0 ** -10) + stat_sc[2][0:1, 0:1]
        bound.append(jnp.broadcast_to(b, (t, LANES)))

    acc_sc[...] = jnp.zeros(acc_sc.shape, F32)

    def scores(start, width):
        kt = kt_ref[:, pl.ds(start, width)]
        return [jnp.dot(qm[mp], kt, preferred_element_type=F32) - _tile_lanes(bound[mp], width // LANES)
                for mp in range(2)]

    def accumulate(s, start, width):
        v_aug = jnp.concatenate([v_ref[pl.ds(start, width), :], jnp.ones((width, LANES), BF16)], axis=1)
        for mp in range(2):
            acc_sc[mp] += jnp.dot(jnp.exp2(s[mp]).astype(BF16), v_aug, preferred_element_type=F32)

    pair = 2 * t

    @pl.when(n_pairs > 0)
    def _():
        first = scores(0, pair)
        for mp in range(2):
            s_sc[mp] = first[mp]

        def far_body(r, carry):
            cur = [s_sc[mp] for mp in range(2)]
            nxt = scores(pl.multiple_of((r + 1) * pair, pair), pair)
            accumulate(cur, pl.multiple_of(r * pair, pair), pair)
            for mp in range(2):
                s_sc[mp] = nxt[mp]
            return carry

        lax.fori_loop(0, n_pairs - 1, far_body, 0)
        accumulate([s_sc[mp] for mp in range(2)], pl.multiple_of((n_pairs - 1) * pair, pair), pair)

    def tail(first_block, biases):
        width = len(biases) * t
        start = pl.multiple_of(first_block * t, t)
        s = scores(start, width)
        for mp in range(2):
            cols = [s[mp][:, c * t:(c + 1) * t] for c in range(len(biases))]
            s[mp] = jnp.concatenate([x if b is None else x + b for x, b in zip(cols, biases)], axis=1)
        accumulate(s, start, width)

    @pl.when(i == 0)
    def _():
        tail(0, [bias_ref[0, 0]])

    @pl.when(jnp.logical_and(i >= 1, n_far % 2 == 0))
    def _():
        tail(i - 1, [bias_ref[0, 1], bias_ref[0, 0]])

    @pl.when(n_far % 2 == 1)
    def _():
        tail(i - 2, [None, bias_ref[0, 1], bias_ref[0, 0]])

    l0, l1 = acc_sc[0][:, LANES:], acc_sc[1][:, LANES:]
    write_out(acc_sc[0][:, :LANES] / l0, acc_sc[1][:, :LANES] / l1)

    @pl.when(jnp.logical_not(jnp.min(jnp.minimum(l0, l1)) >= SOFTMAX_MIN_DENOMINATOR))
    def _():
        m_sc[...] = jnp.full(m_sc.shape, MASK_VALUE, F32)
        l_sc[...] = jnp.zeros(l_sc.shape, F32)
        slow_acc_sc[...] = jnp.zeros(slow_acc_sc.shape, F32)

        def step(j, bias):
            start = pl.multiple_of(j * t, t)
            kt = kt_ref[:, pl.ds(start, t)]
            v = v_ref[pl.ds(start, t), :]
            for mp in range(2):
                s = jnp.dot(qm[mp], kt, preferred_element_type=F32)
                if bias is not None:
                    s = s + bias
                m_prev = m_sc[mp]
                m_new = jnp.maximum(m_prev, jnp.max(s, axis=1, keepdims=True))
                alpha = jnp.exp2(m_prev - m_new)
                p = jnp.exp2(s - _tile_lanes(m_new, t // LANES))
                l_sc[mp] = alpha * l_sc[mp] + jnp.sum(p, axis=1, keepdims=True)
                slow_acc_sc[mp] = (alpha * slow_acc_sc[mp]
                                   + jnp.dot(p.astype(BF16), v, preferred_element_type=F32))
                m_sc[mp] = m_new

        def slow_body(j, carry):
            step(j, None)
            return carry

        lax.fori_loop(0, n_far, slow_body, 0)

        @pl.when(i >= 1)
        def _():
            step(i - 1, bias_ref[0, 1])

        step(i, bias_ref[0, 0])
        write_out(slow_acc_sc[0] / l_sc[0], slow_acc_sc[1] / l_sc[1])


def _diff_attention(lam_params, qd, kdt, vd, bias, g):
    s = qd.shape[0]
    t = DIFF_BLOCK
    return pl.pallas_call(
        _diff_kernel,
        grid=(N_DIFF_HEADS, s // t),
        in_specs=[
            pl.BlockSpec(lam_params.shape, lambda h, i: (0, 0)),
            pl.BlockSpec((t, LANES), lambda h, i: (i, h)),
            pl.BlockSpec((LANES, s), lambda h, i: (h, 0)),
            pl.BlockSpec((s, LANES), lambda h, i: (0, h)),
            pl.BlockSpec((1, 2, t, t), lambda h, i: (h, 0, 0, 0)),
            pl.BlockSpec(g.shape, lambda h, i: (0, 0)),
        ],
        out_specs=pl.BlockSpec((t, LANES), lambda h, i: (i, h)),
        out_shape=jax.ShapeDtypeStruct((s, DIFF_WIDTH), BF16),
        scratch_shapes=[pltpu.VMEM((2, t, 2 * LANES), F32), pltpu.VMEM((3, 8, LANES), F32),
                        pltpu.VMEM((2, t, 2 * t), F32)]
        + [pltpu.VMEM((2, t, LANES), F32)] * 3,
        compiler_params=pltpu.CompilerParams(
            dimension_semantics=("arbitrary", "arbitrary"), vmem_limit_bytes=VMEM_LIMIT),
        name="diff_attn",
    )(lam_params, qd, kdt, vd, bias, g)


def _sb_kernel(q_ref, kt_ref, v_ref, g_ref, o_ref, carry_sc, acc_sc, knorm_sc, thresh_sm, done_sm):
    i = pl.program_id(1)
    t = SB_K_BLOCK
    n_sub = SB_Q_BLOCK // t
    lane = lax.broadcasted_iota(jnp.int32, (t, LANES), 1)
    head0 = lane < HEAD_DIM
    lower = (lax.broadcasted_iota(jnp.int32, (t, t), 0)
             >= lax.broadcasted_iota(jnp.int32, (t, t), 1)).astype(BF16)
    strict = lax.broadcasted_iota(jnp.int32, (t, t), 0) > lax.broadcasted_iota(jnp.int32, (t, t), 1)

    @pl.when(i == 0)
    def _():
        def norm_body(c, m):
            k = kt_ref[:, pl.ds(pl.multiple_of(c * t, t), t)].astype(F32)
            sq = k * k
            return jnp.maximum(m, jnp.maximum(jnp.sum(sq[:HEAD_DIM], axis=0, keepdims=True),
                                              jnp.sum(sq[HEAD_DIM:], axis=0, keepdims=True)))

        m = lax.fori_loop(0, kt_ref.shape[1] // t, norm_body, jnp.zeros((1, t), F32))
        knorm_sc[...] = jnp.broadcast_to(jnp.max(m, axis=1, keepdims=True), knorm_sc.shape)

    def load_q(u):
        q = q_ref[pl.ds(pl.multiple_of(u * t, t), t), :]
        zero_q = jnp.zeros_like(q)
        return q, (jnp.where(head0, q, zero_q), jnp.where(head0, zero_q, q))

    def step(u, qm, jb, masked):
        start = pl.multiple_of(jb * t, t)
        kt = kt_ref[:, pl.ds(start, t)]
        v = v_ref[pl.ds(start, t), :]
        zero_v = jnp.zeros_like(v)
        v2 = jnp.concatenate([jnp.where(head0, v, zero_v), jnp.where(head0, zero_v, v)], axis=0)
        probs = []
        for hh in range(2):
            z = jnp.dot(qm[hh], kt, preferred_element_type=F32)
            neg_abs = _from_bits(_bits(z) | jnp.uint32(0x80000000))
            nl = jnp.maximum(z, 0.0) + jnp.log(1.0 + jnp.exp2(neg_abs)) * LOG2E
            if masked:
                nl = jnp.where(strict, nl, 0.0)
            csum = jnp.dot(nl.astype(BF16), lower, preferred_element_type=F32)
            carry = carry_sc[u, hh]
            a = jnp.exp2((z - _tile_lanes(carry, t // LANES)) - csum)
            if masked:
                a = jnp.where(strict, a, 0.0)
            probs.append(a.astype(BF16))
            carry_sc[u, hh] = carry + csum[:, 0:1]
        acc_sc[u] += jnp.dot(jnp.concatenate(probs, axis=1), v2, preferred_element_type=F32)

    carry_sc[...] = jnp.zeros(carry_sc.shape, F32)
    acc_sc[...] = jnp.zeros(acc_sc.shape, F32)

    qs = [load_q(u) for u in range(n_sub)]
    for u in range(n_sub):
        step(u, qs[u][1], i * n_sub + u, True)
    for u in range(1, n_sub):
        step(u, qs[u][1], i * n_sub + u - 1, False)

    def carry_min(u):
        return jnp.min(jnp.minimum(carry_sc[u, 0], carry_sc[u, 1]))

    for u in range(n_sub):
        qsq = qs[u][0].astype(F32) * qs[u][0].astype(F32)
        qn0 = jnp.sum(jnp.where(head0, qsq, 0.0), axis=1, keepdims=True)
        qn1 = jnp.sum(jnp.where(head0, 0.0, qsq), axis=1, keepdims=True)
        qnorm = jnp.max(jnp.maximum(qn0, qn1), axis=0, keepdims=True)
        thresh = jnp.sqrt(qnorm * knorm_sc[0:1, 0:1]) * (1.0 + 2.0 ** -10) + EXP2_UNDERFLOW
        thresh_sm[u] = jnp.max(thresh)
        done_sm[u] = (carry_min(u) >= thresh_sm[u]).astype(jnp.int32)

    def remaining_blocks(u, c):
        _, qm = load_q(u)

        def far_cond(state):
            jb, all_zero = state
            return jnp.logical_and(jb >= 0, all_zero == 0)

        def far_body(state):
            jb, _ = state
            step(u, qm, jb, False)
            return jb - 1, (carry_min(u) >= thresh_sm[u]).astype(jnp.int32)

        first = i * n_sub + u - 1 - jnp.minimum(u, 1)
        lax.while_loop(far_cond, far_body, (first, done_sm[u]))
        return c

    lax.fori_loop(0, n_sub, remaining_blocks, 0)

    for u in range(n_sub):
        o = acc_sc[u]
        sq = o * o
        ss0 = jnp.sum(jnp.where(head0, sq, 0.0), axis=1, keepdims=True)
        ss1 = jnp.sum(jnp.where(head0, 0.0, sq), axis=1, keepdims=True)
        ms = jnp.where(head0, ss0, ss1) * (1.0 / HEAD_DIM)
        o_ref[u * t:(u + 1) * t, :] = (o * lax.rsqrt(ms + RMS_EPS) * g_ref[...]).astype(o_ref.dtype)


def _sb_attention(qs, kst, vs, g2):
    s = qs.shape[0]
    tq, t = SB_Q_BLOCK, SB_K_BLOCK
    return pl.pallas_call(
        _sb_kernel,
        grid=(N_SB_HEADS // 2, s // tq),
        in_specs=[
            pl.BlockSpec((tq, LANES), lambda h, i: (i, h)),
            pl.BlockSpec((LANES, s), lambda h, i: (h, 0)),
            pl.BlockSpec((s, LANES), lambda h, i: (0, h)),
            pl.BlockSpec(g2.shape, lambda h, i: (0, 0)),
        ],
        out_specs=pl.BlockSpec((tq, LANES), lambda h, i: (i, h)),
        out_shape=jax.ShapeDtypeStruct((s, SB_WIDTH), BF16),
        scratch_shapes=[pltpu.VMEM((tq // t, 2, t, LANES), F32), pltpu.VMEM((tq // t, t, LANES), F32),
                        pltpu.VMEM((8, LANES), F32),
                        pltpu.SMEM((tq // t,), F32), pltpu.SMEM((tq // t,), jnp.int32)],
        compiler_params=pltpu.CompilerParams(
            dimension_semantics=("arbitrary", "arbitrary"), vmem_limit_bytes=VMEM_LIMIT),
        name="sb_attn",
    )(qs, kst, vs, g2)


def _out_kernel(d_ref, s_ref, wd_ref, ws_ref, x_ref, gpost_ref, gpre_ref, x1_ref, h2_ref):
    mix = (jnp.dot(d_ref[...], wd_ref[...], preferred_element_type=F32)
           + jnp.dot(s_ref[...], ws_ref[...], preferred_element_type=F32))
    x1 = x_ref[...] + _rms(mix, gpost_ref[...])
    x1_ref[...] = x1
    h2_ref[...] = _rms(x1, gpre_ref[...]).astype(h2_ref.dtype)


def _out_proj(diff_o, sb_o, w_od, w_os, x2, g_post, g_pre):
    s = x2.shape[0]
    rows = OUT_ROWS
    row_spec = lambda w: pl.BlockSpec((rows, w), lambda i: (i, 0))
    full = lambda a: pl.BlockSpec(a.shape, lambda i: (0, 0))
    return pl.pallas_call(
        _out_kernel,
        grid=(s // rows,),
        in_specs=[row_spec(512), row_spec(512), full(w_od), full(w_os), row_spec(D_MODEL),
                  full(g_post), full(g_pre)],
        out_specs=[row_spec(D_MODEL), row_spec(D_MODEL)],
        out_shape=[jax.ShapeDtypeStruct((s, D_MODEL), F32), jax.ShapeDtypeStruct((s, D_MODEL), BF16)],
        compiler_params=pltpu.CompilerParams(
            dimension_semantics=("arbitrary",), vmem_limit_bytes=VMEM_LIMIT),
        name="out_proj",
    )(diff_o, sb_o, w_od, w_os, x2, g_post, g_pre)


def _gelu_tanh(x):
    return 0.5 * x * (1.0 + jnp.tanh(math.sqrt(2.0 / math.pi) * (x + 0.044715 * (x * x * x))))


def _ffn_kernel(h_ref, halo_ref, wg_ref, wv_ref, cwg_ref, cwv_ref, cbg_ref, cbv_ref, wd_ref,
                x1_ref, g_ref, o_ref, ug_sc, uv_sc, acc_sc):
    i = pl.program_id(0)
    f = pl.program_id(1)
    rows = FFN_ROWS
    h = h_ref[...]
    halo = jnp.where(i > 0, halo_ref[...], jnp.zeros_like(halo_ref[...]))

    def conv(w_ref, cw_ref, cb_ref, u_sc):
        u_sc[0:HALO_ROWS, :] = jnp.dot(halo, w_ref[...], preferred_element_type=F32)
        u_sc[HALO_ROWS:, :] = jnp.dot(h, w_ref[...], preferred_element_type=F32)
        cw = cw_ref[...]
        y = cb_ref[...]
        for k in range(CONV_WIDTH):
            off = HALO_ROWS - (CONV_WIDTH - 1) + k
            y = y + u_sc[off:off + rows, :] * cw[k:k + 1]
        return y

    gate = conv(wg_ref, cwg_ref, cbg_ref, ug_sc)
    val = conv(wv_ref, cwv_ref, cbv_ref, uv_sc)
    act = (_gelu_tanh(gate) * val).astype(BF16)
    part = jnp.dot(act, wd_ref[...], preferred_element_type=F32)

    @pl.when(f == 0)
    def _():
        acc_sc[...] = part

    @pl.when(f > 0)
    def _():
        acc_sc[...] += part

    @pl.when(f == pl.num_programs(1) - 1)
    def _():
        o_ref[...] = x1_ref[...] + _rms(acc_sc[...], g_ref[...])


def _ffn(h2, w_up, conv_w, conv_b, w_down, x1, g_post):
    s = h2.shape[0]
    rows, chunk = FFN_ROWS, FFN_CHUNK
    n_chunk = D_FF // chunk
    halo_blocks = rows // HALO_ROWS
    row_spec = pl.BlockSpec((rows, D_MODEL), lambda i, f: (i, 0))
    halo_spec = pl.BlockSpec((HALO_ROWS, D_MODEL),
                             lambda i, f: (jnp.maximum(i * halo_blocks - 1, 0), 0))
    gate_cols = lambda r: pl.BlockSpec((r, chunk), lambda i, f: (0, f))
    val_cols = lambda r: pl.BlockSpec((r, chunk), lambda i, f: (0, f + n_chunk))
    return pl.pallas_call(
        _ffn_kernel,
        grid=(s // rows, n_chunk),
        in_specs=[row_spec, halo_spec, gate_cols(D_MODEL), val_cols(D_MODEL),
                  gate_cols(CONV_WIDTH), val_cols(CONV_WIDTH), gate_cols(1), val_cols(1),
                  pl.BlockSpec((chunk, D_MODEL), lambda i, f: (f, 0)),
                  row_spec, pl.BlockSpec(g_post.shape, lambda i, f: (0, 0))],
        out_specs=row_spec,
        out_shape=jax.ShapeDtypeStruct((s, D_MODEL), F32),
        scratch_shapes=[pltpu.VMEM((HALO_ROWS + rows, chunk), F32)] * 2
        + [pltpu.VMEM((rows, D_MODEL), F32)],
        compiler_params=pltpu.CompilerParams(
            dimension_semantics=("arbitrary", "arbitrary"), vmem_limit_bytes=VMEM_LIMIT),
        name="ffn",
    )(h2, h2, w_up, w_up, conv_w, conv_w, conv_b, conv_b, w_down, x1, g_post)


def kernel(x, attn_pre_norm, w_qkv, lambda_q1, lambda_k1, lambda_q2, lambda_k2, diff_subln, sb_norm,
           rel_bias, w_o, attn_post_norm, ffn_pre_norm, w_up, conv_w, conv_b, w_down, ffn_post_norm):
    b, s, _ = x.shape
    assert b == 1 and attn_pre_norm.shape[0] == 1, "single sequence, single layer"
    assert s % max(PROJ_ROWS, DIFF_BLOCK, SB_Q_BLOCK, OUT_ROWS, FFN_ROWS) == 0
    x2 = x[0]
    row = lambda a: a.reshape(1, -1).astype(F32)

    wq = w_qkv[0].astype(BF16)
    w_cat = jnp.concatenate([wq[:, 0:512], wq[:, 1024:1536], wq[:, 1536:2048], wq[:, 2560:3072]], axis=1)
    wkt_cat = jnp.concatenate([wq[:, 512:1024], wq[:, 2048:2560]], axis=1).T
    qd, vd, qs, vs, kdt, kst = _proj(x2, row(attn_pre_norm), w_cat, wkt_cat)

    bias = _bias_tiles(rel_bias.astype(F32))
    lam_params = jnp.concatenate([lambda_q1, lambda_k1, lambda_q2, lambda_k2], axis=0).astype(F32)
    diff_o = _diff_attention(lam_params, qd, kdt, vd, bias, row(diff_subln))
    sb_o = _sb_attention(qs, kst, vs, row(jnp.concatenate([sb_norm[0], sb_norm[0]])))

    wo = w_o[0].astype(BF16)
    x1, h2 = _out_proj(diff_o, sb_o, wo[:DIFF_WIDTH], wo[DIFF_WIDTH:], x2,
                       row(attn_post_norm), row(ffn_pre_norm))
    out = _ffn(h2, w_up[0].astype(BF16), conv_w[0].astype(F32), conv_b.astype(F32),
               w_down[0].astype(BF16), x1, row(ffn_post_norm))
    return out[None]
```

```python
import functools
import math

import jax
import jax.numpy as jnp
from jax import lax
from jax.experimental import pallas as pl
from jax.experimental.pallas import tpu as pltpu

F32 = jnp.float32
BF16 = jnp.bfloat16

D_MODEL = 1024
HEAD_DIM = 64
N_DIFF_HEADS = 4
N_SB_HEADS = 8
DIFF_WIDTH = N_DIFF_HEADS * 2 * HEAD_DIM
SB_WIDTH = N_SB_HEADS * HEAD_DIM
D_FF = 2816
CONV_WIDTH = 3
N_BUCKETS = 32
MAX_DISTANCE = 128
RMS_EPS = 1e-6
LAMBDA_INIT = 0.8 - 0.6 * math.exp(-0.3 * 0)

LOG2E = 1.4426950408889634
Q_SCALE = HEAD_DIM ** -0.5 * LOG2E
MASK_VALUE = -1e30
EXP2_UNDERFLOW = 160.0
SOFTMAX_MIN_DENOMINATOR = 2.0 ** -60

LANES = 128
PROJ_ROWS = 512
DIFF_BLOCK = 512
SB_Q_BLOCK = 1024
SB_K_BLOCK = 256
OUT_ROWS = 512
FFN_ROWS = 512
FFN_CHUNK = D_FF // 2
HALO_ROWS = 8
VMEM_LIMIT = 48 * 1024 * 1024
DIFF_VMEM_LIMIT = 58 * 1024 * 1024
FFN_VMEM_LIMIT = 58 * 1024 * 1024


def _bits(x):
    return lax.bitcast_convert_type(x, jnp.uint32)


def _from_bits(u):
    return lax.bitcast_convert_type(u, F32)


def _tile_lanes(x, k):
    return jnp.concatenate([x] * k, axis=1) if k > 1 else x


def _rms(x, g):
    return x * lax.rsqrt(jnp.mean(x * x, axis=-1, keepdims=True) + RMS_EPS) * g


def _proj_kernel(x_ref, g_ref, w_ref, wkt_ref, qd_ref, vd_ref, qs_ref, vs_ref, kdt_ref, kst_ref):
    hb = _rms(x_ref[...], g_ref[...]).astype(BF16)
    o = jnp.dot(hb, w_ref[...], preferred_element_type=F32)
    qd_ref[...] = (o[:, 0:512] * Q_SCALE).astype(BF16)
    vd_ref[...] = o[:, 512:1024].astype(BF16)
    qs_ref[...] = (o[:, 1024:1536] * Q_SCALE).astype(BF16)
    vs_ref[...] = o[:, 1536:2048].astype(BF16)
    kt = lax.dot_general(wkt_ref[...], hb, (((1,), (1,)), ((), ())), preferred_element_type=F32)
    kdt_ref[...] = kt[0:512].astype(BF16)
    kst_ref[...] = kt[512:1024].astype(BF16)


def _proj(x2, g, w_cat, wkt_cat):
    s = x2.shape[0]
    rows = PROJ_ROWS
    row_spec = lambda w: pl.BlockSpec((rows, w), lambda i: (i, 0))
    full = lambda a: pl.BlockSpec(a.shape, lambda i: (0, 0))
    out_rows = jax.ShapeDtypeStruct((s, 512), BF16)
    out_t = jax.ShapeDtypeStruct((512, s), BF16)
    return pl.pallas_call(
        _proj_kernel,
        grid=(s // rows,),
        in_specs=[row_spec(D_MODEL), full(g), full(w_cat), full(wkt_cat)],
        out_specs=[row_spec(512)] * 4 + [pl.BlockSpec((512, rows), lambda i: (0, i))] * 2,
        out_shape=[out_rows] * 4 + [out_t] * 2,
        compiler_params=pltpu.CompilerParams(
            dimension_semantics=("arbitrary",), vmem_limit_bytes=VMEM_LIMIT),
        name="proj",
    )(x2, g, w_cat, wkt_cat)


def _bias_kernel(tab_ref, o_ref):
    h = pl.program_id(0)
    w = pl.program_id(1)
    t = DIFF_BLOCK
    rel = (lax.broadcasted_iota(jnp.int32, (t, t), 0) - lax.broadcasted_iota(jnp.int32, (t, t), 1)
           + w * t)
    n = jnp.maximum(rel, 0)
    max_exact = N_BUCKETS // 2
    nf = jnp.maximum(n, 1).astype(F32)
    large = max_exact + (jnp.log(nf / max_exact) / math.log(MAX_DISTANCE / max_exact)
                         * (N_BUCKETS - max_exact)).astype(jnp.int32)
    large = jnp.minimum(large, N_BUCKETS - 1)
    bucket = jnp.where(n < max_exact, n, large)
    b = jnp.zeros((t, t), F32)
    for k in range(N_BUCKETS):
        b = jnp.where(bucket == k, tab_ref[k, h], b)
    b = (b - tab_ref[N_BUCKETS - 1, h]) * LOG2E
    o_ref[0, 0] = jnp.where(rel >= 0, b, MASK_VALUE)


def _bias_tiles(rel_bias):
    t = DIFF_BLOCK
    return pl.pallas_call(
        _bias_kernel,
        grid=(N_DIFF_HEADS, 2),
        in_specs=[pl.BlockSpec(memory_space=pltpu.SMEM)],
        out_specs=pl.BlockSpec((1, 1, t, t), lambda h, w: (h, w, 0, 0)),
        out_shape=jax.ShapeDtypeStruct((N_DIFF_HEADS, 2, t, t), F32),
        compiler_params=pltpu.CompilerParams(
            dimension_semantics=("arbitrary", "arbitrary"), vmem_limit_bytes=VMEM_LIMIT),
        name="bias",
    )(rel_bias)


def _diff_kernel(lam_ref, q_ref, kt_ref, v_ref, bias_ref, g_ref, o_ref,
                 acc_sc, stat_sc, s_sc, m_sc, l_sc, slow_acc_sc):
    i = pl.program_id(1)
    t = DIFF_BLOCK
    tq = 2 * t
    half_a, half_b = slice(0, t), slice(t, tq)
    lane = lax.broadcasted_iota(jnp.int32, (tq, LANES), 1)
    q = q_ref[...]
    n_pairs = jnp.maximum(i - 1, 0)

    @pl.when(i == 0)
    def _():
        def norm_body(c, ms):
            k = kt_ref[:, pl.ds(pl.multiple_of(c * t, t), t)].astype(F32)
            sq = k * k
            return (jnp.maximum(ms[0], jnp.sum(sq[:HEAD_DIM], axis=0, keepdims=True)),
                    jnp.maximum(ms[1], jnp.sum(sq[HEAD_DIM:], axis=0, keepdims=True)))

        zeros = jnp.zeros((1, t), F32)
        ms = lax.fori_loop(0, kt_ref.shape[1] // t, norm_body, (zeros, zeros))
        for mp in range(2):
            stat_sc[mp] = jnp.broadcast_to(jnp.max(ms[mp], axis=1, keepdims=True), stat_sc.shape[1:])
        bias_max = jnp.max(jnp.max(bias_ref[0, 0], axis=1, keepdims=True), axis=0, keepdims=True)
        stat_sc[2] = jnp.broadcast_to(bias_max, stat_sc.shape[1:])

    def write_out(rows, o0, o1):
        lp = lam_ref[...]
        lam = (jnp.exp(jnp.sum(lp[0:1] * lp[1:2], axis=1, keepdims=True))
               - jnp.exp(jnp.sum(lp[2:3] * lp[3:4], axis=1, keepdims=True)) + LAMBDA_INIT)
        o_ref[rows, :] = (_rms(o0 - lam * o1, g_ref[...]) * (1.0 - LAMBDA_INIT)).astype(o_ref.dtype)

    zero_q = jnp.zeros_like(q)
    qm = (jnp.where(lane < HEAD_DIM, q, zero_q), jnp.where(lane >= HEAD_DIM, q, zero_q))
    qsq = q.astype(F32) * q.astype(F32)
    bound = []
    for mp in range(2):
        own = (lane < HEAD_DIM) if mp == 0 else (lane >= HEAD_DIM)
        qn = jnp.sum(jnp.where(own, qsq, 0.0), axis=1, keepdims=True)
        b = jnp.sqrt(qn * stat_sc[mp][0:1, 0:1]) * (1.0 + 2.0 ** -10) + stat_sc[2][0:1, 0:1]
        bound.append(jnp.broadcast_to(b, (tq, LANES)))

    acc_sc[...] = jnp.zeros(acc_sc.shape, F32)

    def scores(rows, start, width):
        kt = kt_ref[:, pl.ds(start, width)]
        return [jnp.dot(qm[mp][rows], kt, preferred_element_type=F32)
                - _tile_lanes(bound[mp][rows], width // LANES) for mp in range(2)]

    def accumulate(rows, s, start, width):
        v_aug = jnp.concatenate([v_ref[pl.ds(start, width), :], jnp.ones((width, LANES), BF16)], axis=1)
        for mp in range(2):
            acc_sc[mp, rows, :] += jnp.dot(jnp.exp2(s[mp]).astype(BF16), v_aug, preferred_element_type=F32)

    every = slice(0, tq)
    pair = 2 * t

    @pl.when(n_pairs > 0)
    def _():
        first = scores(every, 0, pair)
        for mp in range(2):
            s_sc[mp] = first[mp]

        def far_body(r, carry):
            cur = [s_sc[mp] for mp in range(2)]
            nxt = scores(every, pl.multiple_of((r + 1) * pair, pair), pair)
            accumulate(every, cur, pl.multiple_of(r * pair, pair), pair)
            for mp in range(2):
                s_sc[mp] = nxt[mp]
            return carry

        lax.fori_loop(0, n_pairs - 1, far_body, 0)
        accumulate(every, [s_sc[mp] for mp in range(2)], pl.multiple_of((n_pairs - 1) * pair, pair), pair)

    bias_diag, bias_prev = bias_ref[0, 0], bias_ref[0, 1]

    def last_block_of_b(block):
        start = pl.multiple_of(block * t, t)
        s = [x + bias_diag for x in scores(half_b, start, t)]
        accumulate(half_b, s, start, t)

    @pl.when(i == 0)
    def _():
        s = [x + jnp.concatenate([bias_diag, bias_prev], axis=0) for x in scores(every, 0, t)]
        accumulate(every, s, 0, t)
        last_block_of_b(1)

    @pl.when(i >= 1)
    def _():
        start = pl.multiple_of((2 * i - 2) * t, t)
        s = scores(every, start, 3 * t)
        for mp in range(2):
            x = s[mp]
            s[mp] = jnp.concatenate([
                x[:, :t],
                jnp.concatenate([x[half_a, t:2 * t] + bias_prev, x[half_b, t:2 * t]], axis=0),
                jnp.concatenate([x[half_a, 2 * t:] + bias_diag, x[half_b, 2 * t:] + bias_prev], axis=0),
            ], axis=1)
        accumulate(every, s, start, 3 * t)
        last_block_of_b(2 * i + 1)

    l0, l1 = acc_sc[0][:, LANES:], acc_sc[1][:, LANES:]
    write_out(every, acc_sc[0][:, :LANES] / l0, acc_sc[1][:, :LANES] / l1)

    @pl.when(jnp.logical_not(jnp.min(jnp.minimum(l0, l1)) >= SOFTMAX_MIN_DENOMINATOR))
    def _():
        def online_softmax_half(rows, diag_block):
            m_sc[...] = jnp.full(m_sc.shape, MASK_VALUE, F32)
            l_sc[...] = jnp.zeros(l_sc.shape, F32)
            slow_acc_sc[...] = jnp.zeros(slow_acc_sc.shape, F32)

            def step(j, bias):
                start = pl.multiple_of(j * t, t)
                kt = kt_ref[:, pl.ds(start, t)]
                v = v_ref[pl.ds(start, t), :]
                for mp in range(2):
                    s = jnp.dot(qm[mp][rows], kt, preferred_element_type=F32)
                    if bias is not None:
                        s = s + bias
                    m_prev = m_sc[mp]
                    m_new = jnp.maximum(m_prev, jnp.max(s, axis=1, keepdims=True))
                    alpha = jnp.exp2(m_prev - m_new)
                    p = jnp.exp2(s - _tile_lanes(m_new, t // LANES))
                    l_sc[mp] = alpha * l_sc[mp] + jnp.sum(p, axis=1, keepdims=True)
                    slow_acc_sc[mp] = (alpha * slow_acc_sc[mp]
                                       + jnp.dot(p.astype(BF16), v, preferred_element_type=F32))
                    m_sc[mp] = m_new

            def slow_body(j, carry):
                step(j, None)
                return carry

            lax.fori_loop(0, jnp.maximum(diag_block - 1, 0), slow_body, 0)

            @pl.when(diag_block >= 1)
            def _():
                step(diag_block - 1, bias_prev)

            step(diag_block, bias_diag)
            write_out(rows, slow_acc_sc[0] / l_sc[0], slow_acc_sc[1] / l_sc[1])

        online_softmax_half(half_a, 2 * i)
        online_softmax_half(half_b, 2 * i + 1)


def _diff_attention(lam_params, qd, kdt, vd, bias, g):
    s = qd.shape[0]
    t = DIFF_BLOCK
    tq = 2 * t
    return pl.pallas_call(
        _diff_kernel,
        grid=(N_DIFF_HEADS, s // tq),
        in_specs=[
            pl.BlockSpec(lam_params.shape, lambda h, i: (0, 0)),
            pl.BlockSpec((tq, LANES), lambda h, i: (i, h)),
            pl.BlockSpec((LANES, s), lambda h, i: (h, 0)),
            pl.BlockSpec((s, LANES), lambda h, i: (0, h)),
            pl.BlockSpec((1, 2, t, t), lambda h, i: (h, 0, 0, 0)),
            pl.BlockSpec(g.shape, lambda h, i: (0, 0)),
        ],
        out_specs=pl.BlockSpec((tq, LANES), lambda h, i: (i, h)),
        out_shape=jax.ShapeDtypeStruct((s, DIFF_WIDTH), BF16),
        scratch_shapes=[pltpu.VMEM((2, tq, 2 * LANES), F32), pltpu.VMEM((3, 8, LANES), F32),
                        pltpu.VMEM((2, tq, 2 * t), F32)]
        + [pltpu.VMEM((2, t, LANES), F32)] * 3,
        compiler_params=pltpu.CompilerParams(
            dimension_semantics=("arbitrary", "arbitrary"), vmem_limit_bytes=DIFF_VMEM_LIMIT),
        name="diff_attn",
    )(lam_params, qd, kdt, vd, bias, g)


def _sb_kernel(q_ref, kt_ref, v_ref, g_ref, o_ref, carry_sc, acc_sc, knorm_sc, thresh_sm, done_sm):
    i = pl.program_id(1)
    t = SB_K_BLOCK
    n_sub = SB_Q_BLOCK // t
    lane = lax.broadcasted_iota(jnp.int32, (t, LANES), 1)
    head0 = lane < HEAD_DIM
    lower = (lax.broadcasted_iota(jnp.int32, (t, t), 0)
             >= lax.broadcasted_iota(jnp.int32, (t, t), 1)).astype(BF16)
    strict = lax.broadcasted_iota(jnp.int32, (t, t), 0) > lax.broadcasted_iota(jnp.int32, (t, t), 1)

    @pl.when(i == 0)
    def _():
        def norm_body(c, m):
            k = kt_ref[:, pl.ds(pl.multiple_of(c * t, t), t)].astype(F32)
            sq = k * k
            return jnp.maximum(m, jnp.maximum(jnp.sum(sq[:HEAD_DIM], axis=0, keepdims=True),
                                              jnp.sum(sq[HEAD_DIM:], axis=0, keepdims=True)))

        m = lax.fori_loop(0, kt_ref.shape[1] // t, norm_body, jnp.zeros((1, t), F32))
        knorm_sc[...] = jnp.broadcast_to(jnp.max(m, axis=1, keepdims=True), knorm_sc.shape)

    def load_q(u):
        q = q_ref[pl.ds(pl.multiple_of(u * t, t), t), :]
        zero_q = jnp.zeros_like(q)
        return q, (jnp.where(head0, q, zero_q), jnp.where(head0, zero_q, q))

    def step(u, qm, jb, masked):
        start = pl.multiple_of(jb * t, t)
        kt = kt_ref[:, pl.ds(start, t)]
        v = v_ref[pl.ds(start, t), :]
        zero_v = jnp.zeros_like(v)
        v2 = jnp.concatenate([jnp.where(head0, v, zero_v), jnp.where(head0, zero_v, v)], axis=0)
        probs = []
        for hh in range(2):
            z = jnp.dot(qm[hh], kt, preferred_element_type=F32)
            neg_abs = _from_bits(_bits(z) | jnp.uint32(0x80000000))
            nl = jnp.maximum(z, 0.0) + jnp.log(1.0 + jnp.exp2(neg_abs)) * LOG2E
            if masked:
                nl = jnp.where(strict, nl, 0.0)
            csum = jnp.dot(nl.astype(BF16), lower, preferred_element_type=F32)
            carry = carry_sc[u, hh]
            a = jnp.exp2((z - _tile_lanes(carry, t // LANES)) - csum)
            if masked:
                a = jnp.where(strict, a, 0.0)
            probs.append(a.astype(BF16))
            carry_sc[u, hh] = carry + csum[:, 0:1]
        acc_sc[u] += jnp.dot(jnp.concatenate(probs, axis=1), v2, preferred_element_type=F32)

    carry_sc[...] = jnp.zeros(carry_sc.shape, F32)
    acc_sc[...] = jnp.zeros(acc_sc.shape, F32)

    qs = [load_q(u) for u in range(n_sub)]
    for u in range(n_sub):
        step(u, qs[u][1], i * n_sub + u, True)
    for u in range(1, n_sub):
        step(u, qs[u][1], i * n_sub + u - 1, False)

    def carry_min(u):
        return jnp.min(jnp.minimum(carry_sc[u, 0], carry_sc[u, 1]))

    for u in range(n_sub):
        qsq = qs[u][0].astype(F32) * qs[u][0].astype(F32)
        qn0 = jnp.sum(jnp.where(head0, qsq, 0.0), axis=1, keepdims=True)
        qn1 = jnp.sum(jnp.where(head0, 0.0, qsq), axis=1, keepdims=True)
        qnorm = jnp.max(jnp.maximum(qn0, qn1), axis=0, keepdims=True)
        thresh = jnp.sqrt(qnorm * knorm_sc[0:1, 0:1]) * (1.0 + 2.0 ** -10) + EXP2_UNDERFLOW
        thresh_sm[u] = jnp.max(thresh)
        done_sm[u] = (carry_min(u) >= thresh_sm[u]).astype(jnp.int32)

    def remaining_blocks(u, c):
        _, qm = load_q(u)

        def far_cond(state):
            jb, all_zero = state
            return jnp.logical_and(jb >= 0, all_zero == 0)

        def far_body(state):
            jb, _ = state
            step(u, qm, jb, False)
            return jb - 1, (carry_min(u) >= thresh_sm[u]).astype(jnp.int32)

        first = i * n_sub + u - 1 - jnp.minimum(u, 1)
        lax.while_loop(far_cond, far_body, (first, done_sm[u]))
        return c

    lax.fori_loop(0, n_sub, remaining_blocks, 0)

    for u in range(n_sub):
        o = acc_sc[u]
        sq = o * o
        ss0 = jnp.sum(jnp.where(head0, sq, 0.0), axis=1, keepdims=True)
        ss1 = jnp.sum(jnp.where(head0, 0.0, sq), axis=1, keepdims=True)
        ms = jnp.where(head0, ss0, ss1) * (1.0 / HEAD_DIM)
        o_ref[u * t:(u + 1) * t, :] = (o * lax.rsqrt(ms + RMS_EPS) * g_ref[...]).astype(o_ref.dtype)


def _sb_attention(qs, kst, vs, g2):
    s = qs.shape[0]
    tq, t = SB_Q_BLOCK, SB_K_BLOCK
    return pl.pallas_call(
        _sb_kernel,
        grid=(N_SB_HEADS // 2, s // tq),
        in_specs=[
            pl.BlockSpec((tq, LANES), lambda h, i: (i, h)),
            pl.BlockSpec((LANES, s), lambda h, i: (h, 0)),
            pl.BlockSpec((s, LANES), lambda h, i: (0, h)),
            pl.BlockSpec(g2.shape, lambda h, i: (0, 0)),
        ],
        out_specs=pl.BlockSpec((tq, LANES), lambda h, i: (i, h)),
        out_shape=jax.ShapeDtypeStruct((s, SB_WIDTH), BF16),
        scratch_shapes=[pltpu.VMEM((tq // t, 2, t, LANES), F32), pltpu.VMEM((tq // t, t, LANES), F32),
                        pltpu.VMEM((8, LANES), F32),
                        pltpu.SMEM((tq // t,), F32), pltpu.SMEM((tq // t,), jnp.int32)],
        compiler_params=pltpu.CompilerParams(
            dimension_semantics=("arbitrary", "arbitrary"), vmem_limit_bytes=VMEM_LIMIT),
        name="sb_attn",
    )(qs, kst, vs, g2)


def _out_kernel(d_ref, s_ref, wd_ref, ws_ref, x_ref, gpost_ref, gpre_ref, x1_ref, h2_ref):
    mix = (jnp.dot(d_ref[...], wd_ref[...], preferred_element_type=F32)
           + jnp.dot(s_ref[...], ws_ref[...], preferred_element_type=F32))
    x1 = x_ref[...] + _rms(mix, gpost_ref[...])
    x1_ref[...] = x1
    h2_ref[...] = _rms(x1, gpre_ref[...]).astype(h2_ref.dtype)


def _out_proj(diff_o, sb_o, w_od, w_os, x2, g_post, g_pre):
    s = x2.shape[0]
    rows = OUT_ROWS
    row_spec = lambda w: pl.BlockSpec((rows, w), lambda i: (i, 0))
    full = lambda a: pl.BlockSpec(a.shape, lambda i: (0, 0))
    return pl.pallas_call(
        _out_kernel,
        grid=(s // rows,),
        in_specs=[row_spec(512), row_spec(512), full(w_od), full(w_os), row_spec(D_MODEL),
                  full(g_post), full(g_pre)],
        out_specs=[row_spec(D_MODEL), row_spec(D_MODEL)],
        out_shape=[jax.ShapeDtypeStruct((s, D_MODEL), F32), jax.ShapeDtypeStruct((s, D_MODEL), BF16)],
        compiler_params=pltpu.CompilerParams(
            dimension_semantics=("arbitrary",), vmem_limit_bytes=VMEM_LIMIT),
        name="out_proj",
    )(diff_o, sb_o, w_od, w_os, x2, g_post, g_pre)


def _gelu_tanh(x):
    return 0.5 * x * (1.0 + jnp.tanh(math.sqrt(2.0 / math.pi) * (x + 0.044715 * (x * x * x))))


def _ffn_kernel(h_ref, halo_ref, wup_ref, cw_ref, cb_ref, wd_ref, x1_ref, g_ref, o_ref, u_sc):
    i = pl.program_id(0)
    rows, chunk = FFN_ROWS, FFN_CHUNK
    h = h_ref[...]
    halo = jnp.where(i > 0, halo_ref[...], jnp.zeros_like(halo_ref[...]))

    def conv(col, slot):
        cols = slice(col, col + chunk)
        u_sc[slot, 0:HALO_ROWS, :] = jnp.dot(halo, wup_ref[:, cols], preferred_element_type=F32)
        u_sc[slot, HALO_ROWS:, :] = jnp.dot(h, wup_ref[:, cols], preferred_element_type=F32)
        y = cb_ref[:, cols]
        for k in range(CONV_WIDTH):
            off = HALO_ROWS - (CONV_WIDTH - 1) + k
            y = y + u_sc[slot, off:off + rows, :] * cw_ref[k:k + 1, cols]
        return y

    y = None
    for c in range(D_FF // chunk):
        gate = conv(c * chunk, 2 * c)
        val = conv(D_FF + c * chunk, 2 * c + 1)
        act = (_gelu_tanh(gate) * val).astype(BF16)
        part = jnp.dot(act, wd_ref[c * chunk:(c + 1) * chunk, :], preferred_element_type=F32)
        y = part if y is None else y + part
    o_ref[...] = x1_ref[...] + _rms(y, g_ref[...])


def _ffn(h2, w_up, conv_w, conv_b, w_down, x1, g_post):
    s = h2.shape[0]
    rows = FFN_ROWS
    halo_blocks = rows // HALO_ROWS
    row_spec = pl.BlockSpec((rows, D_MODEL), lambda i: (i, 0))
    halo_spec = pl.BlockSpec((HALO_ROWS, D_MODEL), lambda i: (jnp.maximum(i * halo_blocks - 1, 0), 0))
    resident = lambda a: pl.BlockSpec(a.shape, lambda i: (0, 0), pipeline_mode=pl.Buffered(1))
    return pl.pallas_call(
        _ffn_kernel,
        grid=(s // rows,),
        in_specs=[row_spec, halo_spec, resident(w_up), resident(conv_w), resident(conv_b),
                  resident(w_down), row_spec, resident(g_post)],
        out_specs=row_spec,
        out_shape=jax.ShapeDtypeStruct((s, D_MODEL), F32),
        scratch_shapes=[pltpu.VMEM((2 * (D_FF // FFN_CHUNK), HALO_ROWS + rows, FFN_CHUNK), F32)],
        compiler_params=pltpu.CompilerParams(
            dimension_semantics=("arbitrary",), vmem_limit_bytes=FFN_VMEM_LIMIT),
        name="ffn",
    )(h2, h2, w_up, conv_w, conv_b, w_down, x1, g_post)


def kernel(x, attn_pre_norm, w_qkv, lambda_q1, lambda_k1, lambda_q2, lambda_k2, diff_subln, sb_norm,
           rel_bias, w_o, attn_post_norm, ffn_pre_norm, w_up, conv_w, conv_b, w_down, ffn_post_norm):
    b, s, _ = x.shape
    assert b == 1 and attn_pre_norm.shape[0] == 1, "single sequence, single layer"
    assert s % max(PROJ_ROWS, 2 * DIFF_BLOCK, SB_Q_BLOCK, OUT_ROWS, FFN_ROWS) == 0
    x2 = x[0]
    row = lambda a: a.reshape(1, -1).astype(F32)

    wq = w_qkv[0].astype(BF16)
    w_cat = jnp.concatenate([wq[:, 0:512], wq[:, 1024:1536], wq[:, 1536:2048], wq[:, 2560:3072]], axis=1)
    wkt_cat = jnp.concatenate([wq[:, 512:1024], wq[:, 2048:2560]], axis=1).T
    qd, vd, qs, vs, kdt, kst = _proj(x2, row(attn_pre_norm), w_cat, wkt_cat)

    bias = _bias_tiles(rel_bias.astype(F32))
    lam_params = jnp.concatenate([lambda_q1, lambda_k1, lambda_q2, lambda_k2], axis=0).astype(F32)
    diff_o = _diff_attention(lam_params, qd, kdt, vd, bias, row(diff_subln))
    sb_o = _sb_attention(qs, kst, vs, row(jnp.concatenate([sb_norm[0], sb_norm[0]])))

    wo = w_o[0].astype(BF16)
    x1, h2 = _out_proj(diff_o, sb_o, wo[:DIFF_WIDTH], wo[DIFF_WIDTH:], x2,
                       row(attn_post_norm), row(ffn_pre_norm))
    out = _ffn(h2, w_up[0].astype(BF16), conv_w[0].astype(F32), conv_b.astype(F32),
               w_down[0].astype(BF16), x1, row(ffn_post_norm))
    return out[None]
```

```python
import functools
import math

import jax
import jax.numpy as jnp
from jax import lax
from jax.experimental import pallas as pl
from jax.experimental.pallas import tpu as pltpu

F32 = jnp.float32
BF16 = jnp.bfloat16

D_MODEL = 1024
HEAD_DIM = 64
N_DIFF_HEADS = 4
N_SB_HEADS = 8
DIFF_WIDTH = N_DIFF_HEADS * 2 * HEAD_DIM
SB_WIDTH = N_SB_HEADS * HEAD_DIM
D_FF = 2816
CONV_WIDTH = 3
N_BUCKETS = 32
MAX_DISTANCE = 128
RMS_EPS = 1e-6
LAMBDA_INIT = 0.8 - 0.6 * math.exp(-0.3 * 0)

LOG2E = 1.4426950408889634
Q_SCALE = HEAD_DIM ** -0.5 * LOG2E
MASK_VALUE = -1e30
EXP2_UNDERFLOW = 160.0
SOFTMAX_MIN_DENOMINATOR = 2.0 ** -60

LANES = 128
PROJ_ROWS = 512
DIFF_BLOCK = 512
SB_Q_BLOCK = 1024
SB_K_BLOCK = 256
OUT_ROWS = 512
FFN_ROWS = 512
FFN_CHUNK = D_FF // 2
HALO_ROWS = 8
VMEM_LIMIT = 48 * 1024 * 1024
DIFF_VMEM_LIMIT = 58 * 1024 * 1024
FFN_VMEM_LIMIT = 58 * 1024 * 1024


def _bits(x):
    return lax.bitcast_convert_type(x, jnp.uint32)


def _from_bits(u):
    return lax.bitcast_convert_type(u, F32)


def _tile_lanes(x, k):
    return jnp.concatenate([x] * k, axis=1) if k > 1 else x


def _rms(x, g):
    return x * lax.rsqrt(jnp.mean(x * x, axis=-1, keepdims=True) + RMS_EPS) * g


def _proj_kernel(x_ref, g_ref, w_ref, qd_ref, vd_ref, qs_ref, vs_ref, kdt_ref, kst_ref):
    hb = _rms(x_ref[...], g_ref[...]).astype(BF16)
    proj = lambda c: jnp.dot(hb, w_ref[:, c * 512:(c + 1) * 512], preferred_element_type=F32)
    qd_ref[...] = (proj(0) * Q_SCALE).astype(BF16)
    vd_ref[...] = proj(2).astype(BF16)
    qs_ref[...] = (proj(3) * Q_SCALE).astype(BF16)
    vs_ref[...] = proj(5).astype(BF16)
    proj_t = lambda c: lax.dot_general(w_ref[:, c * 512:(c + 1) * 512], hb, (((0,), (1,)), ((), ())),
                                       preferred_element_type=F32)
    kdt_ref[...] = proj_t(1).astype(BF16)
    kst_ref[...] = proj_t(4).astype(BF16)


def _proj(x2, g, w):
    s = x2.shape[0]
    rows = PROJ_ROWS
    row_spec = lambda w: pl.BlockSpec((rows, w), lambda i: (i, 0))
    full = lambda a: pl.BlockSpec(a.shape, lambda i: (0, 0))
    out_rows = jax.ShapeDtypeStruct((s, 512), BF16)
    out_t = jax.ShapeDtypeStruct((512, s), BF16)
    return pl.pallas_call(
        _proj_kernel,
        grid=(s // rows,),
        in_specs=[row_spec(D_MODEL), full(g), full(w)],
        out_specs=[row_spec(512)] * 4 + [pl.BlockSpec((512, rows), lambda i: (0, i))] * 2,
        out_shape=[out_rows] * 4 + [out_t] * 2,
        compiler_params=pltpu.CompilerParams(
            dimension_semantics=("arbitrary",), vmem_limit_bytes=VMEM_LIMIT),
        name="proj",
    )(x2, g, w)


def _bias_kernel(tab_ref, o_ref):
    h = pl.program_id(0)
    w = pl.program_id(1)
    t = DIFF_BLOCK
    rel = (lax.broadcasted_iota(jnp.int32, (t, t), 0) - lax.broadcasted_iota(jnp.int32, (t, t), 1)
           + w * t)
    n = jnp.maximum(rel, 0)
    max_exact = N_BUCKETS // 2
    nf = jnp.maximum(n, 1).astype(F32)
    large = max_exact + (jnp.log(nf / max_exact) / math.log(MAX_DISTANCE / max_exact)
                         * (N_BUCKETS - max_exact)).astype(jnp.int32)
    large = jnp.minimum(large, N_BUCKETS - 1)
    bucket = jnp.where(n < max_exact, n, large)
    b = jnp.zeros((t, t), F32)
    for k in range(N_BUCKETS):
        b = jnp.where(bucket == k, tab_ref[k, h], b)
    b = (b - tab_ref[N_BUCKETS - 1, h]) * LOG2E
    o_ref[0, 0] = jnp.where(rel >= 0, b, MASK_VALUE)


def _bias_tiles(rel_bias):
    t = DIFF_BLOCK
    return pl.pallas_call(
        _bias_kernel,
        grid=(N_DIFF_HEADS, 2),
        in_specs=[pl.BlockSpec(memory_space=pltpu.SMEM)],
        out_specs=pl.BlockSpec((1, 1, t, t), lambda h, w: (h, w, 0, 0)),
        out_shape=jax.ShapeDtypeStruct((N_DIFF_HEADS, 2, t, t), F32),
        compiler_params=pltpu.CompilerParams(
            dimension_semantics=("arbitrary", "arbitrary"), vmem_limit_bytes=VMEM_LIMIT),
        name="bias",
    )(rel_bias)


def _diff_kernel(lam_ref, q_ref, kt_ref, v_ref, bias_ref, g_ref, o_ref,
                 acc_sc, stat_sc, s_sc, m_sc, l_sc, slow_acc_sc):
    i = pl.program_id(1)
    t = DIFF_BLOCK
    tq = 2 * t
    half_a, half_b = slice(0, t), slice(t, tq)
    lane = lax.broadcasted_iota(jnp.int32, (tq, LANES), 1)
    q = q_ref[...]
    n_pairs = jnp.maximum(i - 1, 0)

    @pl.when(i == 0)
    def _():
        def norm_body(c, ms):
            k = kt_ref[:, pl.ds(pl.multiple_of(c * t, t), t)].astype(F32)
            sq = k * k
            return (jnp.maximum(ms[0], jnp.sum(sq[:HEAD_DIM], axis=0, keepdims=True)),
                    jnp.maximum(ms[1], jnp.sum(sq[HEAD_DIM:], axis=0, keepdims=True)))

        zeros = jnp.zeros((1, t), F32)
        ms = lax.fori_loop(0, kt_ref.shape[1] // t, norm_body, (zeros, zeros))
        for mp in range(2):
            stat_sc[mp] = jnp.broadcast_to(jnp.max(ms[mp], axis=1, keepdims=True), stat_sc.shape[1:])
        bias_max = jnp.max(jnp.max(bias_ref[0, 0], axis=1, keepdims=True), axis=0, keepdims=True)
        stat_sc[2] = jnp.broadcast_to(bias_max, stat_sc.shape[1:])

    def write_out(rows, o0, o1):
        lp = lam_ref[...]
        lam = (jnp.exp(jnp.sum(lp[0:1] * lp[1:2], axis=1, keepdims=True))
               - jnp.exp(jnp.sum(lp[2:3] * lp[3:4], axis=1, keepdims=True)) + LAMBDA_INIT)
        o_ref[rows, :] = (_rms(o0 - lam * o1, g_ref[...]) * (1.0 - LAMBDA_INIT)).astype(o_ref.dtype)

    zero_q = jnp.zeros_like(q)
    qm = (jnp.where(lane < HEAD_DIM, q, zero_q), jnp.where(lane >= HEAD_DIM, q, zero_q))
    qsq = q.astype(F32) * q.astype(F32)
    bound = []
    for mp in range(2):
        own = (lane < HEAD_DIM) if mp == 0 else (lane >= HEAD_DIM)
        qn = jnp.sum(jnp.where(own, qsq, 0.0), axis=1, keepdims=True)
        b = jnp.sqrt(qn * stat_sc[mp][0:1, 0:1]) * (1.0 + 2.0 ** -10) + stat_sc[2][0:1, 0:1]
        bound.append(jnp.broadcast_to(b, (tq, LANES)))

    acc_sc[...] = jnp.zeros(acc_sc.shape, F32)

    def scores(rows, start, width):
        kt = kt_ref[:, pl.ds(start, width)]
        return [jnp.dot(qm[mp][rows], kt, preferred_element_type=F32)
                - _tile_lanes(bound[mp][rows], width // LANES) for mp in range(2)]

    def accumulate(rows, s, start, width):
        v_aug = jnp.concatenate([v_ref[pl.ds(start, width), :], jnp.ones((width, LANES), BF16)], axis=1)
        for mp in range(2):
            acc_sc[mp, rows, :] += jnp.dot(jnp.exp2(s[mp]).astype(BF16), v_aug, preferred_element_type=F32)

    every = slice(0, tq)
    pair = 2 * t

    @pl.when(n_pairs > 0)
    def _():
        first = scores(every, 0, pair)
        for mp in range(2):
            s_sc[mp] = first[mp]

        def far_body(r, carry):
            cur = [s_sc[mp] for mp in range(2)]
            nxt = scores(every, pl.multiple_of((r + 1) * pair, pair), pair)
            accumulate(every, cur, pl.multiple_of(r * pair, pair), pair)
            for mp in range(2):
                s_sc[mp] = nxt[mp]
            return carry

        lax.fori_loop(0, n_pairs - 1, far_body, 0)
        accumulate(every, [s_sc[mp] for mp in range(2)], pl.multiple_of((n_pairs - 1) * pair, pair), pair)

    bias_diag, bias_prev = bias_ref[0, 0], bias_ref[0, 1]

    def last_block_of_b(block):
        start = pl.multiple_of(block * t, t)
        s = [x + bias_diag for x in scores(half_b, start, t)]
        accumulate(half_b, s, start, t)

    @pl.when(i == 0)
    def _():
        s = [x + jnp.concatenate([bias_diag, bias_prev], axis=0) for x in scores(every, 0, t)]
        accumulate(every, s, 0, t)
        last_block_of_b(1)

    @pl.when(i >= 1)
    def _():
        start = pl.multiple_of((2 * i - 2) * t, t)
        s = scores(every, start, 3 * t)
        for mp in range(2):
            x = s[mp]
            s[mp] = jnp.concatenate([
                x[:, :t],
                jnp.concatenate([x[half_a, t:2 * t] + bias_prev, x[half_b, t:2 * t]], axis=0),
                jnp.concatenate([x[half_a, 2 * t:] + bias_diag, x[half_b, 2 * t:] + bias_prev], axis=0),
            ], axis=1)
        accumulate(every, s, start, 3 * t)
        last_block_of_b(2 * i + 1)

    l0, l1 = acc_sc[0][:, LANES:], acc_sc[1][:, LANES:]
    write_out(every, acc_sc[0][:, :LANES] / l0, acc_sc[1][:, :LANES] / l1)

    @pl.when(jnp.logical_not(jnp.min(jnp.minimum(l0, l1)) >= SOFTMAX_MIN_DENOMINATOR))
    def _():
        def online_softmax_half(rows, diag_block):
            m_sc[...] = jnp.full(m_sc.shape, MASK_VALUE, F32)
            l_sc[...] = jnp.zeros(l_sc.shape, F32)
            slow_acc_sc[...] = jnp.zeros(slow_acc_sc.shape, F32)

            def step(j, bias):
                start = pl.multiple_of(j * t, t)
                kt = kt_ref[:, pl.ds(start, t)]
                v = v_ref[pl.ds(start, t), :]
                for mp in range(2):
                    s = jnp.dot(qm[mp][rows], kt, preferred_element_type=F32)
                    if bias is not None:
                        s = s + bias
                    m_prev = m_sc[mp]
                    m_new = jnp.maximum(m_prev, jnp.max(s, axis=1, keepdims=True))
                    alpha = jnp.exp2(m_prev - m_new)
                    p = jnp.exp2(s - _tile_lanes(m_new, t // LANES))
                    l_sc[mp] = alpha * l_sc[mp] + jnp.sum(p, axis=1, keepdims=True)
                    slow_acc_sc[mp] = (alpha * slow_acc_sc[mp]
                                       + jnp.dot(p.astype(BF16), v, preferred_element_type=F32))
                    m_sc[mp] = m_new

            def slow_body(j, carry):
                step(j, None)
                return carry

            lax.fori_loop(0, jnp.maximum(diag_block - 1, 0), slow_body, 0)

            @pl.when(diag_block >= 1)
            def _():
                step(diag_block - 1, bias_prev)

            step(diag_block, bias_diag)
            write_out(rows, slow_acc_sc[0] / l_sc[0], slow_acc_sc[1] / l_sc[1])

        online_softmax_half(half_a, 2 * i)
        online_softmax_half(half_b, 2 * i + 1)


def _diff_attention(lam_params, qd, kdt, vd, bias, g):
    s = qd.shape[0]
    t = DIFF_BLOCK
    tq = 2 * t
    return pl.pallas_call(
        _diff_kernel,
        grid=(N_DIFF_HEADS, s // tq),
        in_specs=[
            pl.BlockSpec(lam_params.shape, lambda h, i: (0, 0)),
            pl.BlockSpec((tq, LANES), lambda h, i: (i, h)),
            pl.BlockSpec((LANES, s), lambda h, i: (h, 0)),
            pl.BlockSpec((s, LANES), lambda h, i: (0, h)),
            pl.BlockSpec((1, 2, t, t), lambda h, i: (h, 0, 0, 0)),
            pl.BlockSpec(g.shape, lambda h, i: (0, 0)),
        ],
        out_specs=pl.BlockSpec((tq, LANES), lambda h, i: (i, h)),
        out_shape=jax.ShapeDtypeStruct((s, DIFF_WIDTH), BF16),
        scratch_shapes=[pltpu.VMEM((2, tq, 2 * LANES), F32), pltpu.VMEM((3, 8, LANES), F32),
                        pltpu.VMEM((2, tq, 2 * t), F32)]
        + [pltpu.VMEM((2, t, LANES), F32)] * 3,
        compiler_params=pltpu.CompilerParams(
            dimension_semantics=("arbitrary", "arbitrary"), vmem_limit_bytes=DIFF_VMEM_LIMIT),
        name="diff_attn",
    )(lam_params, qd, kdt, vd, bias, g)


def _sb_kernel(q_ref, kt_ref, v_ref, g_ref, o_ref, carry_sc, acc_sc, knorm_sc, thresh_sm, done_sm):
    i = pl.program_id(1)
    t = SB_K_BLOCK
    n_sub = SB_Q_BLOCK // t
    lane = lax.broadcasted_iota(jnp.int32, (t, LANES), 1)
    head0 = lane < HEAD_DIM
    lower = (lax.broadcasted_iota(jnp.int32, (t, t), 0)
             >= lax.broadcasted_iota(jnp.int32, (t, t), 1)).astype(BF16)
    strict = lax.broadcasted_iota(jnp.int32, (t, t), 0) > lax.broadcasted_iota(jnp.int32, (t, t), 1)

    @pl.when(i == 0)
    def _():
        def norm_body(c, m):
            k = kt_ref[:, pl.ds(pl.multiple_of(c * t, t), t)].astype(F32)
            sq = k * k
            return jnp.maximum(m, jnp.maximum(jnp.sum(sq[:HEAD_DIM], axis=0, keepdims=True),
                                              jnp.sum(sq[HEAD_DIM:], axis=0, keepdims=True)))

        m = lax.fori_loop(0, kt_ref.shape[1] // t, norm_body, jnp.zeros((1, t), F32))
        knorm_sc[...] = jnp.broadcast_to(jnp.max(m, axis=1, keepdims=True), knorm_sc.shape)

    def load_q(u):
        q = q_ref[pl.ds(pl.multiple_of(u * t, t), t), :]
        zero_q = jnp.zeros_like(q)
        return q, (jnp.where(head0, q, zero_q), jnp.where(head0, zero_q, q))

    def block_pass(jb, consumers):
        start = pl.multiple_of(jb * t, t)
        kt = kt_ref[:, pl.ds(start, t)]
        v = v_ref[pl.ds(start, t), :]
        zero_v = jnp.zeros_like(v)
        v2 = jnp.concatenate([jnp.where(head0, v, zero_v), jnp.where(head0, zero_v, v)], axis=0)
        q_rows = [qm[hh] for _, qm, _ in consumers for hh in range(2)]
        z = jnp.dot(jnp.concatenate(q_rows, axis=0), kt, preferred_element_type=F32)
        neg_abs = _from_bits(_bits(z) | jnp.uint32(0x80000000))
        nl = jnp.maximum(z, 0.0) + jnp.log(1.0 + jnp.exp2(neg_abs)) * LOG2E
        segments = [slice(n * t, (n + 1) * t) for n in range(len(q_rows))]
        masked = [m for _, _, m in consumers for _ in range(2)]
        nl = jnp.concatenate([jnp.where(strict, nl[seg], 0.0) if m else nl[seg]
                              for seg, m in zip(segments, masked)], axis=0)
        csum = jnp.dot(nl.astype(BF16), lower, preferred_element_type=F32)
        probs = []
        for n, (u, _, m) in enumerate(consumers):
            heads = []
            for hh in range(2):
                seg = segments[2 * n + hh]
                carry = carry_sc[u, hh]
                a = jnp.exp2((z[seg] - _tile_lanes(carry, t // LANES)) - csum[seg])
                if m:
                    a = jnp.where(strict, a, 0.0)
                heads.append(a.astype(BF16))
                carry_sc[u, hh] = carry + csum[seg][:, 0:1]
            probs.append(jnp.concatenate(heads, axis=1))
        pv = jnp.dot(jnp.concatenate(probs, axis=0), v2, preferred_element_type=F32)
        for n, (u, _, _) in enumerate(consumers):
            acc_sc[u] += pv[n * t:(n + 1) * t]

    carry_sc[...] = jnp.zeros(carry_sc.shape, F32)
    acc_sc[...] = jnp.zeros(acc_sc.shape, F32)
    qs = [load_q(u) for u in range(n_sub)]

    for u in range(n_sub):
        qsq = qs[u][0].astype(F32) * qs[u][0].astype(F32)
        qn0 = jnp.sum(jnp.where(head0, qsq, 0.0), axis=1, keepdims=True)
        qn1 = jnp.sum(jnp.where(head0, 0.0, qsq), axis=1, keepdims=True)
        qnorm = jnp.max(jnp.maximum(qn0, qn1), axis=0, keepdims=True)
        thresh = jnp.sqrt(qnorm * knorm_sc[0:1, 0:1]) * (1.0 + 2.0 ** -10) + EXP2_UNDERFLOW
        thresh_sm[u] = jnp.max(thresh)

    def near_passes(has_previous_block):
        for u in range(n_sub - 1, -1, -1):
            consumers = [(u, qs[u][1], True)]
            if u + 1 < n_sub:
                consumers.append((u + 1, qs[u + 1][1], False))
            block_pass(i * n_sub + u, consumers)
        if has_previous_block:
            block_pass(i * n_sub - 1, [(0, qs[0][1], False)])

    @pl.when(i == 0)
    def _():
        near_passes(False)

    @pl.when(i >= 1)
    def _():
        near_passes(True)

    def carry_min(u):
        return jnp.min(jnp.minimum(carry_sc[u, 0], carry_sc[u, 1]))

    for u in range(n_sub):
        done_sm[u] = (carry_min(u) >= thresh_sm[u]).astype(jnp.int32)

    def remaining_blocks(u, c):
        _, qm = load_q(u)

        def far_cond(state):
            jb, all_zero = state
            return jnp.logical_and(jb >= 0, all_zero == 0)

        def far_body(state):
            jb, _ = state
            block_pass(jb, [(u, qm, False)])
            return jb - 1, (carry_min(u) >= thresh_sm[u]).astype(jnp.int32)

        lax.while_loop(far_cond, far_body, (i * n_sub + u - 2, done_sm[u]))
        return c

    lax.fori_loop(0, n_sub, remaining_blocks, 0)

    for u in range(n_sub):
        o = acc_sc[u]
        sq = o * o
        ss0 = jnp.sum(jnp.where(head0, sq, 0.0), axis=1, keepdims=True)
        ss1 = jnp.sum(jnp.where(head0, 0.0, sq), axis=1, keepdims=True)
        ms = jnp.where(head0, ss0, ss1) * (1.0 / HEAD_DIM)
        o_ref[u * t:(u + 1) * t, :] = (o * lax.rsqrt(ms + RMS_EPS) * g_ref[...]).astype(o_ref.dtype)


def _sb_attention(qs, kst, vs, g2):
    s = qs.shape[0]
    tq, t = SB_Q_BLOCK, SB_K_BLOCK
    return pl.pallas_call(
        _sb_kernel,
        grid=(N_SB_HEADS // 2, s // tq),
        in_specs=[
            pl.BlockSpec((tq, LANES), lambda h, i: (i, h)),
            pl.BlockSpec((LANES, s), lambda h, i: (h, 0)),
            pl.BlockSpec((s, LANES), lambda h, i: (0, h)),
            pl.BlockSpec(g2.shape, lambda h, i: (0, 0)),
        ],
        out_specs=pl.BlockSpec((tq, LANES), lambda h, i: (i, h)),
        out_shape=jax.ShapeDtypeStruct((s, SB_WIDTH), BF16),
        scratch_shapes=[pltpu.VMEM((tq // t, 2, t, LANES), F32), pltpu.VMEM((tq // t, t, LANES), F32),
                        pltpu.VMEM((8, LANES), F32),
                        pltpu.SMEM((tq // t,), F32), pltpu.SMEM((tq // t,), jnp.int32)],
        compiler_params=pltpu.CompilerParams(
            dimension_semantics=("arbitrary", "arbitrary"), vmem_limit_bytes=VMEM_LIMIT),
        name="sb_attn",
    )(qs, kst, vs, g2)


def _out_kernel(d_ref, s_ref, wd_ref, ws_ref, x_ref, gpost_ref, gpre_ref, x1_ref, h2_ref):
    mix = (jnp.dot(d_ref[...], wd_ref[...], preferred_element_type=F32)
           + jnp.dot(s_ref[...], ws_ref[...], preferred_element_type=F32))
    x1 = x_ref[...] + _rms(mix, gpost_ref[...])
    x1_ref[...] = x1
    h2_ref[...] = _rms(x1, gpre_ref[...]).astype(h2_ref.dtype)


def _out_proj(diff_o, sb_o, w_od, w_os, x2, g_post, g_pre):
    s = x2.shape[0]
    rows = OUT_ROWS
    row_spec = lambda w: pl.BlockSpec((rows, w), lambda i: (i, 0))
    full = lambda a: pl.BlockSpec(a.shape, lambda i: (0, 0))
    return pl.pallas_call(
        _out_kernel,
        grid=(s // rows,),
        in_specs=[row_spec(512), row_spec(512), full(w_od), full(w_os), row_spec(D_MODEL),
                  full(g_post), full(g_pre)],
        out_specs=[row_spec(D_MODEL), row_spec(D_MODEL)],
        out_shape=[jax.ShapeDtypeStruct((s, D_MODEL), F32), jax.ShapeDtypeStruct((s, D_MODEL), BF16)],
        compiler_params=pltpu.CompilerParams(
            dimension_semantics=("arbitrary",), vmem_limit_bytes=VMEM_LIMIT),
        name="out_proj",
    )(diff_o, sb_o, w_od, w_os, x2, g_post, g_pre)


def _gelu_tanh(x):
    return 0.5 * x * (1.0 + jnp.tanh(math.sqrt(2.0 / math.pi) * (x + 0.044715 * (x * x * x))))


def _ffn_kernel(h_ref, halo_ref, wup_ref, cw_ref, cb_ref, wd_ref, x1_ref, g_ref, o_ref, u_sc):
    i = pl.program_id(0)
    rows, chunk = FFN_ROWS, FFN_CHUNK
    h = h_ref[...]
    halo = jnp.where(i > 0, halo_ref[...], jnp.zeros_like(halo_ref[...]))

    def conv(col, slot):
        cols = slice(col, col + chunk)
        u_sc[slot, 0:HALO_ROWS, :] = jnp.dot(halo, wup_ref[:, cols], preferred_element_type=F32)
        u_sc[slot, HALO_ROWS:, :] = jnp.dot(h, wup_ref[:, cols], preferred_element_type=F32)
        y = cb_ref[:, cols]
        for k in range(CONV_WIDTH):
            off = HALO_ROWS - (CONV_WIDTH - 1) + k
            y = y + u_sc[slot, off:off + rows, :] * cw_ref[k:k + 1, cols]
        return y

    y = None
    for c in range(D_FF // chunk):
        gate = conv(c * chunk, 2 * c)
        val = conv(D_FF + c * chunk, 2 * c + 1)
        act = (_gelu_tanh(gate) * val).astype(BF16)
        part = jnp.dot(act, wd_ref[c * chunk:(c + 1) * chunk, :], preferred_element_type=F32)
        y = part if y is None else y + part
    o_ref[...] = x1_ref[...] + _rms(y, g_ref[...])


def _ffn(h2, w_up, conv_w, conv_b, w_down, x1, g_post):
    s = h2.shape[0]
    rows = FFN_ROWS
    halo_blocks = rows // HALO_ROWS
    row_spec = pl.BlockSpec((rows, D_MODEL), lambda i: (i, 0))
    halo_spec = pl.BlockSpec((HALO_ROWS, D_MODEL), lambda i: (jnp.maximum(i * halo_blocks - 1, 0), 0))
    resident = lambda a: pl.BlockSpec(a.shape, lambda i: (0, 0), pipeline_mode=pl.Buffered(1))
    return pl.pallas_call(
        _ffn_kernel,
        grid=(s // rows,),
        in_specs=[row_spec, halo_spec, resident(w_up), resident(conv_w), resident(conv_b),
                  resident(w_down), row_spec, resident(g_post)],
        out_specs=row_spec,
        out_shape=jax.ShapeDtypeStruct((s, D_MODEL), F32),
        scratch_shapes=[pltpu.VMEM((2 * (D_FF // FFN_CHUNK), HALO_ROWS + rows, FFN_CHUNK), F32)],
        compiler_params=pltpu.CompilerParams(
            dimension_semantics=("arbitrary",), vmem_limit_bytes=FFN_VMEM_LIMIT),
        name="ffn",
    )(h2, h2, w_up, conv_w, conv_b, w_down, x1, g_post)


def kernel(x, attn_pre_norm, w_qkv, lambda_q1, lambda_k1, lambda_q2, lambda_k2, diff_subln, sb_norm,
           rel_bias, w_o, attn_post_norm, ffn_pre_norm, w_up, conv_w, conv_b, w_down, ffn_post_norm):
    b, s, _ = x.shape
    assert b == 1 and attn_pre_norm.shape[0] == 1, "single sequence, single layer"
    assert s % max(PROJ_ROWS, 2 * DIFF_BLOCK, SB_Q_BLOCK, OUT_ROWS, FFN_ROWS) == 0
    x2 = x[0]
    row = lambda a: a.reshape(1, -1).astype(F32)

    qd, vd, qs, vs, kdt, kst = _proj(x2, row(attn_pre_norm), w_qkv[0].astype(BF16))

    bias = _bias_tiles(rel_bias.astype(F32))
    lam_params = jnp.concatenate([lambda_q1, lambda_k1, lambda_q2, lambda_k2], axis=0).astype(F32)
    diff_o = _diff_attention(lam_params, qd, kdt, vd, bias, row(diff_subln))
    sb_o = _sb_attention(qs, kst, vs, row(jnp.concatenate([sb_norm[0], sb_norm[0]])))

    wo = w_o[0].astype(BF16)
    x1, h2 = _out_proj(diff_o, sb_o, wo[:DIFF_WIDTH], wo[DIFF_WIDTH:], x2,
                       row(attn_post_norm), row(ffn_pre_norm))
    out = _ffn(h2, w_up[0].astype(BF16), conv_w[0].astype(F32), conv_b.astype(F32),
               w_down[0].astype(BF16), x1, row(ffn_post_norm))
    return out[None]
```

```python
import functools
import math

import jax
import jax.numpy as jnp
from jax import lax
from jax.experimental import pallas as pl
from jax.experimental.pallas import tpu as pltpu

F32 = jnp.float32
BF16 = jnp.bfloat16

D_MODEL = 1024
HEAD_DIM = 64
N_DIFF_HEADS = 4
N_SB_HEADS = 8
DIFF_WIDTH = N_DIFF_HEADS * 2 * HEAD_DIM
SB_WIDTH = N_SB_HEADS * HEAD_DIM
QKV_GROUP = DIFF_WIDTH
assert SB_WIDTH == QKV_GROUP
D_FF = 2816
CONV_WIDTH = 3
N_BUCKETS = 32
MAX_DISTANCE = 128
RMS_EPS = 1e-6
LAMBDA_INIT = 0.8 - 0.6 * math.exp(-0.3 * 0)

LOG2E = 1.4426950408889634
Q_SCALE = HEAD_DIM ** -0.5 * LOG2E
MASK_VALUE = -1e30
EXP2_UNDERFLOW = 160.0
NORM_BOUND_SLACK = 1.0 + 2.0 ** -10
SOFTMAX_MIN_DENOMINATOR = 2.0 ** -60

LANES = 128
PROJ_ROWS = 512
DIFF_BLOCK = 512
SB_Q_BLOCK = 1024
SB_K_BLOCK = 256
OUT_ROWS = 512
FFN_ROWS = 512
FFN_CHUNK = D_FF // 2
HALO_ROWS = 8
VMEM_LIMIT = 48 * 1024 * 1024
DIFF_VMEM_LIMIT = 58 * 1024 * 1024
FFN_VMEM_LIMIT = 58 * 1024 * 1024


def _bits(x):
    return lax.bitcast_convert_type(x, jnp.uint32)


def _from_bits(u):
    return lax.bitcast_convert_type(u, F32)


def _tile_lanes(x, k):
    return jnp.concatenate([x] * k, axis=1) if k > 1 else x


def _rms(x, g):
    return x * lax.rsqrt(jnp.mean(x * x, axis=-1, keepdims=True) + RMS_EPS) * g


def _proj_kernel(x_ref, g_ref, w_ref, qd_ref, vd_ref, qs_ref, vs_ref, kdt_ref, kst_ref):
    hb = _rms(x_ref[...], g_ref[...]).astype(BF16)
    proj = lambda c: jnp.dot(hb, w_ref[:, c * QKV_GROUP:(c + 1) * QKV_GROUP], preferred_element_type=F32)
    qd_ref[...] = (proj(0) * Q_SCALE).astype(BF16)
    vd_ref[...] = proj(2).astype(BF16)
    qs_ref[...] = (proj(3) * Q_SCALE).astype(BF16)
    vs_ref[...] = proj(5).astype(BF16)
    proj_t = lambda c: lax.dot_general(w_ref[:, c * QKV_GROUP:(c + 1) * QKV_GROUP], hb, (((0,), (1,)), ((), ())),
                                       preferred_element_type=F32)
    kdt_ref[...] = proj_t(1).astype(BF16)
    kst_ref[...] = proj_t(4).astype(BF16)


def _proj(x2, g, w):
    s = x2.shape[0]
    rows = PROJ_ROWS
    row_spec = lambda w: pl.BlockSpec((rows, w), lambda i: (i, 0))
    full = lambda a: pl.BlockSpec(a.shape, lambda i: (0, 0))
    out_rows = jax.ShapeDtypeStruct((s, QKV_GROUP), BF16)
    out_t = jax.ShapeDtypeStruct((QKV_GROUP, s), BF16)
    return pl.pallas_call(
        _proj_kernel,
        grid=(s // rows,),
        in_specs=[row_spec(D_MODEL), full(g), full(w)],
        out_specs=[row_spec(QKV_GROUP)] * 4 + [pl.BlockSpec((QKV_GROUP, rows), lambda i: (0, i))] * 2,
        out_shape=[out_rows] * 4 + [out_t] * 2,
        compiler_params=pltpu.CompilerParams(
            dimension_semantics=("arbitrary",), vmem_limit_bytes=VMEM_LIMIT),
        name="proj",
    )(x2, g, w)


def _bias_kernel(tab_ref, o_ref):
    h = pl.program_id(0)
    w = pl.program_id(1)
    t, sub = DIFF_BLOCK, LANES

    def sub_tile(offset):
        rel = (lax.broadcasted_iota(jnp.int32, (sub, sub), 0)
               - lax.broadcasted_iota(jnp.int32, (sub, sub), 1) + offset)
        n = jnp.maximum(rel, 0)
        max_exact = N_BUCKETS // 2
        nf = jnp.maximum(n, 1).astype(F32)
        large = max_exact + (jnp.log(nf / max_exact) / math.log(MAX_DISTANCE / max_exact)
                             * (N_BUCKETS - max_exact)).astype(jnp.int32)
        large = jnp.minimum(large, N_BUCKETS - 1)
        bucket = jnp.where(n < max_exact, n, large)
        b = jnp.zeros((sub, sub), F32)
        for k in range(N_BUCKETS):
            b = jnp.where(bucket == k, tab_ref[k, h], b)
        b = (b - tab_ref[N_BUCKETS - 1, h]) * LOG2E
        return jnp.where(rel >= 0, b, MASK_VALUE)

    def tile(first_offset):
        for br in range(t // sub):
            for bc in range(t // sub):
                offset = first_offset + (br - bc) * sub
                if offset + sub <= 0:
                    val = jnp.full((sub, sub), MASK_VALUE, F32)
                elif offset - sub >= MAX_DISTANCE:
                    val = jnp.zeros((sub, sub), F32)
                else:
                    val = sub_tile(offset)
                o_ref[0, 0, br * sub:(br + 1) * sub, bc * sub:(bc + 1) * sub] = val

    @pl.when(w == 0)
    def _():
        tile(0)

    @pl.when(w == 1)
    def _():
        tile(t)


def _bias_tiles(rel_bias):
    t = DIFF_BLOCK
    return pl.pallas_call(
        _bias_kernel,
        grid=(N_DIFF_HEADS, 2),
        in_specs=[pl.BlockSpec(memory_space=pltpu.SMEM)],
        out_specs=pl.BlockSpec((1, 1, t, t), lambda h, w: (h, w, 0, 0)),
        out_shape=jax.ShapeDtypeStruct((N_DIFF_HEADS, 2, t, t), F32),
        compiler_params=pltpu.CompilerParams(
            dimension_semantics=("arbitrary", "arbitrary"), vmem_limit_bytes=VMEM_LIMIT),
        name="bias",
    )(rel_bias)


def _diff_kernel(lam_ref, q_ref, kt_ref, v_ref, bias_ref, g_ref, o_ref,
                 acc_sc, stat_sc, s_sc, m_sc, l_sc, slow_acc_sc):
    i = pl.program_id(1)
    t = DIFF_BLOCK
    tq = 2 * t
    half_a, half_b = slice(0, t), slice(t, tq)
    lane = lax.broadcasted_iota(jnp.int32, (tq, LANES), 1)
    q = q_ref[...]
    n_pairs = jnp.maximum(i - 1, 0)

    @pl.when(i == 0)
    def _():
        def norm_body(c, ms):
            k = kt_ref[:, pl.ds(pl.multiple_of(c * t, t), t)].astype(F32)
            sq = k * k
            return (jnp.maximum(ms[0], jnp.sum(sq[:HEAD_DIM], axis=0, keepdims=True)),
                    jnp.maximum(ms[1], jnp.sum(sq[HEAD_DIM:], axis=0, keepdims=True)))

        zeros = jnp.zeros((1, t), F32)
        ms = lax.fori_loop(0, kt_ref.shape[1] // t, norm_body, (zeros, zeros))
        for mp in range(2):
            stat_sc[mp] = jnp.broadcast_to(jnp.max(ms[mp], axis=1, keepdims=True), stat_sc.shape[1:])
        bias_max = jnp.max(jnp.max(bias_ref[0, 0], axis=1, keepdims=True), axis=0, keepdims=True)
        stat_sc[2] = jnp.broadcast_to(bias_max, stat_sc.shape[1:])

    def write_out(rows, o0, o1):
        lp = lam_ref[...]
        lam = (jnp.exp(jnp.sum(lp[0:1] * lp[1:2], axis=1, keepdims=True))
               - jnp.exp(jnp.sum(lp[2:3] * lp[3:4], axis=1, keepdims=True)) + LAMBDA_INIT)
        o_ref[rows, :] = (_rms(o0 - lam * o1, g_ref[...]) * (1.0 - LAMBDA_INIT)).astype(o_ref.dtype)

    zero_q = jnp.zeros_like(q)
    qm = (jnp.where(lane < HEAD_DIM, q, zero_q), jnp.where(lane >= HEAD_DIM, q, zero_q))
    qsq = q.astype(F32) * q.astype(F32)
    bound = []
    for mp in range(2):
        own = (lane < HEAD_DIM) if mp == 0 else (lane >= HEAD_DIM)
        qn = jnp.sum(jnp.where(own, qsq, 0.0), axis=1, keepdims=True)
        b = jnp.sqrt(qn * stat_sc[mp][0:1, 0:1]) * NORM_BOUND_SLACK + stat_sc[2][0:1, 0:1]
        bound.append(jnp.broadcast_to(b, (tq, LANES)))

    acc_sc[...] = jnp.zeros(acc_sc.shape, F32)

    def scores(rows, start, width):
        kt = kt_ref[:, pl.ds(start, width)]
        return [jnp.dot(qm[mp][rows], kt, preferred_element_type=F32)
                - _tile_lanes(bound[mp][rows], width // LANES) for mp in range(2)]

    def accumulate(rows, s, start, width):
        v_aug = jnp.concatenate([v_ref[pl.ds(start, width), :], jnp.ones((width, LANES), BF16)], axis=1)
        for mp in range(2):
            acc_sc[mp, rows, :] += jnp.dot(jnp.exp2(s[mp]).astype(BF16), v_aug, preferred_element_type=F32)

    every = slice(0, tq)
    pair = 2 * t

    @pl.when(n_pairs > 0)
    def _():
        first = scores(every, 0, pair)
        for mp in range(2):
            s_sc[mp] = first[mp]

        def far_body(r, carry):
            cur = [s_sc[mp] for mp in range(2)]
            nxt = scores(every, pl.multiple_of((r + 1) * pair, pair), pair)
            accumulate(every, cur, pl.multiple_of(r * pair, pair), pair)
            for mp in range(2):
                s_sc[mp] = nxt[mp]
            return carry

        lax.fori_loop(0, n_pairs - 1, far_body, 0)
        accumulate(every, [s_sc[mp] for mp in range(2)], pl.multiple_of((n_pairs - 1) * pair, pair), pair)

    bias_diag, bias_prev = bias_ref[0, 0], bias_ref[0, 1]

    def last_block_of_b(block):
        start = pl.multiple_of(block * t, t)
        s = [x + bias_diag for x in scores(half_b, start, t)]
        accumulate(half_b, s, start, t)

    @pl.when(i == 0)
    def _():
        s = [x + jnp.concatenate([bias_diag, bias_prev], axis=0) for x in scores(every, 0, t)]
        accumulate(every, s, 0, t)
        last_block_of_b(1)

    @pl.when(i >= 1)
    def _():
        start = pl.multiple_of((2 * i - 2) * t, t)
        s = scores(every, start, 3 * t)
        for mp in range(2):
            x = s[mp]
            s[mp] = jnp.concatenate([
                x[:, :t],
                jnp.concatenate([x[half_a, t:2 * t] + bias_prev, x[half_b, t:2 * t]], axis=0),
                jnp.concatenate([x[half_a, 2 * t:] + bias_diag, x[half_b, 2 * t:] + bias_prev], axis=0),
            ], axis=1)
        accumulate(every, s, start, 3 * t)
        last_block_of_b(2 * i + 1)

    l0, l1 = acc_sc[0][:, LANES:], acc_sc[1][:, LANES:]
    write_out(every, acc_sc[0][:, :LANES] / l0, acc_sc[1][:, :LANES] / l1)

    @pl.when(jnp.logical_not(jnp.min(jnp.minimum(l0, l1)) >= SOFTMAX_MIN_DENOMINATOR))
    def _():
        def online_softmax_half(rows, diag_block):
            m_sc[...] = jnp.full(m_sc.shape, MASK_VALUE, F32)
            l_sc[...] = jnp.zeros(l_sc.shape, F32)
            slow_acc_sc[...] = jnp.zeros(slow_acc_sc.shape, F32)

            def step(j, bias):
                start = pl.multiple_of(j * t, t)
                kt = kt_ref[:, pl.ds(start, t)]
                v = v_ref[pl.ds(start, t), :]
                for mp in range(2):
                    s = jnp.dot(qm[mp][rows], kt, preferred_element_type=F32)
                    if bias is not None:
                        s = s + bias
                    m_prev = m_sc[mp]
                    m_new = jnp.maximum(m_prev, jnp.max(s, axis=1, keepdims=True))
                    alpha = jnp.exp2(m_prev - m_new)
                    p = jnp.exp2(s - _tile_lanes(m_new, t // LANES))
                    l_sc[mp] = alpha * l_sc[mp] + jnp.sum(p, axis=1, keepdims=True)
                    slow_acc_sc[mp] = (alpha * slow_acc_sc[mp]
                                       + jnp.dot(p.astype(BF16), v, preferred_element_type=F32))
                    m_sc[mp] = m_new

            def slow_body(j, carry):
                step(j, None)
                return carry

            lax.fori_loop(0, jnp.maximum(diag_block - 1, 0), slow_body, 0)

            @pl.when(diag_block >= 1)
            def _():
                step(diag_block - 1, bias_prev)

            step(diag_block, bias_diag)
            write_out(rows, slow_acc_sc[0] / l_sc[0], slow_acc_sc[1] / l_sc[1])

        online_softmax_half(half_a, 2 * i)
        online_softmax_half(half_b, 2 * i + 1)


def _diff_attention(lam_params, qd, kdt, vd, bias, g):
    s = qd.shape[0]
    t = DIFF_BLOCK
    tq = 2 * t
    return pl.pallas_call(
        _diff_kernel,
        grid=(N_DIFF_HEADS, s // tq),
        in_specs=[
            pl.BlockSpec(lam_params.shape, lambda h, i: (0, 0)),
            pl.BlockSpec((tq, LANES), lambda h, i: (i, h)),
            pl.BlockSpec((LANES, s), lambda h, i: (h, 0)),
            pl.BlockSpec((s, LANES), lambda h, i: (0, h)),
            pl.BlockSpec((1, 2, t, t), lambda h, i: (h, 0, 0, 0)),
            pl.BlockSpec(g.shape, lambda h, i: (0, 0)),
        ],
        out_specs=pl.BlockSpec((tq, LANES), lambda h, i: (i, h)),
        out_shape=jax.ShapeDtypeStruct((s, DIFF_WIDTH), BF16),
        scratch_shapes=[pltpu.VMEM((2, tq, 2 * LANES), F32), pltpu.VMEM((3, 8, LANES), F32),
                        pltpu.VMEM((2, tq, 2 * t), F32)]
        + [pltpu.VMEM((2, t, LANES), F32)] * 3,
        compiler_params=pltpu.CompilerParams(
            dimension_semantics=("arbitrary", "arbitrary"), vmem_limit_bytes=DIFF_VMEM_LIMIT),
        name="diff_attn",
    )(lam_params, qd, kdt, vd, bias, g)


def _sb_kernel(q_ref, kt_ref, v_ref, g_ref, o_ref, carry_sc, acc_sc, knorm_sc, thresh_sm, done_sm):
    i = pl.program_id(1)
    t = SB_K_BLOCK
    n_sub = SB_Q_BLOCK // t
    lane = lax.broadcasted_iota(jnp.int32, (t, LANES), 1)
    head0 = lane < HEAD_DIM
    lower = (lax.broadcasted_iota(jnp.int32, (t, t), 0)
             >= lax.broadcasted_iota(jnp.int32, (t, t), 1)).astype(BF16)
    strict = lax.broadcasted_iota(jnp.int32, (t, t), 0) > lax.broadcasted_iota(jnp.int32, (t, t), 1)

    @pl.when(i == 0)
    def _():
        def norm_body(c, m):
            k = kt_ref[:, pl.ds(pl.multiple_of(c * t, t), t)].astype(F32)
            sq = k * k
            return jnp.maximum(m, jnp.maximum(jnp.sum(sq[:HEAD_DIM], axis=0, keepdims=True),
                                              jnp.sum(sq[HEAD_DIM:], axis=0, keepdims=True)))

        m = lax.fori_loop(0, kt_ref.shape[1] // t, norm_body, jnp.zeros((1, t), F32))
        knorm_sc[...] = jnp.broadcast_to(jnp.max(m, axis=1, keepdims=True), knorm_sc.shape)

    def load_q(u):
        q = q_ref[pl.ds(pl.multiple_of(u * t, t), t), :]
        zero_q = jnp.zeros_like(q)
        return q, (jnp.where(head0, q, zero_q), jnp.where(head0, zero_q, q))

    def block_pass(jb, consumers):
        start = pl.multiple_of(jb * t, t)
        kt = kt_ref[:, pl.ds(start, t)]
        v = v_ref[pl.ds(start, t), :]
        zero_v = jnp.zeros_like(v)
        v2 = jnp.concatenate([jnp.where(head0, v, zero_v), jnp.where(head0, zero_v, v)], axis=0)
        q_rows = [qm[hh] for _, qm, _ in consumers for hh in range(2)]
        z = jnp.dot(jnp.concatenate(q_rows, axis=0), kt, preferred_element_type=F32)
        neg_abs = _from_bits(_bits(z) | jnp.uint32(0x80000000))
        nl = jnp.maximum(z, 0.0) + jnp.log(1.0 + jnp.exp2(neg_abs)) * LOG2E
        segments = [slice(n * t, (n + 1) * t) for n in range(len(q_rows))]
        masked = [m for _, _, m in consumers for _ in range(2)]
        nl = jnp.concatenate([jnp.where(strict, nl[seg], 0.0) if m else nl[seg]
                              for seg, m in zip(segments, masked)], axis=0)
        csum = jnp.dot(nl.astype(BF16), lower, preferred_element_type=F32)
        probs = []
        for n, (u, _, m) in enumerate(consumers):
            heads = []
            for hh in range(2):
                seg = segments[2 * n + hh]
                carry = carry_sc[u, hh]
                a = jnp.exp2((z[seg] - _tile_lanes(carry, t // LANES)) - csum[seg])
                if m:
                    a = jnp.where(strict, a, 0.0)
                heads.append(a.astype(BF16))
                carry_sc[u, hh] = carry + csum[seg][:, 0:1]
            probs.append(jnp.concatenate(heads, axis=1))
        pv = jnp.dot(jnp.concatenate(probs, axis=0), v2, preferred_element_type=F32)
        for n, (u, _, _) in enumerate(consumers):
            acc_sc[u] += pv[n * t:(n + 1) * t]

    carry_sc[...] = jnp.zeros(carry_sc.shape, F32)
    acc_sc[...] = jnp.zeros(acc_sc.shape, F32)
    qs = [load_q(u) for u in range(n_sub)]

    for u in range(n_sub):
        qsq = qs[u][0].astype(F32) * qs[u][0].astype(F32)
        qn0 = jnp.sum(jnp.where(head0, qsq, 0.0), axis=1, keepdims=True)
        qn1 = jnp.sum(jnp.where(head0, 0.0, qsq), axis=1, keepdims=True)
        qnorm = jnp.max(jnp.maximum(qn0, qn1), axis=0, keepdims=True)
        thresh = jnp.sqrt(qnorm * knorm_sc[0:1, 0:1]) * NORM_BOUND_SLACK + EXP2_UNDERFLOW
        thresh_sm[u] = jnp.max(thresh)

    def near_passes(has_previous_block):
        for u in range(n_sub - 1, -1, -1):
            consumers = [(u, qs[u][1], True)]
            if u + 1 < n_sub:
                consumers.append((u + 1, qs[u + 1][1], False))
            block_pass(i * n_sub + u, consumers)
        if has_previous_block:
            block_pass(i * n_sub - 1, [(0, qs[0][1], False)])

    @pl.when(i == 0)
    def _():
        near_passes(False)

    @pl.when(i >= 1)
    def _():
        near_passes(True)

    def carry_min(u):
        return jnp.min(jnp.minimum(carry_sc[u, 0], carry_sc[u, 1]))

    for u in range(n_sub):
        done_sm[u] = (carry_min(u) >= thresh_sm[u]).astype(jnp.int32)

    def remaining_blocks(u, c):
        _, qm = load_q(u)

        def far_cond(state):
            jb, all_zero = state
            return jnp.logical_and(jb >= 0, all_zero == 0)

        def far_body(state):
            jb, _ = state
            block_pass(jb, [(u, qm, False)])
            return jb - 1, (carry_min(u) >= thresh_sm[u]).astype(jnp.int32)

        lax.while_loop(far_cond, far_body, (i * n_sub + u - 2, done_sm[u]))
        return c

    lax.fori_loop(0, n_sub, remaining_blocks, 0)

    for u in range(n_sub):
        o = acc_sc[u]
        sq = o * o
        ss0 = jnp.sum(jnp.where(head0, sq, 0.0), axis=1, keepdims=True)
        ss1 = jnp.sum(jnp.where(head0, 0.0, sq), axis=1, keepdims=True)
        ms = jnp.where(head0, ss0, ss1) * (1.0 / HEAD_DIM)
        o_ref[u * t:(u + 1) * t, :] = (o * lax.rsqrt(ms + RMS_EPS) * g_ref[...]).astype(o_ref.dtype)


def _sb_attention(qs, kst, vs, g2):
    s = qs.shape[0]
    tq, t = SB_Q_BLOCK, SB_K_BLOCK
    return pl.pallas_call(
        _sb_kernel,
        grid=(N_SB_HEADS // 2, s // tq),
        in_specs=[
            pl.BlockSpec((tq, LANES), lambda h, i: (i, h)),
            pl.BlockSpec((LANES, s), lambda h, i: (h, 0)),
            pl.BlockSpec((s, LANES), lambda h, i: (0, h)),
            pl.BlockSpec(g2.shape, lambda h, i: (0, 0)),
        ],
        out_specs=pl.BlockSpec((tq, LANES), lambda h, i: (i, h)),
        out_shape=jax.ShapeDtypeStruct((s, SB_WIDTH), BF16),
        scratch_shapes=[pltpu.VMEM((tq // t, 2, t, LANES), F32), pltpu.VMEM((tq // t, t, LANES), F32),
                        pltpu.VMEM((8, LANES), F32),
                        pltpu.SMEM((tq // t,), F32), pltpu.SMEM((tq // t,), jnp.int32)],
        compiler_params=pltpu.CompilerParams(
            dimension_semantics=("arbitrary", "arbitrary"), vmem_limit_bytes=VMEM_LIMIT),
        name="sb_attn",
    )(qs, kst, vs, g2)


def _out_kernel(d_ref, s_ref, wd_ref, ws_ref, x_ref, gpost_ref, gpre_ref, x1_ref, h2_ref):
    mix = (jnp.dot(d_ref[...], wd_ref[...], preferred_element_type=F32)
           + jnp.dot(s_ref[...], ws_ref[...], preferred_element_type=F32))
    x1 = x_ref[...] + _rms(mix, gpost_ref[...])
    x1_ref[...] = x1
    h2_ref[...] = _rms(x1, gpre_ref[...]).astype(h2_ref.dtype)


def _out_proj(diff_o, sb_o, w_od, w_os, x2, g_post, g_pre):
    s = x2.shape[0]
    rows = OUT_ROWS
    row_spec = lambda w: pl.BlockSpec((rows, w), lambda i: (i, 0))
    full = lambda a: pl.BlockSpec(a.shape, lambda i: (0, 0))
    return pl.pallas_call(
        _out_kernel,
        grid=(s // rows,),
        in_specs=[row_spec(DIFF_WIDTH), row_spec(SB_WIDTH), full(w_od), full(w_os), row_spec(D_MODEL),
                  full(g_post), full(g_pre)],
        out_specs=[row_spec(D_MODEL), row_spec(D_MODEL)],
        out_shape=[jax.ShapeDtypeStruct((s, D_MODEL), F32), jax.ShapeDtypeStruct((s, D_MODEL), BF16)],
        compiler_params=pltpu.CompilerParams(
            dimension_semantics=("arbitrary",), vmem_limit_bytes=VMEM_LIMIT),
        name="out_proj",
    )(diff_o, sb_o, w_od, w_os, x2, g_post, g_pre)


def _gelu_tanh(x):
    return 0.5 * x * (1.0 + jnp.tanh(math.sqrt(2.0 / math.pi) * (x + 0.044715 * (x * x * x))))


def _ffn_kernel(h_ref, halo_ref, wup_ref, cw_ref, cb_ref, wd_ref, x1_ref, g_ref, o_ref, u_sc):
    i = pl.program_id(0)
    rows, chunk = FFN_ROWS, FFN_CHUNK
    h = h_ref[...]
    halo = jnp.where(i > 0, halo_ref[...], jnp.zeros_like(halo_ref[...]))

    def conv(col, slot):
        cols = slice(col, col + chunk)
        u_sc[slot, 0:HALO_ROWS, :] = jnp.dot(halo, wup_ref[:, cols], preferred_element_type=F32)
        u_sc[slot, HALO_ROWS:, :] = jnp.dot(h, wup_ref[:, cols], preferred_element_type=F32)
        y = cb_ref[:, cols]
        for k in range(CONV_WIDTH):
            off = HALO_ROWS - (CONV_WIDTH - 1) + k
            y = y + u_sc[slot, off:off + rows, :] * cw_ref[k:k + 1, cols]
        return y

    n_chunk = D_FF // chunk
    gates = [conv(c * chunk, 2 * c) for c in range(n_chunk)]
    vals = [conv(D_FF + c * chunk, 2 * c + 1) for c in range(n_chunk)]
    y = None
    for c in range(n_chunk):
        act = (_gelu_tanh(gates[c]) * vals[c]).astype(BF16)
        part = jnp.dot(act, wd_ref[c * chunk:(c + 1) * chunk, :], preferred_element_type=F32)
        y = part if y is None else y + part
    o_ref[...] = x1_ref[...] + _rms(y, g_ref[...])


def _ffn(h2, w_up, conv_w, conv_b, w_down, x1, g_post):
    s = h2.shape[0]
    rows = FFN_ROWS
    halo_blocks = rows // HALO_ROWS
    row_spec = pl.BlockSpec((rows, D_MODEL), lambda i: (i, 0))
    halo_spec = pl.BlockSpec((HALO_ROWS, D_MODEL), lambda i: (jnp.maximum(i * halo_blocks - 1, 0), 0))
    resident = lambda a: pl.BlockSpec(a.shape, lambda i: (0, 0), pipeline_mode=pl.Buffered(1))
    return pl.pallas_call(
        _ffn_kernel,
        grid=(s // rows,),
        in_specs=[row_spec, halo_spec, resident(w_up), resident(conv_w), resident(conv_b),
                  resident(w_down), row_spec, resident(g_post)],
        out_specs=row_spec,
        out_shape=jax.ShapeDtypeStruct((s, D_MODEL), F32),
        scratch_shapes=[pltpu.VMEM((2 * (D_FF // FFN_CHUNK), HALO_ROWS + rows, FFN_CHUNK), F32)],
        compiler_params=pltpu.CompilerParams(
            dimension_semantics=("arbitrary",), vmem_limit_bytes=FFN_VMEM_LIMIT),
        name="ffn",
    )(h2, h2, w_up, conv_w, conv_b, w_down, x1, g_post)


def kernel(x, attn_pre_norm, w_qkv, lambda_q1, lambda_k1, lambda_q2, lambda_k2, diff_subln, sb_norm,
           rel_bias, w_o, attn_post_norm, ffn_pre_norm, w_up, conv_w, conv_b, w_down, ffn_post_norm):
    b, s, _ = x.shape
    assert b == 1 and attn_pre_norm.shape[0] == 1, "single sequence, single layer"
    assert s % max(PROJ_ROWS, 2 * DIFF_BLOCK, SB_Q_BLOCK, OUT_ROWS, FFN_ROWS) == 0
    x2 = x[0]
    row = lambda a: a.reshape(1, -1).astype(F32)

    qd, vd, qs, vs, kdt, kst = _proj(x2, row(attn_pre_norm), w_qkv[0].astype(BF16))

    bias = _bias_tiles(rel_bias.astype(F32))
    lam_params = jnp.concatenate([lambda_q1, lambda_k1, lambda_q2, lambda_k2], axis=0).astype(F32)
    diff_o = _diff_attention(lam_params, qd, kdt, vd, bias, row(diff_subln))
    sb_o = _sb_attention(qs, kst, vs, row(jnp.concatenate([sb_norm[0], sb_norm[0]])))

    wo = w_o[0].astype(BF16)
    x1, h2 = _out_proj(diff_o, sb_o, wo[:DIFF_WIDTH], wo[DIFF_WIDTH:], x2,
                       row(attn_post_norm), row(ffn_pre_norm))
    out = _ffn(h2, w_up[0].astype(BF16), conv_w[0].astype(F32), conv_b.astype(F32),
               w_down[0].astype(BF16), x1, row(ffn_post_norm))
    return out[None]
```

```python
import functools
import math

import jax
import jax.numpy as jnp
from jax import lax
from jax.experimental import pallas as pl
from jax.experimental.pallas import tpu as pltpu

F32 = jnp.float32
BF16 = jnp.bfloat16

D_MODEL = 1024
HEAD_DIM = 64
N_DIFF_HEADS = 4
N_SB_HEADS = 8
DIFF_WIDTH = N_DIFF_HEADS * 2 * HEAD_DIM
SB_WIDTH = N_SB_HEADS * HEAD_DIM
QKV_GROUP = DIFF_WIDTH
assert SB_WIDTH == QKV_GROUP
D_FF = 2816
CONV_WIDTH = 3
N_BUCKETS = 32
MAX_DISTANCE = 128
RMS_EPS = 1e-6
LAMBDA_INIT = 0.8 - 0.6 * math.exp(-0.3 * 0)

LOG2E = 1.4426950408889634
Q_SCALE = HEAD_DIM ** -0.5 * LOG2E
MASK_VALUE = -1e30
EXP2_UNDERFLOW = 160.0
NORM_BOUND_SLACK = 1.0 + 2.0 ** -10
SOFTMAX_MIN_DENOMINATOR = 2.0 ** -60

LANES = 128
PROJ_ROWS = 1024
DIFF_BLOCK = 512
SB_Q_BLOCK = 1024
SB_K_BLOCK = 256
OUT_ROWS = 1024
FFN_ROWS = 512
FFN_CHUNK = D_FF // 2
HALO_ROWS = 8
VMEM_LIMIT = 48 * 1024 * 1024
DIFF_VMEM_LIMIT = 58 * 1024 * 1024
FFN_VMEM_LIMIT = 58 * 1024 * 1024


def _bits(x):
    return lax.bitcast_convert_type(x, jnp.uint32)


def _from_bits(u):
    return lax.bitcast_convert_type(u, F32)


def _tile_lanes(x, k):
    return jnp.concatenate([x] * k, axis=1) if k > 1 else x


def _rms(x, g):
    return x * lax.rsqrt(jnp.mean(x * x, axis=-1, keepdims=True) + RMS_EPS) * g


def _proj_kernel(x_ref, g_ref, w_ref, qd_ref, vd_ref, qs_ref, vs_ref, kdt_ref, kst_ref):
    hb = _rms(x_ref[...], g_ref[...]).astype(BF16)
    proj = lambda c: jnp.dot(hb, w_ref[:, c * QKV_GROUP:(c + 1) * QKV_GROUP], preferred_element_type=F32)
    qd_ref[...] = (proj(0) * Q_SCALE).astype(BF16)
    vd_ref[...] = proj(2).astype(BF16)
    qs_ref[...] = (proj(3) * Q_SCALE).astype(BF16)
    vs_ref[...] = proj(5).astype(BF16)
    proj_t = lambda c: lax.dot_general(w_ref[:, c * QKV_GROUP:(c + 1) * QKV_GROUP], hb, (((0,), (1,)), ((), ())),
                                       preferred_element_type=F32)
    kdt_ref[...] = proj_t(1).astype(BF16)
    kst_ref[...] = proj_t(4).astype(BF16)


def _proj(x2, g, w):
    s = x2.shape[0]
    rows = PROJ_ROWS
    row_spec = lambda w: pl.BlockSpec((rows, w), lambda i: (i, 0))
    full = lambda a: pl.BlockSpec(a.shape, lambda i: (0, 0))
    out_rows = jax.ShapeDtypeStruct((s, QKV_GROUP), BF16)
    out_t = jax.ShapeDtypeStruct((QKV_GROUP, s), BF16)
    return pl.pallas_call(
        _proj_kernel,
        grid=(s // rows,),
        in_specs=[row_spec(D_MODEL), full(g), full(w)],
        out_specs=[row_spec(QKV_GROUP)] * 4 + [pl.BlockSpec((QKV_GROUP, rows), lambda i: (0, i))] * 2,
        out_shape=[out_rows] * 4 + [out_t] * 2,
        compiler_params=pltpu.CompilerParams(
            dimension_semantics=("arbitrary",), vmem_limit_bytes=VMEM_LIMIT),
        name="proj",
    )(x2, g, w)


def _bias_kernel(tab_ref, o_ref):
    h = pl.program_id(0)
    w = pl.program_id(1)
    t, sub = DIFF_BLOCK, LANES

    def sub_tile(offset):
        rel = (lax.broadcasted_iota(jnp.int32, (sub, sub), 0)
               - lax.broadcasted_iota(jnp.int32, (sub, sub), 1) + offset)
        n = jnp.maximum(rel, 0)
        max_exact = N_BUCKETS // 2
        nf = jnp.maximum(n, 1).astype(F32)
        large = max_exact + (jnp.log(nf / max_exact) / math.log(MAX_DISTANCE / max_exact)
                             * (N_BUCKETS - max_exact)).astype(jnp.int32)
        large = jnp.minimum(large, N_BUCKETS - 1)
        bucket = jnp.where(n < max_exact, n, large)
        b = jnp.zeros((sub, sub), F32)
        for k in range(N_BUCKETS):
            b = jnp.where(bucket == k, tab_ref[k, h], b)
        b = (b - tab_ref[N_BUCKETS - 1, h]) * LOG2E
        return jnp.where(rel >= 0, b, MASK_VALUE)

    def tile(first_offset):
        for br in range(t // sub):
            for bc in range(t // sub):
                offset = first_offset + (br - bc) * sub
                if offset + sub <= 0:
                    val = jnp.full((sub, sub), MASK_VALUE, F32)
                elif offset - sub >= MAX_DISTANCE:
                    val = jnp.zeros((sub, sub), F32)
                else:
                    val = sub_tile(offset)
                o_ref[0, 0, br * sub:(br + 1) * sub, bc * sub:(bc + 1) * sub] = val

    @pl.when(w == 0)
    def _():
        tile(0)

    @pl.when(w == 1)
    def _():
        tile(t)


def _bias_tiles(rel_bias):
    t = DIFF_BLOCK
    return pl.pallas_call(
        _bias_kernel,
        grid=(N_DIFF_HEADS, 2),
        in_specs=[pl.BlockSpec(memory_space=pltpu.SMEM)],
        out_specs=pl.BlockSpec((1, 1, t, t), lambda h, w: (h, w, 0, 0)),
        out_shape=jax.ShapeDtypeStruct((N_DIFF_HEADS, 2, t, t), F32),
        compiler_params=pltpu.CompilerParams(
            dimension_semantics=("arbitrary", "arbitrary"), vmem_limit_bytes=VMEM_LIMIT),
        name="bias",
    )(rel_bias)


def _diff_kernel(lam_ref, q_ref, kt_ref, v_ref, bias_ref, g_ref, o_ref,
                 acc_sc, stat_sc, s_sc, m_sc, l_sc, slow_acc_sc):
    i = pl.program_id(1)
    t = DIFF_BLOCK
    tq = 2 * t
    half_a, half_b = slice(0, t), slice(t, tq)
    lane = lax.broadcasted_iota(jnp.int32, (tq, LANES), 1)
    q = q_ref[...]
    n_pairs = jnp.maximum(i - 1, 0)

    @pl.when(i == 0)
    def _():
        def norm_body(c, ms):
            k = kt_ref[:, pl.ds(pl.multiple_of(c * t, t), t)].astype(F32)
            sq = k * k
            return (jnp.maximum(ms[0], jnp.sum(sq[:HEAD_DIM], axis=0, keepdims=True)),
                    jnp.maximum(ms[1], jnp.sum(sq[HEAD_DIM:], axis=0, keepdims=True)))

        zeros = jnp.zeros((1, t), F32)
        ms = lax.fori_loop(0, kt_ref.shape[1] // t, norm_body, (zeros, zeros))
        for mp in range(2):
            stat_sc[mp] = jnp.broadcast_to(jnp.max(ms[mp], axis=1, keepdims=True), stat_sc.shape[1:])
        bias_max = jnp.max(jnp.max(bias_ref[0, 0], axis=1, keepdims=True), axis=0, keepdims=True)
        stat_sc[2] = jnp.broadcast_to(bias_max, stat_sc.shape[1:])

    def write_out(rows, o0, o1):
        lp = lam_ref[...]
        lam = (jnp.exp(jnp.sum(lp[0:1] * lp[1:2], axis=1, keepdims=True))
               - jnp.exp(jnp.sum(lp[2:3] * lp[3:4], axis=1, keepdims=True)) + LAMBDA_INIT)
        o_ref[rows, :] = (_rms(o0 - lam * o1, g_ref[...]) * (1.0 - LAMBDA_INIT)).astype(o_ref.dtype)

    zero_q = jnp.zeros_like(q)
    qm = (jnp.where(lane < HEAD_DIM, q, zero_q), jnp.where(lane >= HEAD_DIM, q, zero_q))
    qsq = q.astype(F32) * q.astype(F32)
    bound = []
    for mp in range(2):
        own = (lane < HEAD_DIM) if mp == 0 else (lane >= HEAD_DIM)
        qn = jnp.sum(jnp.where(own, qsq, 0.0), axis=1, keepdims=True)
        b = jnp.sqrt(qn * stat_sc[mp][0:1, 0:1]) * NORM_BOUND_SLACK + stat_sc[2][0:1, 0:1]
        bound.append(jnp.broadcast_to(b, (tq, LANES)))

    acc_sc[...] = jnp.zeros(acc_sc.shape, F32)

    def scores(rows, start, width):
        kt = kt_ref[:, pl.ds(start, width)]
        return [jnp.dot(qm[mp][rows], kt, preferred_element_type=F32)
                - _tile_lanes(bound[mp][rows], width // LANES) for mp in range(2)]

    def accumulate(rows, s, start, width):
        v_aug = jnp.concatenate([v_ref[pl.ds(start, width), :], jnp.ones((width, LANES), BF16)], axis=1)
        for mp in range(2):
            acc_sc[mp, rows, :] += jnp.dot(jnp.exp2(s[mp]).astype(BF16), v_aug, preferred_element_type=F32)

    every = slice(0, tq)
    pair = 2 * t

    bias_diag, bias_prev = bias_ref[0, 0], bias_ref[0, 1]

    def last_block_of_b(block):
        start = pl.multiple_of(block * t, t)
        s = [x + bias_diag for x in scores(half_b, start, t)]
        accumulate(half_b, s, start, t)

    def tail():
        start = pl.multiple_of((2 * i - 2) * t, t)
        s = scores(every, start, 3 * t)
        for mp in range(2):
            x = s[mp]
            s[mp] = jnp.concatenate([
                x[:, :t],
                jnp.concatenate([x[half_a, t:2 * t] + bias_prev, x[half_b, t:2 * t]], axis=0),
                jnp.concatenate([x[half_a, 2 * t:] + bias_diag, x[half_b, 2 * t:] + bias_prev], axis=0),
            ], axis=1)
        accumulate(every, s, start, 3 * t)
        last_block_of_b(2 * i + 1)

    @pl.when(i == 0)
    def _():
        s = [x + jnp.concatenate([bias_diag, bias_prev], axis=0) for x in scores(every, 0, t)]
        accumulate(every, s, 0, t)
        last_block_of_b(1)

    @pl.when(i == 1)
    def _():
        tail()

    @pl.when(i >= 2)
    def _():
        first = scores(every, 0, pair)
        for mp in range(2):
            s_sc[mp] = first[mp]

        def far_body(r, carry):
            cur = [s_sc[mp] for mp in range(2)]
            nxt = scores(every, pl.multiple_of((r + 1) * pair, pair), pair)
            accumulate(every, cur, pl.multiple_of(r * pair, pair), pair)
            for mp in range(2):
                s_sc[mp] = nxt[mp]
            return carry

        lax.fori_loop(0, n_pairs - 1, far_body, 0)
        accumulate(every, [s_sc[mp] for mp in range(2)], pl.multiple_of((n_pairs - 1) * pair, pair), pair)
        tail()

    l0, l1 = acc_sc[0][:, LANES:], acc_sc[1][:, LANES:]
    write_out(every, acc_sc[0][:, :LANES] / l0, acc_sc[1][:, :LANES] / l1)

    @pl.when(jnp.logical_not(jnp.min(jnp.minimum(l0, l1)) >= SOFTMAX_MIN_DENOMINATOR))
    def _():
        def online_softmax_half(rows, diag_block):
            m_sc[...] = jnp.full(m_sc.shape, MASK_VALUE, F32)
            l_sc[...] = jnp.zeros(l_sc.shape, F32)
            slow_acc_sc[...] = jnp.zeros(slow_acc_sc.shape, F32)

            def step(j, bias):
                start = pl.multiple_of(j * t, t)
                kt = kt_ref[:, pl.ds(start, t)]
                v = v_ref[pl.ds(start, t), :]
                for mp in range(2):
                    s = jnp.dot(qm[mp][rows], kt, preferred_element_type=F32)
                    if bias is not None:
                        s = s + bias
                    m_prev = m_sc[mp]
                    m_new = jnp.maximum(m_prev, jnp.max(s, axis=1, keepdims=True))
                    alpha = jnp.exp2(m_prev - m_new)
                    p = jnp.exp2(s - _tile_lanes(m_new, t // LANES))
                    l_sc[mp] = alpha * l_sc[mp] + jnp.sum(p, axis=1, keepdims=True)
                    slow_acc_sc[mp] = (alpha * slow_acc_sc[mp]
                                       + jnp.dot(p.astype(BF16), v, preferred_element_type=F32))
                    m_sc[mp] = m_new

            def slow_body(j, carry):
                step(j, None)
                return carry

            lax.fori_loop(0, jnp.maximum(diag_block - 1, 0), slow_body, 0)

            @pl.when(diag_block >= 1)
            def _():
                step(diag_block - 1, bias_prev)

            step(diag_block, bias_diag)
            write_out(rows, slow_acc_sc[0] / l_sc[0], slow_acc_sc[1] / l_sc[1])

        online_softmax_half(half_a, 2 * i)
        online_softmax_half(half_b, 2 * i + 1)


def _diff_attention(lam_params, qd, kdt, vd, bias, g):
    s = qd.shape[0]
    t = DIFF_BLOCK
    tq = 2 * t
    return pl.pallas_call(
        _diff_kernel,
        grid=(N_DIFF_HEADS, s // tq),
        in_specs=[
            pl.BlockSpec(lam_params.shape, lambda h, i: (0, 0)),
            pl.BlockSpec((tq, LANES), lambda h, i: (i, h)),
            pl.BlockSpec((LANES, s), lambda h, i: (h, 0)),
            pl.BlockSpec((s, LANES), lambda h, i: (0, h)),
            pl.BlockSpec((1, 2, t, t), lambda h, i: (h, 0, 0, 0)),
            pl.BlockSpec(g.shape, lambda h, i: (0, 0)),
        ],
        out_specs=pl.BlockSpec((tq, LANES), lambda h, i: (i, h)),
        out_shape=jax.ShapeDtypeStruct((s, DIFF_WIDTH), BF16),
        scratch_shapes=[pltpu.VMEM((2, tq, 2 * LANES), F32), pltpu.VMEM((3, 8, LANES), F32),
                        pltpu.VMEM((2, tq, 2 * t), F32)]
        + [pltpu.VMEM((2, t, LANES), F32)] * 3,
        compiler_params=pltpu.CompilerParams(
            dimension_semantics=("arbitrary", "arbitrary"), vmem_limit_bytes=DIFF_VMEM_LIMIT),
        name="diff_attn",
    )(lam_params, qd, kdt, vd, bias, g)


def _sb_kernel(q_ref, kt_ref, v_ref, g_ref, o_ref, carry_sc, acc_sc, knorm_sc, thresh_sm, done_sm):
    i = pl.program_id(1)
    t = SB_K_BLOCK
    n_sub = SB_Q_BLOCK // t
    lane = lax.broadcasted_iota(jnp.int32, (t, LANES), 1)
    head0 = lane < HEAD_DIM
    lower = (lax.broadcasted_iota(jnp.int32, (t, t), 0)
             >= lax.broadcasted_iota(jnp.int32, (t, t), 1)).astype(BF16)
    strict = lax.broadcasted_iota(jnp.int32, (t, t), 0) > lax.broadcasted_iota(jnp.int32, (t, t), 1)

    @pl.when(i == 0)
    def _():
        def norm_body(c, m):
            k = kt_ref[:, pl.ds(pl.multiple_of(c * t, t), t)].astype(F32)
            sq = k * k
            return jnp.maximum(m, jnp.maximum(jnp.sum(sq[:HEAD_DIM], axis=0, keepdims=True),
                                              jnp.sum(sq[HEAD_DIM:], axis=0, keepdims=True)))

        m = lax.fori_loop(0, kt_ref.shape[1] // t, norm_body, jnp.zeros((1, t), F32))
        knorm_sc[...] = jnp.broadcast_to(jnp.max(m, axis=1, keepdims=True), knorm_sc.shape)

    def load_q(u):
        q = q_ref[pl.ds(pl.multiple_of(u * t, t), t), :]
        zero_q = jnp.zeros_like(q)
        return q, (jnp.where(head0, q, zero_q), jnp.where(head0, zero_q, q))

    def block_pass(jb, consumers):
        start = pl.multiple_of(jb * t, t)
        kt = kt_ref[:, pl.ds(start, t)]
        v = v_ref[pl.ds(start, t), :]
        zero_v = jnp.zeros_like(v)
        v2 = jnp.concatenate([jnp.where(head0, v, zero_v), jnp.where(head0, zero_v, v)], axis=0)
        q_rows = [qm[hh] for _, qm, _ in consumers for hh in range(2)]
        z = jnp.dot(jnp.concatenate(q_rows, axis=0), kt, preferred_element_type=F32)
        neg_abs = _from_bits(_bits(z) | jnp.uint32(0x80000000))
        nl = jnp.maximum(z, 0.0) + jnp.log(1.0 + jnp.exp2(neg_abs)) * LOG2E
        segments = [slice(n * t, (n + 1) * t) for n in range(len(q_rows))]
        masked = [m for _, _, m in consumers for _ in range(2)]
        nl = jnp.concatenate([jnp.where(strict, nl[seg], 0.0) if m else nl[seg]
                              for seg, m in zip(segments, masked)], axis=0)
        csum = jnp.dot(nl.astype(BF16), lower, preferred_element_type=F32)
        probs = []
        for n, (u, _, m) in enumerate(consumers):
            heads = []
            for hh in range(2):
                seg = segments[2 * n + hh]
                if m:
                    a = jnp.where(strict, jnp.exp2(z[seg] - csum[seg]), 0.0)
                    carry_sc[u, hh] = jnp.broadcast_to(csum[seg][:, 0:1], (t, LANES))
                else:
                    carry = carry_sc[u, hh]
                    a = jnp.exp2((z[seg] - _tile_lanes(carry, t // LANES)) - csum[seg])
                    carry_sc[u, hh] = carry + csum[seg][:, 0:1]
                heads.append(a.astype(BF16))
            probs.append(jnp.concatenate(heads, axis=1))
        pv = jnp.dot(jnp.concatenate(probs, axis=0), v2, preferred_element_type=F32)
        for n, (u, _, m) in enumerate(consumers):
            if m:
                acc_sc[u] = pv[n * t:(n + 1) * t]
            else:
                acc_sc[u] += pv[n * t:(n + 1) * t]

    qs = [load_q(u) for u in range(n_sub)]

    for u in range(n_sub):
        qsq = qs[u][0].astype(F32) * qs[u][0].astype(F32)
        qn0 = jnp.sum(jnp.where(head0, qsq, 0.0), axis=1, keepdims=True)
        qn1 = jnp.sum(jnp.where(head0, 0.0, qsq), axis=1, keepdims=True)
        qnorm = jnp.max(jnp.maximum(qn0, qn1), axis=0, keepdims=True)
        thresh = jnp.sqrt(qnorm * knorm_sc[0:1, 0:1]) * NORM_BOUND_SLACK + EXP2_UNDERFLOW
        thresh_sm[u] = jnp.max(thresh)

    def near_passes(has_previous_block):
        for u in range(n_sub - 1, -1, -1):
            consumers = [(u, qs[u][1], True)]
            if u + 1 < n_sub:
                consumers.append((u + 1, qs[u + 1][1], False))
            block_pass(i * n_sub + u, consumers)
        if has_previous_block:
            block_pass(i * n_sub - 1, [(0, qs[0][1], False)])

    @pl.when(i == 0)
    def _():
        near_passes(False)

    @pl.when(i >= 1)
    def _():
        near_passes(True)

    def carry_min(u):
        return jnp.min(jnp.minimum(carry_sc[u, 0], carry_sc[u, 1]))

    for u in range(n_sub):
        done_sm[u] = (carry_min(u) >= thresh_sm[u]).astype(jnp.int32)

    def remaining_blocks(u, c):
        _, qm = load_q(u)

        def far_cond(state):
            jb, all_zero = state
            return jnp.logical_and(jb >= 0, all_zero == 0)

        def far_body(state):
            jb, _ = state
            block_pass(jb, [(u, qm, False)])
            return jb - 1, (carry_min(u) >= thresh_sm[u]).astype(jnp.int32)

        lax.while_loop(far_cond, far_body, (i * n_sub + u - 2, done_sm[u]))
        return c

    lax.fori_loop(0, n_sub, remaining_blocks, 0)

    for u in range(n_sub):
        o = acc_sc[u]
        sq = o * o
        ss0 = jnp.sum(jnp.where(head0, sq, 0.0), axis=1, keepdims=True)
        ss1 = jnp.sum(jnp.where(head0, 0.0, sq), axis=1, keepdims=True)
        ms = jnp.where(head0, ss0, ss1) * (1.0 / HEAD_DIM)
        o_ref[u * t:(u + 1) * t, :] = (o * lax.rsqrt(ms + RMS_EPS) * g_ref[...]).astype(o_ref.dtype)


def _sb_attention(qs, kst, vs, g2):
    s = qs.shape[0]
    tq, t = SB_Q_BLOCK, SB_K_BLOCK
    return pl.pallas_call(
        _sb_kernel,
        grid=(N_SB_HEADS // 2, s // tq),
        in_specs=[
            pl.BlockSpec((tq, LANES), lambda h, i: (i, h)),
            pl.BlockSpec((LANES, s), lambda h, i: (h, 0)),
            pl.BlockSpec((s, LANES), lambda h, i: (0, h)),
            pl.BlockSpec(g2.shape, lambda h, i: (0, 0)),
        ],
        out_specs=pl.BlockSpec((tq, LANES), lambda h, i: (i, h)),
        out_shape=jax.ShapeDtypeStruct((s, SB_WIDTH), BF16),
        scratch_shapes=[pltpu.VMEM((tq // t, 2, t, LANES), F32), pltpu.VMEM((tq // t, t, LANES), F32),
                        pltpu.VMEM((8, LANES), F32),
                        pltpu.SMEM((tq // t,), F32), pltpu.SMEM((tq // t,), jnp.int32)],
        compiler_params=pltpu.CompilerParams(
            dimension_semantics=("arbitrary", "arbitrary"), vmem_limit_bytes=VMEM_LIMIT),
        name="sb_attn",
    )(qs, kst, vs, g2)


def _out_kernel(d_ref, s_ref, wd_ref, ws_ref, x_ref, gpost_ref, gpre_ref, x1_ref, h2_ref):
    mix = (jnp.dot(d_ref[...], wd_ref[...], preferred_element_type=F32)
           + jnp.dot(s_ref[...], ws_ref[...], preferred_element_type=F32))
    x1 = x_ref[...] + _rms(mix, gpost_ref[...])
    x1_ref[...] = x1
    h2_ref[...] = _rms(x1, gpre_ref[...]).astype(h2_ref.dtype)


def _out_proj(diff_o, sb_o, w_od, w_os, x2, g_post, g_pre):
    s = x2.shape[0]
    rows = OUT_ROWS
    row_spec = lambda w: pl.BlockSpec((rows, w), lambda i: (i, 0))
    full = lambda a: pl.BlockSpec(a.shape, lambda i: (0, 0))
    return pl.pallas_call(
        _out_kernel,
        grid=(s // rows,),
        in_specs=[row_spec(DIFF_WIDTH), row_spec(SB_WIDTH), full(w_od), full(w_os), row_spec(D_MODEL),
                  full(g_post), full(g_pre)],
        out_specs=[row_spec(D_MODEL), row_spec(D_MODEL)],
        out_shape=[jax.ShapeDtypeStruct((s, D_MODEL), F32), jax.ShapeDtypeStruct((s, D_MODEL), BF16)],
        compiler_params=pltpu.CompilerParams(
            dimension_semantics=("arbitrary",), vmem_limit_bytes=VMEM_LIMIT),
        name="out_proj",
    )(diff_o, sb_o, w_od, w_os, x2, g_post, g_pre)


def _gelu_tanh(x):
    return 0.5 * x * (1.0 + jnp.tanh(math.sqrt(2.0 / math.pi) * (x + 0.044715 * (x * x * x))))


def _ffn_kernel(h_ref, halo_ref, wup_ref, cw_ref, cb_ref, wd_ref, x1_ref, g_ref, o_ref, u_sc):
    i = pl.program_id(0)
    rows, chunk = FFN_ROWS, FFN_CHUNK
    h = h_ref[...]
    halo = jnp.where(i > 0, halo_ref[...], jnp.zeros_like(halo_ref[...]))

    def conv(col, slot):
        cols = slice(col, col + chunk)
        u_sc[slot, 0:HALO_ROWS, :] = jnp.dot(halo, wup_ref[:, cols], preferred_element_type=F32)
        u_sc[slot, HALO_ROWS:, :] = jnp.dot(h, wup_ref[:, cols], preferred_element_type=F32)
        y = cb_ref[:, cols]
        for k in range(CONV_WIDTH):
            off = HALO_ROWS - (CONV_WIDTH - 1) + k
            y = y + u_sc[slot, off:off + rows, :] * cw_ref[k:k + 1, cols]
        return y

    n_chunk = D_FF // chunk
    gates = [conv(c * chunk, 2 * c) for c in range(n_chunk)]
    vals = [conv(D_FF + c * chunk, 2 * c + 1) for c in range(n_chunk)]
    y = None
    for c in range(n_chunk):
        act = (_gelu_tanh(gates[c]) * vals[c]).astype(BF16)
        part = jnp.dot(act, wd_ref[c * chunk:(c + 1) * chunk, :], preferred_element_type=F32)
        y = part if y is None else y + part
    o_ref[...] = x1_ref[...] + _rms(y, g_ref[...])


def _ffn(h2, w_up, conv_w, conv_b, w_down, x1, g_post):
    s = h2.shape[0]
    rows = FFN_ROWS
    halo_blocks = rows // HALO_ROWS
    row_spec = pl.BlockSpec((rows, D_MODEL), lambda i: (i, 0))
    halo_spec = pl.BlockSpec((HALO_ROWS, D_MODEL), lambda i: (jnp.maximum(i * halo_blocks - 1, 0), 0))
    resident = lambda a: pl.BlockSpec(a.shape, lambda i: (0, 0), pipeline_mode=pl.Buffered(1))
    return pl.pallas_call(
        _ffn_kernel,
        grid=(s // rows,),
        in_specs=[row_spec, halo_spec, resident(w_up), resident(conv_w), resident(conv_b),
                  resident(w_down), row_spec, resident(g_post)],
        out_specs=row_spec,
        out_shape=jax.ShapeDtypeStruct((s, D_MODEL), F32),
        scratch_shapes=[pltpu.VMEM((2 * (D_FF // FFN_CHUNK), HALO_ROWS + rows, FFN_CHUNK), F32)],
        compiler_params=pltpu.CompilerParams(
            dimension_semantics=("arbitrary",), vmem_limit_bytes=FFN_VMEM_LIMIT),
        name="ffn",
    )(h2, h2, w_up, conv_w, conv_b, w_down, x1, g_post)


def kernel(x, attn_pre_norm, w_qkv, lambda_q1, lambda_k1, lambda_q2, lambda_k2, diff_subln, sb_norm,
           rel_bias, w_o, attn_post_norm, ffn_pre_norm, w_up, conv_w, conv_b, w_down, ffn_post_norm):
    b, s, _ = x.shape
    assert b == 1 and attn_pre_norm.shape[0] == 1, "single sequence, single layer"
    assert s % max(PROJ_ROWS, 2 * DIFF_BLOCK, SB_Q_BLOCK, OUT_ROWS, FFN_ROWS) == 0
    x2 = x[0]
    row = lambda a: a.reshape(1, -1).astype(F32)

    qd, vd, qs, vs, kdt, kst = _proj(x2, row(attn_pre_norm), w_qkv[0].astype(BF16))

    bias = _bias_tiles(rel_bias.astype(F32))
    lam_params = jnp.concatenate([lambda_q1, lambda_k1, lambda_q2, lambda_k2], axis=0).astype(F32)
    diff_o = _diff_attention(lam_params, qd, kdt, vd, bias, row(diff_subln))
    sb_o = _sb_attention(qs, kst, vs, row(jnp.concatenate([sb_norm[0], sb_norm[0]])))

    wo = w_o[0].astype(BF16)
    x1, h2 = _out_proj(diff_o, sb_o, wo[:DIFF_WIDTH], wo[DIFF_WIDTH:], x2,
                       row(attn_post_norm), row(ffn_pre_norm))
    out = _ffn(h2, w_up[0].astype(BF16), conv_w[0].astype(F32), conv_b.astype(F32),
               w_down[0].astype(BF16), x1, row(ffn_post_norm))
    return out[None]
```

```python
import functools
import math

import jax
import jax.numpy as jnp
from jax import lax
from jax.experimental import pallas as pl
from jax.experimental.pallas import tpu as pltpu

F32 = jnp.float32
BF16 = jnp.bfloat16

D_MODEL = 1024
HEAD_DIM = 64
N_DIFF_HEADS = 4
N_SB_HEADS = 8
DIFF_WIDTH = N_DIFF_HEADS * 2 * HEAD_DIM
SB_WIDTH = N_SB_HEADS * HEAD_DIM
QKV_GROUP = DIFF_WIDTH
assert SB_WIDTH == QKV_GROUP
D_FF = 2816
CONV_WIDTH = 3
N_BUCKETS = 32
MAX_DISTANCE = 128
RMS_EPS = 1e-6
LAMBDA_INIT = 0.8 - 0.6 * math.exp(-0.3 * 0)

LOG2E = 1.4426950408889634
Q_SCALE = HEAD_DIM ** -0.5 * LOG2E
MASK_VALUE = -1e30
EXP2_UNDERFLOW = 160.0
NORM_BOUND_SLACK = 1.0 + 2.0 ** -10
SOFTMAX_MIN_DENOMINATOR = 2.0 ** -60

LANES = 128
PROJ_ROWS = 1024
DIFF_BLOCK = 512
SB_Q_BLOCK = 1024
SB_K_BLOCK = 256
OUT_ROWS = 1024
FFN_ROWS = 512
FFN_CHUNK = D_FF // 2
HALO_ROWS = 8
VMEM_LIMIT = 48 * 1024 * 1024
DIFF_VMEM_LIMIT = 58 * 1024 * 1024
FFN_VMEM_LIMIT = 58 * 1024 * 1024


def _tile_lanes(x, k):
    return jnp.concatenate([x] * k, axis=1) if k > 1 else x


def _rms(x, g):
    return x * lax.rsqrt(jnp.mean(x * x, axis=-1, keepdims=True) + RMS_EPS) * g


def _proj_kernel(x_ref, g_ref, w_ref, qd_ref, vd_ref, qs_ref, vs_ref, kdt_ref, kst_ref):
    hb = _rms(x_ref[...], g_ref[...]).astype(BF16)
    proj = lambda c: jnp.dot(hb, w_ref[:, c * QKV_GROUP:(c + 1) * QKV_GROUP], preferred_element_type=F32)
    qd_ref[...] = (proj(0) * Q_SCALE).astype(BF16)
    vd_ref[...] = proj(2).astype(BF16)
    qs_ref[...] = (proj(3) * Q_SCALE).astype(BF16)
    vs_ref[...] = proj(5).astype(BF16)
    proj_t = lambda c: lax.dot_general(w_ref[:, c * QKV_GROUP:(c + 1) * QKV_GROUP], hb, (((0,), (1,)), ((), ())),
                                       preferred_element_type=F32)
    kdt_ref[...] = proj_t(1).astype(BF16)
    kst_ref[...] = proj_t(4).astype(BF16)


def _proj(x2, g, w):
    s = x2.shape[0]
    rows = PROJ_ROWS
    row_spec = lambda w: pl.BlockSpec((rows, w), lambda i: (i, 0))
    full = lambda a: pl.BlockSpec(a.shape, lambda i: (0, 0))
    out_rows = jax.ShapeDtypeStruct((s, QKV_GROUP), BF16)
    out_t = jax.ShapeDtypeStruct((QKV_GROUP, s), BF16)
    return pl.pallas_call(
        _proj_kernel,
        grid=(s // rows,),
        in_specs=[row_spec(D_MODEL), full(g), full(w)],
        out_specs=[row_spec(QKV_GROUP)] * 4 + [pl.BlockSpec((QKV_GROUP, rows), lambda i: (0, i))] * 2,
        out_shape=[out_rows] * 4 + [out_t] * 2,
        compiler_params=pltpu.CompilerParams(
            dimension_semantics=("arbitrary",), vmem_limit_bytes=VMEM_LIMIT),
        name="proj",
    )(x2, g, w)


def _bias_kernel(tab_ref, o_ref):
    h = pl.program_id(0)
    w = pl.program_id(1)
    t, sub = DIFF_BLOCK, LANES

    def sub_tile(offset):
        rel = (lax.broadcasted_iota(jnp.int32, (sub, sub), 0)
               - lax.broadcasted_iota(jnp.int32, (sub, sub), 1) + offset)
        n = jnp.maximum(rel, 0)
        max_exact = N_BUCKETS // 2
        nf = jnp.maximum(n, 1).astype(F32)
        large = max_exact + jnp.floor(jnp.log(nf / max_exact) / math.log(MAX_DISTANCE / max_exact)
                                      * (N_BUCKETS - max_exact)).astype(jnp.int32)
        large = jnp.minimum(large, N_BUCKETS - 1)
        bucket = jnp.where(n < max_exact, n, large)
        b = jnp.zeros((sub, sub), F32)
        for k in range(N_BUCKETS):
            b = jnp.where(bucket == k, tab_ref[k, h], b)
        b = (b - tab_ref[N_BUCKETS - 1, h]) * LOG2E
        return jnp.where(rel >= 0, b, MASK_VALUE)

    def tile(first_offset):
        for br in range(t // sub):
            for bc in range(t // sub):
                offset = first_offset + (br - bc) * sub
                if offset + sub <= 0:
                    val = jnp.full((sub, sub), MASK_VALUE, F32)
                elif offset - sub >= MAX_DISTANCE:
                    val = jnp.zeros((sub, sub), F32)
                else:
                    val = sub_tile(offset)
                o_ref[0, 0, br * sub:(br + 1) * sub, bc * sub:(bc + 1) * sub] = val

    @pl.when(w == 0)
    def _():
        tile(0)

    @pl.when(w == 1)
    def _():
        tile(t)


def _bias_tiles(rel_bias):
    t = DIFF_BLOCK
    return pl.pallas_call(
        _bias_kernel,
        grid=(N_DIFF_HEADS, 2),
        in_specs=[pl.BlockSpec(memory_space=pltpu.SMEM)],
        out_specs=pl.BlockSpec((1, 1, t, t), lambda h, w: (h, w, 0, 0)),
        out_shape=jax.ShapeDtypeStruct((N_DIFF_HEADS, 2, t, t), F32),
        compiler_params=pltpu.CompilerParams(
            dimension_semantics=("arbitrary", "arbitrary"), vmem_limit_bytes=VMEM_LIMIT),
        name="bias",
    )(rel_bias)


def _diff_kernel(lam_ref, q_ref, kt_ref, v_ref, bias_ref, g_ref, o_ref,
                 acc_sc, stat_sc, s_sc, m_sc, l_sc, slow_acc_sc, ok_sm):
    i = pl.program_id(1)
    t = DIFF_BLOCK
    tq = 2 * t
    half_a, half_b = slice(0, t), slice(t, tq)
    lane = lax.broadcasted_iota(jnp.int32, (tq, LANES), 1)
    q = q_ref[...]
    n_pairs = jnp.maximum(i - 1, 0)

    @pl.when(i == 0)
    def _():
        def norm_body(c, ms):
            k = kt_ref[:, pl.ds(pl.multiple_of(c * t, t), t)].astype(F32)
            sq = k * k
            return (jnp.maximum(ms[0], jnp.sum(sq[:HEAD_DIM], axis=0, keepdims=True)),
                    jnp.maximum(ms[1], jnp.sum(sq[HEAD_DIM:], axis=0, keepdims=True)))

        zeros = jnp.zeros((1, t), F32)
        ms = lax.fori_loop(0, kt_ref.shape[1] // t, norm_body, (zeros, zeros))
        for mp in range(2):
            stat_sc[mp] = jnp.broadcast_to(jnp.max(ms[mp], axis=1, keepdims=True), stat_sc.shape[1:])
        bias_max = jnp.max(jnp.max(bias_ref[0, 0], axis=1, keepdims=True), axis=0, keepdims=True)
        stat_sc[2] = jnp.broadcast_to(bias_max, stat_sc.shape[1:])

    def write_out(rows, o0, o1):
        lp = lam_ref[...]
        lam = (jnp.exp(jnp.sum(lp[0:1] * lp[1:2], axis=1, keepdims=True))
               - jnp.exp(jnp.sum(lp[2:3] * lp[3:4], axis=1, keepdims=True)) + LAMBDA_INIT)
        o_ref[rows, :] = (_rms(o0 - lam * o1, g_ref[...]) * (1.0 - LAMBDA_INIT)).astype(o_ref.dtype)

    every = slice(0, tq)
    pair = 2 * t
    zero_q = jnp.zeros_like(q)
    qm = (jnp.where(lane < HEAD_DIM, q, zero_q), jnp.where(lane >= HEAD_DIM, q, zero_q))
    bias_diag, bias_prev = bias_ref[0, 0], bias_ref[0, 1]

    def fast_path(first_block, far_pairs):
        qsq = q.astype(F32) * q.astype(F32)
        bound = []
        for mp in range(2):
            own = (lane < HEAD_DIM) if mp == 0 else (lane >= HEAD_DIM)
            qn = jnp.sum(jnp.where(own, qsq, 0.0), axis=1, keepdims=True)
            b = jnp.sqrt(qn * stat_sc[mp][0:1, 0:1]) * NORM_BOUND_SLACK + stat_sc[2][0:1, 0:1]
            bound.append(jnp.broadcast_to(b, (tq, LANES)))

        acc_sc[...] = jnp.zeros(acc_sc.shape, F32)

        def scores(rows, start, width):
            kt = kt_ref[:, pl.ds(start, width)]
            return [jnp.dot(qm[mp][rows], kt, preferred_element_type=F32)
                    - _tile_lanes(bound[mp][rows], width // LANES) for mp in range(2)]

        def accumulate(rows, s, start, width):
            v_aug = jnp.concatenate([v_ref[pl.ds(start, width), :], jnp.ones((width, LANES), BF16)],
                                    axis=1)
            for mp in range(2):
                acc_sc[mp, rows, :] += jnp.dot(jnp.exp2(s[mp]).astype(BF16), v_aug,
                                               preferred_element_type=F32)

        def last_block_of_b(block):
            start = pl.multiple_of(block * t, t)
            s = [x + bias_diag for x in scores(half_b, start, t)]
            accumulate(half_b, s, start, t)

        if far_pairs is not None:
            first = scores(every, 0, pair)
            for mp in range(2):
                s_sc[mp] = first[mp]

            def far_body(r, carry):
                cur = [s_sc[mp] for mp in range(2)]
                nxt = scores(every, pl.multiple_of((r + 1) * pair, pair), pair)
                accumulate(every, cur, pl.multiple_of(r * pair, pair), pair)
                for mp in range(2):
                    s_sc[mp] = nxt[mp]
                return carry

            lax.fori_loop(0, far_pairs - 1, far_body, 0)
            accumulate(every, [s_sc[mp] for mp in range(2)],
                       pl.multiple_of((far_pairs - 1) * pair, pair), pair)

        if first_block:
            s = [x + jnp.concatenate([bias_diag, bias_prev], axis=0) for x in scores(every, 0, t)]
            accumulate(every, s, 0, t)
            last_block_of_b(1)
        else:
            start = pl.multiple_of((2 * i - 2) * t, t)
            s = scores(every, start, 3 * t)
            for mp in range(2):
                x = s[mp]
                s[mp] = jnp.concatenate([
                    x[:, :t],
                    jnp.concatenate([x[half_a, t:2 * t] + bias_prev, x[half_b, t:2 * t]], axis=0),
                    jnp.concatenate([x[half_a, 2 * t:] + bias_diag, x[half_b, 2 * t:] + bias_prev], axis=0),
                ], axis=1)
            accumulate(every, s, start, 3 * t)
            last_block_of_b(2 * i + 1)

        l0, l1 = acc_sc[0][:, LANES:], acc_sc[1][:, LANES:]
        write_out(every, acc_sc[0][:, :LANES] / l0, acc_sc[1][:, :LANES] / l1)
        ok_sm[0] = (jnp.min(jnp.minimum(l0, l1)) >= SOFTMAX_MIN_DENOMINATOR).astype(jnp.int32)

    @pl.when(i == 0)
    def _():
        fast_path(True, None)

    @pl.when(i == 1)
    def _():
        fast_path(False, None)

    @pl.when(i >= 2)
    def _():
        fast_path(False, n_pairs)

    @pl.when(ok_sm[0] == 0)
    def _():
        def online_softmax_half(rows, diag_block):
            m_sc[...] = jnp.full(m_sc.shape, MASK_VALUE, F32)
            l_sc[...] = jnp.zeros(l_sc.shape, F32)
            slow_acc_sc[...] = jnp.zeros(slow_acc_sc.shape, F32)

            def step(j, bias):
                start = pl.multiple_of(j * t, t)
                kt = kt_ref[:, pl.ds(start, t)]
                v = v_ref[pl.ds(start, t), :]
                for mp in range(2):
                    s = jnp.dot(qm[mp][rows], kt, preferred_element_type=F32)
                    if bias is not None:
                        s = s + bias
                    m_prev = m_sc[mp]
                    m_new = jnp.maximum(m_prev, jnp.max(s, axis=1, keepdims=True))
                    alpha = jnp.exp2(m_prev - m_new)
                    p = jnp.exp2(s - _tile_lanes(m_new, t // LANES))
                    l_sc[mp] = alpha * l_sc[mp] + jnp.sum(p, axis=1, keepdims=True)
                    slow_acc_sc[mp] = (alpha * slow_acc_sc[mp]
                                       + jnp.dot(p.astype(BF16), v, preferred_element_type=F32))
                    m_sc[mp] = m_new

            def slow_body(j, carry):
                step(j, None)
                return carry

            lax.fori_loop(0, jnp.maximum(diag_block - 1, 0), slow_body, 0)

            @pl.when(diag_block >= 1)
            def _():
                step(diag_block - 1, bias_prev)

            step(diag_block, bias_diag)
            write_out(rows, slow_acc_sc[0] / l_sc[0], slow_acc_sc[1] / l_sc[1])

        online_softmax_half(half_a, 2 * i)
        online_softmax_half(half_b, 2 * i + 1)


def _diff_attention(lam_params, qd, kdt, vd, bias, g):
    s = qd.shape[0]
    t = DIFF_BLOCK
    tq = 2 * t
    return pl.pallas_call(
        _diff_kernel,
        grid=(N_DIFF_HEADS, s // tq),
        in_specs=[
            pl.BlockSpec(lam_params.shape, lambda h, i: (0, 0)),
            pl.BlockSpec((tq, LANES), lambda h, i: (i, h)),
            pl.BlockSpec((LANES, s), lambda h, i: (h, 0)),
            pl.BlockSpec((s, LANES), lambda h, i: (0, h)),
            pl.BlockSpec((1, 2, t, t), lambda h, i: (h, 0, 0, 0)),
            pl.BlockSpec(g.shape, lambda h, i: (0, 0)),
        ],
        out_specs=pl.BlockSpec((tq, LANES), lambda h, i: (i, h)),
        out_shape=jax.ShapeDtypeStruct((s, DIFF_WIDTH), BF16),
        scratch_shapes=[pltpu.VMEM((2, tq, 2 * LANES), F32), pltpu.VMEM((3, 8, LANES), F32),
                        pltpu.VMEM((2, tq, 2 * t), F32)]
        + [pltpu.VMEM((2, t, LANES), F32)] * 3 + [pltpu.SMEM((1,), jnp.int32)],
        compiler_params=pltpu.CompilerParams(
            dimension_semantics=("arbitrary", "arbitrary"), vmem_limit_bytes=DIFF_VMEM_LIMIT),
        name="diff_attn",
    )(lam_params, qd, kdt, vd, bias, g)


def _sb_kernel(q_ref, kt_ref, v_ref, g_ref, o_ref, carry_sc, acc_sc, knorm_sc, thresh_sm, done_sm):
    i = pl.program_id(1)
    t = SB_K_BLOCK
    n_sub = SB_Q_BLOCK // t
    lane = lax.broadcasted_iota(jnp.int32, (t, LANES), 1)
    head0 = lane < HEAD_DIM
    lower = (lax.broadcasted_iota(jnp.int32, (t, t), 0)
             >= lax.broadcasted_iota(jnp.int32, (t, t), 1)).astype(BF16)
    strict = lax.broadcasted_iota(jnp.int32, (t, t), 0) > lax.broadcasted_iota(jnp.int32, (t, t), 1)

    @pl.when(i == 0)
    def _():
        def norm_body(c, m):
            k = kt_ref[:, pl.ds(pl.multiple_of(c * t, t), t)].astype(F32)
            sq = k * k
            return jnp.maximum(m, jnp.maximum(jnp.sum(sq[:HEAD_DIM], axis=0, keepdims=True),
                                              jnp.sum(sq[HEAD_DIM:], axis=0, keepdims=True)))

        m = lax.fori_loop(0, kt_ref.shape[1] // t, norm_body, jnp.zeros((1, t), F32))
        knorm_sc[...] = jnp.broadcast_to(jnp.max(m, axis=1, keepdims=True), knorm_sc.shape)

    def load_q(u):
        q = q_ref[pl.ds(pl.multiple_of(u * t, t), t), :]
        zero_q = jnp.zeros_like(q)
        return q, (jnp.where(head0, q, zero_q), jnp.where(head0, zero_q, q))

    def block_pass(jb, consumers):
        start = pl.multiple_of(jb * t, t)
        kt = kt_ref[:, pl.ds(start, t)]
        v = v_ref[pl.ds(start, t), :]
        zero_v = jnp.zeros_like(v)
        v2 = jnp.concatenate([jnp.where(head0, v, zero_v), jnp.where(head0, zero_v, v)], axis=0)
        q_rows = [qm[hh] for _, qm, _ in consumers for hh in range(2)]
        z = jnp.dot(jnp.concatenate(q_rows, axis=0), kt, preferred_element_type=F32)
        nl = jnp.maximum(z, 0.0) + jnp.log(1.0 + jnp.exp2(-jnp.abs(z))) * LOG2E
        segments = [slice(n * t, (n + 1) * t) for n in range(len(q_rows))]
        masked = [m for _, _, m in consumers for _ in range(2)]
        nl = jnp.concatenate([jnp.where(strict, nl[seg], 0.0) if m else nl[seg]
                              for seg, m in zip(segments, masked)], axis=0)
        csum = jnp.dot(nl.astype(BF16), lower, preferred_element_type=F32)
        probs = []
        for n, (u, _, m) in enumerate(consumers):
            heads = []
            for hh in range(2):
                seg = segments[2 * n + hh]
                if m:
                    a = jnp.where(strict, jnp.exp2(z[seg] - csum[seg]), 0.0)
                    carry_sc[u, hh] = jnp.broadcast_to(csum[seg][:, 0:1], (t, LANES))
                else:
                    carry = carry_sc[u, hh]
                    a = jnp.exp2((z[seg] - _tile_lanes(carry, t // LANES)) - csum[seg])
                    carry_sc[u, hh] = carry + csum[seg][:, 0:1]
                heads.append(a.astype(BF16))
            probs.append(jnp.concatenate(heads, axis=1))
        pv = jnp.dot(jnp.concatenate(probs, axis=0), v2, preferred_element_type=F32)
        for n, (u, _, m) in enumerate(consumers):
            if m:
                acc_sc[u] = pv[n * t:(n + 1) * t]
            else:
                acc_sc[u] += pv[n * t:(n + 1) * t]

    qs = [load_q(u) for u in range(n_sub)]

    for u in range(n_sub):
        qsq = qs[u][0].astype(F32) * qs[u][0].astype(F32)
        qn0 = jnp.sum(jnp.where(head0, qsq, 0.0), axis=1, keepdims=True)
        qn1 = jnp.sum(jnp.where(head0, 0.0, qsq), axis=1, keepdims=True)
        qnorm = jnp.max(jnp.maximum(qn0, qn1), axis=0, keepdims=True)
        thresh = jnp.sqrt(qnorm * knorm_sc[0:1, 0:1]) * NORM_BOUND_SLACK + EXP2_UNDERFLOW
        thresh_sm[u] = jnp.max(thresh)

    def near_passes(has_previous_block):
        for u in range(n_sub - 1, -1, -1):
            consumers = [(u, qs[u][1], True)]
            if u + 1 < n_sub:
                consumers.append((u + 1, qs[u + 1][1], False))
            block_pass(i * n_sub + u, consumers)
        if has_previous_block:
            block_pass(i * n_sub - 1, [(0, qs[0][1], False)])

    @pl.when(i == 0)
    def _():
        near_passes(False)

    @pl.when(i >= 1)
    def _():
        near_passes(True)

    def carry_min(u):
        return jnp.min(jnp.minimum(carry_sc[u, 0], carry_sc[u, 1]))

    for u in range(n_sub):
        done_sm[u] = (carry_min(u) >= thresh_sm[u]).astype(jnp.int32)

    def remaining_blocks(u, c):
        _, qm = load_q(u)

        def far_cond(state):
            jb, all_zero = state
            return jnp.logical_and(jb >= 0, all_zero == 0)

        def far_body(state):
            jb, _ = state
            block_pass(jb, [(u, qm, False)])
            return jb - 1, (carry_min(u) >= thresh_sm[u]).astype(jnp.int32)

        lax.while_loop(far_cond, far_body, (i * n_sub + u - 2, done_sm[u]))
        return c

    lax.fori_loop(0, n_sub, remaining_blocks, 0)

    for u in range(n_sub):
        o = acc_sc[u]
        sq = o * o
        ss0 = jnp.sum(jnp.where(head0, sq, 0.0), axis=1, keepdims=True)
        ss1 = jnp.sum(jnp.where(head0, 0.0, sq), axis=1, keepdims=True)
        ms = jnp.where(head0, ss0, ss1) * (1.0 / HEAD_DIM)
        o_ref[u * t:(u + 1) * t, :] = (o * lax.rsqrt(ms + RMS_EPS) * g_ref[...]).astype(o_ref.dtype)


def _sb_attention(qs, kst, vs, g2):
    s = qs.shape[0]
    tq, t = SB_Q_BLOCK, SB_K_BLOCK
    return pl.pallas_call(
        _sb_kernel,
        grid=(N_SB_HEADS // 2, s // tq),
        in_specs=[
            pl.BlockSpec((tq, LANES), lambda h, i: (i, h)),
            pl.BlockSpec((LANES, s), lambda h, i: (h, 0)),
            pl.BlockSpec((s, LANES), lambda h, i: (0, h)),
            pl.BlockSpec(g2.shape, lambda h, i: (0, 0)),
        ],
        out_specs=pl.BlockSpec((tq, LANES), lambda h, i: (i, h)),
        out_shape=jax.ShapeDtypeStruct((s, SB_WIDTH), BF16),
        scratch_shapes=[pltpu.VMEM((tq // t, 2, t, LANES), F32), pltpu.VMEM((tq // t, t, LANES), F32),
                        pltpu.VMEM((8, LANES), F32),
                        pltpu.SMEM((tq // t,), F32), pltpu.SMEM((tq // t,), jnp.int32)],
        compiler_params=pltpu.CompilerParams(
            dimension_semantics=("arbitrary", "arbitrary"), vmem_limit_bytes=VMEM_LIMIT),
        name="sb_attn",
    )(qs, kst, vs, g2)


def _out_kernel(d_ref, s_ref, wd_ref, ws_ref, x_ref, gpost_ref, gpre_ref, x1_ref, h2_ref):
    mix = (jnp.dot(d_ref[...], wd_ref[...], preferred_element_type=F32)
           + jnp.dot(s_ref[...], ws_ref[...], preferred_element_type=F32))
    x1 = x_ref[...] + _rms(mix, gpost_ref[...])
    x1_ref[...] = x1
    h2_ref[...] = _rms(x1, gpre_ref[...]).astype(h2_ref.dtype)


def _out_proj(diff_o, sb_o, w_od, w_os, x2, g_post, g_pre):
    s = x2.shape[0]
    rows = OUT_ROWS
    row_spec = lambda w: pl.BlockSpec((rows, w), lambda i: (i, 0))
    full = lambda a: pl.BlockSpec(a.shape, lambda i: (0, 0))
    return pl.pallas_call(
        _out_kernel,
        grid=(s // rows,),
        in_specs=[row_spec(DIFF_WIDTH), row_spec(SB_WIDTH), full(w_od), full(w_os), row_spec(D_MODEL),
                  full(g_post), full(g_pre)],
        out_specs=[row_spec(D_MODEL), row_spec(D_MODEL)],
        out_shape=[jax.ShapeDtypeStruct((s, D_MODEL), F32), jax.ShapeDtypeStruct((s, D_MODEL), BF16)],
        compiler_params=pltpu.CompilerParams(
            dimension_semantics=("arbitrary",), vmem_limit_bytes=VMEM_LIMIT),
        name="out_proj",
    )(diff_o, sb_o, w_od, w_os, x2, g_post, g_pre)


def _gelu_tanh(x):
    return 0.5 * x * (1.0 + jnp.tanh(math.sqrt(2.0 / math.pi) * (x + 0.044715 * (x * x * x))))


def _ffn_kernel(h_ref, halo_ref, wup_ref, cw_ref, cb_ref, wd_ref, x1_ref, g_ref, o_ref, u_sc):
    i = pl.program_id(0)
    rows, chunk = FFN_ROWS, FFN_CHUNK
    h = h_ref[...]
    halo = jnp.where(i > 0, halo_ref[...], jnp.zeros_like(halo_ref[...]))

    def conv(col, slot):
        cols = slice(col, col + chunk)
        u_sc[slot, 0:HALO_ROWS, :] = jnp.dot(halo, wup_ref[:, cols], preferred_element_type=F32)
        u_sc[slot, HALO_ROWS:, :] = jnp.dot(h, wup_ref[:, cols], preferred_element_type=F32)
        y = cb_ref[:, cols]
        for k in range(CONV_WIDTH):
            off = HALO_ROWS - (CONV_WIDTH - 1) + k
            y = y + u_sc[slot, off:off + rows, :] * cw_ref[k:k + 1, cols]
        return y

    n_chunk = D_FF // chunk
    gates = [conv(c * chunk, 2 * c) for c in range(n_chunk)]
    vals = [conv(D_FF + c * chunk, 2 * c + 1) for c in range(n_chunk)]
    y = None
    for c in range(n_chunk):
        act = (_gelu_tanh(gates[c]) * vals[c]).astype(BF16)
        part = jnp.dot(act, wd_ref[c * chunk:(c + 1) * chunk, :], preferred_element_type=F32)
        y = part if y is None else y + part
    o_ref[...] = x1_ref[...] + _rms(y, g_ref[...])


def _ffn(h2, w_up, conv_w, conv_b, w_down, x1, g_post):
    s = h2.shape[0]
    rows = FFN_ROWS
    halo_blocks = rows // HALO_ROWS
    row_spec = pl.BlockSpec((rows, D_MODEL), lambda i: (i, 0))
    halo_spec = pl.BlockSpec((HALO_ROWS, D_MODEL), lambda i: (jnp.maximum(i * halo_blocks - 1, 0), 0))
    resident = lambda a: pl.BlockSpec(a.shape, lambda i: (0, 0), pipeline_mode=pl.Buffered(1))
    return pl.pallas_call(
        _ffn_kernel,
        grid=(s // rows,),
        in_specs=[row_spec, halo_spec, resident(w_up), resident(conv_w), resident(conv_b),
                  resident(w_down), row_spec, resident(g_post)],
        out_specs=row_spec,
        out_shape=jax.ShapeDtypeStruct((s, D_MODEL), F32),
        scratch_shapes=[pltpu.VMEM((2 * (D_FF // FFN_CHUNK), HALO_ROWS + rows, FFN_CHUNK), F32)],
        compiler_params=pltpu.CompilerParams(
            dimension_semantics=("arbitrary",), vmem_limit_bytes=FFN_VMEM_LIMIT),
        name="ffn",
    )(h2, h2, w_up, conv_w, conv_b, w_down, x1, g_post)


def kernel(x, attn_pre_norm, w_qkv, lambda_q1, lambda_k1, lambda_q2, lambda_k2, diff_subln, sb_norm,
           rel_bias, w_o, attn_post_norm, ffn_pre_norm, w_up, conv_w, conv_b, w_down, ffn_post_norm):
    b, s, _ = x.shape
    assert b == 1 and attn_pre_norm.shape[0] == 1, "single sequence, single layer"
    assert s % max(PROJ_ROWS, 2 * DIFF_BLOCK, SB_Q_BLOCK, OUT_ROWS, FFN_ROWS) == 0
    x2 = x[0]
    row = lambda a: a.reshape(1, -1).astype(F32)

    qd, vd, qs, vs, kdt, kst = _proj(x2, row(attn_pre_norm), w_qkv[0].astype(BF16))

    bias = _bias_tiles(rel_bias.astype(F32))
    lam_params = jnp.concatenate([lambda_q1, lambda_k1, lambda_q2, lambda_k2], axis=0).astype(F32)
    diff_o = _diff_attention(lam_params, qd, kdt, vd, bias, row(diff_subln))
    sb_o = _sb_attention(qs, kst, vs, row(jnp.concatenate([sb_norm[0], sb_norm[0]])))

    wo = w_o[0].astype(BF16)
    x1, h2 = _out_proj(diff_o, sb_o, wo[:DIFF_WIDTH], wo[DIFF_WIDTH:], x2,
                       row(attn_post_norm), row(ffn_pre_norm))
    out = _ffn(h2, w_up[0].astype(BF16), conv_w[0].astype(F32), conv_b.astype(F32),
               w_down[0].astype(BF16), x1, row(ffn_post_norm))
    return out[None]
```

```python
import math

import jax
import jax.numpy as jnp
from jax import lax
from jax.experimental import pallas as pl
from jax.experimental.pallas import tpu as pltpu

F32 = jnp.float32
BF16 = jnp.bfloat16

D_MODEL = 1024
HEAD_DIM = 64
N_DIFF_HEADS = 4
N_SB_HEADS = 8
DIFF_WIDTH = N_DIFF_HEADS * 2 * HEAD_DIM
SB_WIDTH = N_SB_HEADS * HEAD_DIM
QKV_GROUP = DIFF_WIDTH
assert SB_WIDTH == QKV_GROUP
D_FF = 2816
CONV_WIDTH = 3
N_BUCKETS = 32
MAX_DISTANCE = 128
RMS_EPS = 1e-6
LAMBDA_INIT = 0.8 - 0.6 * math.exp(-0.3 * 0)

LOG2E = 1.4426950408889634
Q_SCALE = HEAD_DIM ** -0.5 * LOG2E
MASK_VALUE = -1e30
EXP2_UNDERFLOW = 160.0
NORM_BOUND_SLACK = 1.0 + 2.0 ** -10
SOFTMAX_MIN_DENOMINATOR = 2.0 ** -60

LANES = 128
PROJ_ROWS = 1024
DIFF_BLOCK = 512
SB_Q_BLOCK = 2048
SB_K_BLOCK = 256
OUT_ROWS = 1024
FFN_ROWS = 512
FFN_CHUNK = D_FF // 2
HALO_ROWS = 8
VMEM_LIMIT = 48 * 1024 * 1024
DIFF_VMEM_LIMIT = 58 * 1024 * 1024
FFN_VMEM_LIMIT = 58 * 1024 * 1024


def _tile_lanes(x, k):
    return jnp.concatenate([x] * k, axis=1) if k > 1 else x


def _rms(x, g):
    return x * lax.rsqrt(jnp.mean(x * x, axis=-1, keepdims=True) + RMS_EPS) * g


def _proj_kernel(x_ref, g_ref, w_ref, qd_ref, vd_ref, qs_ref, vs_ref, kdt_ref, kst_ref):
    hb = _rms(x_ref[...], g_ref[...]).astype(BF16)
    proj = lambda c: jnp.dot(hb, w_ref[:, c * QKV_GROUP:(c + 1) * QKV_GROUP], preferred_element_type=F32)
    qd_ref[...] = (proj(0) * Q_SCALE).astype(BF16)
    vd_ref[...] = proj(2).astype(BF16)
    qs_ref[...] = (proj(3) * Q_SCALE).astype(BF16)
    vs_ref[...] = proj(5).astype(BF16)
    proj_t = lambda c: lax.dot_general(w_ref[:, c * QKV_GROUP:(c + 1) * QKV_GROUP], hb, (((0,), (1,)), ((), ())),
                                       preferred_element_type=F32)
    kdt_ref[...] = proj_t(1).astype(BF16)
    kst_ref[...] = proj_t(4).astype(BF16)


def _proj(x2, g, w):
    s = x2.shape[0]
    rows = PROJ_ROWS
    row_spec = lambda w: pl.BlockSpec((rows, w), lambda i: (i, 0))
    full = lambda a: pl.BlockSpec(a.shape, lambda i: (0, 0))
    out_rows = jax.ShapeDtypeStruct((s, QKV_GROUP), BF16)
    out_t = jax.ShapeDtypeStruct((QKV_GROUP, s), BF16)
    return pl.pallas_call(
        _proj_kernel,
        grid=(s // rows,),
        in_specs=[row_spec(D_MODEL), full(g), full(w)],
        out_specs=[row_spec(QKV_GROUP)] * 4 + [pl.BlockSpec((QKV_GROUP, rows), lambda i: (0, i))] * 2,
        out_shape=[out_rows] * 4 + [out_t] * 2,
        compiler_params=pltpu.CompilerParams(
            dimension_semantics=("arbitrary",), vmem_limit_bytes=VMEM_LIMIT),
        name="proj",
    )(x2, g, w)


def _bias_kernel(tab_ref, o_ref):
    h = pl.program_id(0)
    w = pl.program_id(1)
    t, sub = DIFF_BLOCK, LANES

    def sub_tile(offset):
        rel = (lax.broadcasted_iota(jnp.int32, (sub, sub), 0)
               - lax.broadcasted_iota(jnp.int32, (sub, sub), 1) + offset)
        n = jnp.maximum(rel, 0)
        max_exact = N_BUCKETS // 2
        nf = jnp.maximum(n, 1).astype(F32)
        large = max_exact + jnp.floor(jnp.log(nf / max_exact) / math.log(MAX_DISTANCE / max_exact)
                                      * (N_BUCKETS - max_exact)).astype(jnp.int32)
        large = jnp.minimum(large, N_BUCKETS - 1)
        bucket = jnp.where(n < max_exact, n, large)
        b = jnp.zeros((sub, sub), F32)
        for k in range(N_BUCKETS):
            b = jnp.where(bucket == k, tab_ref[k, h], b)
        b = (b - tab_ref[N_BUCKETS - 1, h]) * LOG2E
        return jnp.where(rel >= 0, b, MASK_VALUE)

    def tile(first_offset):
        for br in range(t // sub):
            for bc in range(t // sub):
                offset = first_offset + (br - bc) * sub
                if offset + sub <= 0:
                    val = jnp.full((sub, sub), MASK_VALUE, F32)
                elif offset - sub >= MAX_DISTANCE:
                    val = jnp.zeros((sub, sub), F32)
                else:
                    val = sub_tile(offset)
                o_ref[0, 0, br * sub:(br + 1) * sub, bc * sub:(bc + 1) * sub] = val

    @pl.when(w == 0)
    def _():
        tile(0)

    @pl.when(w == 1)
    def _():
        tile(t)


def _bias_tiles(rel_bias):
    t = DIFF_BLOCK
    return pl.pallas_call(
        _bias_kernel,
        grid=(N_DIFF_HEADS, 2),
        in_specs=[pl.BlockSpec(memory_space=pltpu.SMEM)],
        out_specs=pl.BlockSpec((1, 1, t, t), lambda h, w: (h, w, 0, 0)),
        out_shape=jax.ShapeDtypeStruct((N_DIFF_HEADS, 2, t, t), F32),
        compiler_params=pltpu.CompilerParams(
            dimension_semantics=("arbitrary", "arbitrary"), vmem_limit_bytes=VMEM_LIMIT),
        name="bias",
    )(rel_bias)


def _diff_kernel(lam_ref, q_ref, kt_ref, v_ref, bias_ref, g_ref, o_ref,
                 acc_sc, stat_sc, s_sc, m_sc, l_sc, slow_acc_sc, ok_sm):
    i = pl.program_id(1)
    t = DIFF_BLOCK
    tq = 2 * t
    half_a, half_b = slice(0, t), slice(t, tq)
    lane = lax.broadcasted_iota(jnp.int32, (tq, LANES), 1)
    q = q_ref[...]
    n_pairs = jnp.maximum(i - 1, 0)

    @pl.when(i == 0)
    def _():
        def norm_body(c, ms):
            k = kt_ref[:, pl.ds(pl.multiple_of(c * t, t), t)].astype(F32)
            sq = k * k
            return (jnp.maximum(ms[0], jnp.sum(sq[:HEAD_DIM], axis=0, keepdims=True)),
                    jnp.maximum(ms[1], jnp.sum(sq[HEAD_DIM:], axis=0, keepdims=True)))

        zeros = jnp.zeros((1, t), F32)
        ms = lax.fori_loop(0, kt_ref.shape[1] // t, norm_body, (zeros, zeros))
        for mp in range(2):
            stat_sc[mp] = jnp.broadcast_to(jnp.max(ms[mp], axis=1, keepdims=True), stat_sc.shape[1:])
        bias_max = jnp.max(jnp.max(bias_ref[0, 0], axis=1, keepdims=True), axis=0, keepdims=True)
        stat_sc[2] = jnp.broadcast_to(bias_max, stat_sc.shape[1:])

    def write_out(rows, o0, o1):
        lp = lam_ref[...]
        lam = (jnp.exp(jnp.sum(lp[0:1] * lp[1:2], axis=1, keepdims=True))
               - jnp.exp(jnp.sum(lp[2:3] * lp[3:4], axis=1, keepdims=True)) + LAMBDA_INIT)
        o_ref[rows, :] = (_rms(o0 - lam * o1, g_ref[...]) * (1.0 - LAMBDA_INIT)).astype(o_ref.dtype)

    every = slice(0, tq)
    pair = 2 * t
    zero_q = jnp.zeros_like(q)
    qm = (jnp.where(lane < HEAD_DIM, q, zero_q), jnp.where(lane >= HEAD_DIM, q, zero_q))
    bias_diag, bias_prev = bias_ref[0, 0], bias_ref[0, 1]

    def fast_path(first_block, far_pairs):
        qsq = q.astype(F32) * q.astype(F32)
        bound = []
        for mp in range(2):
            own = (lane < HEAD_DIM) if mp == 0 else (lane >= HEAD_DIM)
            qn = jnp.sum(jnp.where(own, qsq, 0.0), axis=1, keepdims=True)
            b = jnp.sqrt(qn * stat_sc[mp][0:1, 0:1]) * NORM_BOUND_SLACK + stat_sc[2][0:1, 0:1]
            bound.append(jnp.broadcast_to(b, (tq, LANES)))

        acc_sc[...] = jnp.zeros(acc_sc.shape, F32)

        def scores(rows, start, width):
            kt = kt_ref[:, pl.ds(start, width)]
            return [jnp.dot(qm[mp][rows], kt, preferred_element_type=F32)
                    - _tile_lanes(bound[mp][rows], width // LANES) for mp in range(2)]

        def accumulate(rows, s, start, width):
            v_aug = jnp.concatenate([v_ref[pl.ds(start, width), :], jnp.ones((width, LANES), BF16)],
                                    axis=1)
            for mp in range(2):
                acc_sc[mp, rows, :] += jnp.dot(jnp.exp2(s[mp]).astype(BF16), v_aug,
                                               preferred_element_type=F32)

        def last_block_of_b(block):
            start = pl.multiple_of(block * t, t)
            s = [x + bias_diag for x in scores(half_b, start, t)]
            accumulate(half_b, s, start, t)

        if far_pairs is not None:
            first = scores(every, 0, pair)
            for mp in range(2):
                s_sc[mp] = first[mp]

            def far_body(r, carry):
                cur = [s_sc[mp] for mp in range(2)]
                nxt = scores(every, pl.multiple_of((r + 1) * pair, pair), pair)
                accumulate(every, cur, pl.multiple_of(r * pair, pair), pair)
                for mp in range(2):
                    s_sc[mp] = nxt[mp]
                return carry

            lax.fori_loop(0, far_pairs - 1, far_body, 0)
            accumulate(every, [s_sc[mp] for mp in range(2)],
                       pl.multiple_of((far_pairs - 1) * pair, pair), pair)

        if first_block:
            s = [x + jnp.concatenate([bias_diag, bias_prev], axis=0) for x in scores(every, 0, t)]
            accumulate(every, s, 0, t)
            last_block_of_b(1)
        else:
            start = pl.multiple_of((2 * i - 2) * t, t)
            s = scores(every, start, 3 * t)
            for mp in range(2):
                x = s[mp]
                s[mp] = jnp.concatenate([
                    x[:, :t],
                    jnp.concatenate([x[half_a, t:2 * t] + bias_prev, x[half_b, t:2 * t]], axis=0),
                    jnp.concatenate([x[half_a, 2 * t:] + bias_diag, x[half_b, 2 * t:] + bias_prev], axis=0),
                ], axis=1)
            accumulate(every, s, start, 3 * t)
            last_block_of_b(2 * i + 1)

        l0, l1 = acc_sc[0][:, LANES:], acc_sc[1][:, LANES:]
        write_out(every, acc_sc[0][:, :LANES] / l0, acc_sc[1][:, :LANES] / l1)
        ok_sm[0] = (jnp.min(jnp.minimum(l0, l1)) >= SOFTMAX_MIN_DENOMINATOR).astype(jnp.int32)

    @pl.when(i == 0)
    def _():
        fast_path(True, None)

    @pl.when(i == 1)
    def _():
        fast_path(False, None)

    @pl.when(i >= 2)
    def _():
        fast_path(False, n_pairs)

    @pl.when(ok_sm[0] == 0)
    def _():
        def online_softmax_half(rows, diag_block):
            m_sc[...] = jnp.full(m_sc.shape, MASK_VALUE, F32)
            l_sc[...] = jnp.zeros(l_sc.shape, F32)
            slow_acc_sc[...] = jnp.zeros(slow_acc_sc.shape, F32)

            def step(j, bias):
                start = pl.multiple_of(j * t, t)
                kt = kt_ref[:, pl.ds(start, t)]
                v = v_ref[pl.ds(start, t), :]
                for mp in range(2):
                    s = jnp.dot(qm[mp][rows], kt, preferred_element_type=F32)
                    if bias is not None:
                        s = s + bias
                    m_prev = m_sc[mp]
                    m_new = jnp.maximum(m_prev, jnp.max(s, axis=1, keepdims=True))
                    alpha = jnp.exp2(m_prev - m_new)
                    p = jnp.exp2(s - _tile_lanes(m_new, t // LANES))
                    l_sc[mp] = alpha * l_sc[mp] + jnp.sum(p, axis=1, keepdims=True)
                    slow_acc_sc[mp] = (alpha * slow_acc_sc[mp]
                                       + jnp.dot(p.astype(BF16), v, preferred_element_type=F32))
                    m_sc[mp] = m_new

            def slow_body(j, carry):
                step(j, None)
                return carry

            lax.fori_loop(0, jnp.maximum(diag_block - 1, 0), slow_body, 0)

            @pl.when(diag_block >= 1)
            def _():
                step(diag_block - 1, bias_prev)

            step(diag_block, bias_diag)
            write_out(rows, slow_acc_sc[0] / l_sc[0], slow_acc_sc[1] / l_sc[1])

        online_softmax_half(half_a, 2 * i)
        online_softmax_half(half_b, 2 * i + 1)


def _diff_attention(lam_params, qd, kdt, vd, bias, g):
    s = qd.shape[0]
    t = DIFF_BLOCK
    tq = 2 * t
    return pl.pallas_call(
        _diff_kernel,
        grid=(N_DIFF_HEADS, s // tq),
        in_specs=[
            pl.BlockSpec(lam_params.shape, lambda h, i: (0, 0)),
            pl.BlockSpec((tq, LANES), lambda h, i: (i, h)),
            pl.BlockSpec((LANES, s), lambda h, i: (h, 0)),
            pl.BlockSpec((s, LANES), lambda h, i: (0, h)),
            pl.BlockSpec((1, 2, t, t), lambda h, i: (h, 0, 0, 0)),
            pl.BlockSpec(g.shape, lambda h, i: (0, 0)),
        ],
        out_specs=pl.BlockSpec((tq, LANES), lambda h, i: (i, h)),
        out_shape=jax.ShapeDtypeStruct((s, DIFF_WIDTH), BF16),
        scratch_shapes=[pltpu.VMEM((2, tq, 2 * LANES), F32), pltpu.VMEM((3, 8, LANES), F32),
                        pltpu.VMEM((2, tq, 2 * t), F32)]
        + [pltpu.VMEM((2, t, LANES), F32)] * 3 + [pltpu.SMEM((1,), jnp.int32)],
        compiler_params=pltpu.CompilerParams(
            dimension_semantics=("arbitrary", "arbitrary"), vmem_limit_bytes=DIFF_VMEM_LIMIT),
        name="diff_attn",
    )(lam_params, qd, kdt, vd, bias, g)


def _sb_kernel(q_ref, kt_ref, v_ref, g_ref, o_ref, carry_sc, acc_sc, knorm_sc, thresh_sm, done_sm):
    i = pl.program_id(1)
    t = SB_K_BLOCK
    n_sub = SB_Q_BLOCK // t
    lane = lax.broadcasted_iota(jnp.int32, (t, LANES), 1)
    head0 = lane < HEAD_DIM
    lower = (lax.broadcasted_iota(jnp.int32, (t, t), 0)
             >= lax.broadcasted_iota(jnp.int32, (t, t), 1)).astype(BF16)
    strict = lax.broadcasted_iota(jnp.int32, (t, t), 0) > lax.broadcasted_iota(jnp.int32, (t, t), 1)

    @pl.when(i == 0)
    def _():
        def norm_body(c, m):
            k = kt_ref[:, pl.ds(pl.multiple_of(c * t, t), t)].astype(F32)
            sq = k * k
            return jnp.maximum(m, jnp.maximum(jnp.sum(sq[:HEAD_DIM], axis=0, keepdims=True),
                                              jnp.sum(sq[HEAD_DIM:], axis=0, keepdims=True)))

        m = lax.fori_loop(0, kt_ref.shape[1] // t, norm_body, jnp.zeros((1, t), F32))
        knorm_sc[...] = jnp.broadcast_to(jnp.max(m, axis=1, keepdims=True), knorm_sc.shape)

    def load_q(u):
        q = q_ref[pl.ds(pl.multiple_of(u * t, t), t), :]
        zero_q = jnp.zeros_like(q)
        return q, (jnp.where(head0, q, zero_q), jnp.where(head0, zero_q, q))

    def block_pass(jb, consumers):
        start = pl.multiple_of(jb * t, t)
        kt = kt_ref[:, pl.ds(start, t)]
        v = v_ref[pl.ds(start, t), :]
        zero_v = jnp.zeros_like(v)
        v2 = jnp.concatenate([jnp.where(head0, v, zero_v), jnp.where(head0, zero_v, v)], axis=0)
        q_rows = [qm[hh] for _, qm, _ in consumers for hh in range(2)]
        z = jnp.dot(jnp.concatenate(q_rows, axis=0), kt, preferred_element_type=F32)
        nl = jnp.maximum(z, 0.0) + jnp.log(1.0 + jnp.exp2(-jnp.abs(z))) * LOG2E
        segments = [slice(n * t, (n + 1) * t) for n in range(len(q_rows))]
        masked = [m for _, _, m in consumers for _ in range(2)]
        nl = jnp.concatenate([jnp.where(strict, nl[seg], 0.0) if m else nl[seg]
                              for seg, m in zip(segments, masked)], axis=0)
        csum = jnp.dot(nl.astype(BF16), lower, preferred_element_type=F32)
        probs = []
        for n, (u, _, m) in enumerate(consumers):
            heads = []
            for hh in range(2):
                seg = segments[2 * n + hh]
                if m:
                    a = jnp.where(strict, jnp.exp2(z[seg] - csum[seg]), 0.0)
                    carry_sc[u, hh] = jnp.broadcast_to(csum[seg][:, 0:1], (t, LANES))
                else:
                    carry = carry_sc[u, hh]
                    a = jnp.exp2((z[seg] - _tile_lanes(carry, t // LANES)) - csum[seg])
                    carry_sc[u, hh] = carry + csum[seg][:, 0:1]
                heads.append(a.astype(BF16))
            probs.append(jnp.concatenate(heads, axis=1))
        pv = jnp.dot(jnp.concatenate(probs, axis=0), v2, preferred_element_type=F32)
        for n, (u, _, m) in enumerate(consumers):
            if m:
                acc_sc[u] = pv[n * t:(n + 1) * t]
            else:
                acc_sc[u] += pv[n * t:(n + 1) * t]

    qs = [load_q(u) for u in range(n_sub)]

    for u in range(n_sub):
        qsq = qs[u][0].astype(F32) * qs[u][0].astype(F32)
        qn0 = jnp.sum(jnp.where(head0, qsq, 0.0), axis=1, keepdims=True)
        qn1 = jnp.sum(jnp.where(head0, 0.0, qsq), axis=1, keepdims=True)
        qnorm = jnp.max(jnp.maximum(qn0, qn1), axis=0, keepdims=True)
        thresh = jnp.sqrt(qnorm * knorm_sc[0:1, 0:1]) * NORM_BOUND_SLACK + EXP2_UNDERFLOW
        thresh_sm[u] = jnp.max(thresh)

    def near_passes(has_previous_block):
        for u in range(n_sub - 1, -1, -1):
            consumers = [(u, qs[u][1], True)]
            if u + 1 < n_sub:
                consumers.append((u + 1, qs[u + 1][1], False))
            block_pass(i * n_sub + u, consumers)
        if has_previous_block:
            block_pass(i * n_sub - 1, [(0, qs[0][1], False)])

    @pl.when(i == 0)
    def _():
        near_passes(False)

    @pl.when(i >= 1)
    def _():
        near_passes(True)

    def carry_min(u):
        return jnp.min(jnp.minimum(carry_sc[u, 0], carry_sc[u, 1]))

    for u in range(n_sub):
        done_sm[u] = (carry_min(u) >= thresh_sm[u]).astype(jnp.int32)

    def remaining_blocks(u, c):
        _, qm = load_q(u)

        def far_cond(state):
            jb, all_zero = state
            return jnp.logical_and(jb >= 0, all_zero == 0)

        def far_body(state):
            jb, _ = state
            block_pass(jb, [(u, qm, False)])
            return jb - 1, (carry_min(u) >= thresh_sm[u]).astype(jnp.int32)

        lax.while_loop(far_cond, far_body, (i * n_sub + u - 2, done_sm[u]))
        return c

    lax.fori_loop(0, n_sub, remaining_blocks, 0)

    for u in range(n_sub):
        o = acc_sc[u]
        sq = o * o
        ss0 = jnp.sum(jnp.where(head0, sq, 0.0), axis=1, keepdims=True)
        ss1 = jnp.sum(jnp.where(head0, 0.0, sq), axis=1, keepdims=True)
        ms = jnp.where(head0, ss0, ss1) * (1.0 / HEAD_DIM)
        o_ref[u * t:(u + 1) * t, :] = (o * lax.rsqrt(ms + RMS_EPS) * g_ref[...]).astype(o_ref.dtype)


def _sb_attention(qs, kst, vs, g2):
    s = qs.shape[0]
    tq, t = SB_Q_BLOCK, SB_K_BLOCK
    return pl.pallas_call(
        _sb_kernel,
        grid=(N_SB_HEADS // 2, s // tq),
        in_specs=[
            pl.BlockSpec((tq, LANES), lambda h, i: (i, h)),
            pl.BlockSpec((LANES, s), lambda h, i: (h, 0)),
            pl.BlockSpec((s, LANES), lambda h, i: (0, h)),
            pl.BlockSpec(g2.shape, lambda h, i: (0, 0)),
        ],
        out_specs=pl.BlockSpec((tq, LANES), lambda h, i: (i, h)),
        out_shape=jax.ShapeDtypeStruct((s, SB_WIDTH), BF16),
        scratch_shapes=[pltpu.VMEM((tq // t, 2, t, LANES), F32), pltpu.VMEM((tq // t, t, LANES), F32),
                        pltpu.VMEM((8, LANES), F32),
                        pltpu.SMEM((tq // t,), F32), pltpu.SMEM((tq // t,), jnp.int32)],
        compiler_params=pltpu.CompilerParams(
            dimension_semantics=("arbitrary", "arbitrary"), vmem_limit_bytes=VMEM_LIMIT),
        name="sb_attn",
    )(qs, kst, vs, g2)


def _out_kernel(d_ref, s_ref, wd_ref, ws_ref, x_ref, gpost_ref, gpre_ref, x1_ref, h2_ref):
    mix = (jnp.dot(d_ref[...], wd_ref[...], preferred_element_type=F32)
           + jnp.dot(s_ref[...], ws_ref[...], preferred_element_type=F32))
    x1 = x_ref[...] + _rms(mix, gpost_ref[...])
    x1_ref[...] = x1
    h2_ref[...] = _rms(x1, gpre_ref[...]).astype(h2_ref.dtype)


def _out_proj(diff_o, sb_o, w_od, w_os, x2, g_post, g_pre):
    s = x2.shape[0]
    rows = OUT_ROWS
    row_spec = lambda w: pl.BlockSpec((rows, w), lambda i: (i, 0))
    full = lambda a: pl.BlockSpec(a.shape, lambda i: (0, 0))
    return pl.pallas_call(
        _out_kernel,
        grid=(s // rows,),
        in_specs=[row_spec(DIFF_WIDTH), row_spec(SB_WIDTH), full(w_od), full(w_os), row_spec(D_MODEL),
                  full(g_post), full(g_pre)],
        out_specs=[row_spec(D_MODEL), row_spec(D_MODEL)],
        out_shape=[jax.ShapeDtypeStruct((s, D_MODEL), F32), jax.ShapeDtypeStruct((s, D_MODEL), BF16)],
        compiler_params=pltpu.CompilerParams(
            dimension_semantics=("arbitrary",), vmem_limit_bytes=VMEM_LIMIT),
        name="out_proj",
    )(diff_o, sb_o, w_od, w_os, x2, g_post, g_pre)


def _gelu_tanh(x):
    return 0.5 * x * (1.0 + jnp.tanh(math.sqrt(2.0 / math.pi) * (x + 0.044715 * (x * x * x))))


def _ffn_kernel(h_ref, halo_ref, wup_ref, cw_ref, cb_ref, wd_ref, x1_ref, g_ref, o_ref, u_sc):
    i = pl.program_id(0)
    rows, chunk = FFN_ROWS, FFN_CHUNK
    h = h_ref[...]
    halo = jnp.where(i > 0, halo_ref[...], jnp.zeros_like(halo_ref[...]))

    def conv(col, slot):
        cols = slice(col, col + chunk)
        u_sc[slot, 0:HALO_ROWS, :] = jnp.dot(halo, wup_ref[:, cols], preferred_element_type=F32)
        u_sc[slot, HALO_ROWS:, :] = jnp.dot(h, wup_ref[:, cols], preferred_element_type=F32)
        y = cb_ref[:, cols]
        for k in range(CONV_WIDTH):
            off = HALO_ROWS - (CONV_WIDTH - 1) + k
            y = y + u_sc[slot, off:off + rows, :] * cw_ref[k:k + 1, cols]
        return y

    n_chunk = D_FF // chunk
    gates = [conv(c * chunk, 2 * c) for c in range(n_chunk)]
    vals = [conv(D_FF + c * chunk, 2 * c + 1) for c in range(n_chunk)]
    y = None
    for c in range(n_chunk):
        act = (_gelu_tanh(gates[c]) * vals[c]).astype(BF16)
        part = jnp.dot(act, wd_ref[c * chunk:(c + 1) * chunk, :], preferred_element_type=F32)
        y = part if y is None else y + part
    o_ref[...] = x1_ref[...] + _rms(y, g_ref[...])


def _ffn(h2, w_up, conv_w, conv_b, w_down, x1, g_post):
    s = h2.shape[0]
    rows = FFN_ROWS
    halo_blocks = rows // HALO_ROWS
    row_spec = pl.BlockSpec((rows, D_MODEL), lambda i: (i, 0))
    halo_spec = pl.BlockSpec((HALO_ROWS, D_MODEL), lambda i: (jnp.maximum(i * halo_blocks - 1, 0), 0))
    resident = lambda a: pl.BlockSpec(a.shape, lambda i: (0, 0), pipeline_mode=pl.Buffered(1))
    return pl.pallas_call(
        _ffn_kernel,
        grid=(s // rows,),
        in_specs=[row_spec, halo_spec, resident(w_up), resident(conv_w), resident(conv_b),
                  resident(w_down), row_spec, resident(g_post)],
        out_specs=row_spec,
        out_shape=jax.ShapeDtypeStruct((s, D_MODEL), F32),
        scratch_shapes=[pltpu.VMEM((2 * (D_FF // FFN_CHUNK), HALO_ROWS + rows, FFN_CHUNK), F32)],
        compiler_params=pltpu.CompilerParams(
            dimension_semantics=("arbitrary",), vmem_limit_bytes=FFN_VMEM_LIMIT),
        name="ffn",
    )(h2, h2, w_up, conv_w, conv_b, w_down, x1, g_post)


def kernel(x, attn_pre_norm, w_qkv, lambda_q1, lambda_k1, lambda_q2, lambda_k2, diff_subln, sb_norm,
           rel_bias, w_o, attn_post_norm, ffn_pre_norm, w_up, conv_w, conv_b, w_down, ffn_post_norm):
    b, s, _ = x.shape
    assert b == 1 and attn_pre_norm.shape[0] == 1, "single sequence, single layer"
    assert s % max(PROJ_ROWS, 2 * DIFF_BLOCK, SB_Q_BLOCK, OUT_ROWS, FFN_ROWS) == 0
    x2 = x[0]
    row = lambda a: a.reshape(1, -1).astype(F32)

    qd, vd, qs, vs, kdt, kst = _proj(x2, row(attn_pre_norm), w_qkv[0].astype(BF16))

    bias = _bias_tiles(rel_bias.astype(F32))
    lam_params = jnp.concatenate([lambda_q1, lambda_k1, lambda_q2, lambda_k2], axis=0).astype(F32)
    diff_o = _diff_attention(lam_params, qd, kdt, vd, bias, row(diff_subln))
    sb_o = _sb_attention(qs, kst, vs, row(jnp.concatenate([sb_norm[0], sb_norm[0]])))

    wo = w_o[0].astype(BF16)
    x1, h2 = _out_proj(diff_o, sb_o, wo[:DIFF_WIDTH], wo[DIFF_WIDTH:], x2,
                       row(attn_post_norm), row(ffn_pre_norm))
    out = _ffn(h2, w_up[0].astype(BF16), conv_w[0].astype(F32), conv_b.astype(F32),
               w_down[0].astype(BF16), x1, row(ffn_post_norm))
    return out[None]
```

```python
import math

import jax
import jax.numpy as jnp
from jax import lax
from jax.experimental import pallas as pl
from jax.experimental.pallas import tpu as pltpu

F32 = jnp.float32
BF16 = jnp.bfloat16

D_MODEL = 1024
HEAD_DIM = 64
N_DIFF_HEADS = 4
N_SB_HEADS = 8
DIFF_WIDTH = N_DIFF_HEADS * 2 * HEAD_DIM
SB_WIDTH = N_SB_HEADS * HEAD_DIM
QKV_GROUP = DIFF_WIDTH
assert SB_WIDTH == QKV_GROUP
D_FF = 2816
CONV_WIDTH = 3
N_BUCKETS = 32
MAX_DISTANCE = 128
RMS_EPS = 1e-6
LAMBDA_INIT = 0.8 - 0.6 * math.exp(-0.3 * 0)

LOG2E = 1.4426950408889634
Q_SCALE = HEAD_DIM ** -0.5 * LOG2E
MASK_VALUE = -1e30
EXP2_UNDERFLOW = 160.0
NORM_BOUND_SLACK = 1.0 + 2.0 ** -10
SOFTMAX_MIN_DENOMINATOR = 2.0 ** -60

LANES = 128
PROJ_ROWS = 1024
DIFF_BLOCK = 512
SB_Q_BLOCK = 2048
SB_K_BLOCK = 256
OUT_ROWS = 1024
FFN_ROWS = 512
FFN_CHUNK = D_FF // 2
HALO_ROWS = 16
VMEM_LIMIT = 48 * 1024 * 1024
DIFF_VMEM_LIMIT = 58 * 1024 * 1024
FFN_VMEM_LIMIT = 58 * 1024 * 1024


def _tile_lanes(x, k):
    return jnp.concatenate([x] * k, axis=1) if k > 1 else x


def _rms(x, g):
    return x * lax.rsqrt(jnp.mean(x * x, axis=-1, keepdims=True) + RMS_EPS) * g


def _proj_kernel(x_ref, g_ref, w_ref, qd_ref, vd_ref, qs_ref, vs_ref, kdt_ref, kst_ref):
    hb = _rms(x_ref[...], g_ref[...]).astype(BF16)
    proj = lambda c: jnp.dot(hb, w_ref[:, c * QKV_GROUP:(c + 1) * QKV_GROUP], preferred_element_type=F32)
    qd_ref[...] = (proj(0) * Q_SCALE).astype(BF16)
    vd_ref[...] = proj(2).astype(BF16)
    qs_ref[...] = (proj(3) * Q_SCALE).astype(BF16)
    vs_ref[...] = proj(5).astype(BF16)
    proj_t = lambda c: lax.dot_general(w_ref[:, c * QKV_GROUP:(c + 1) * QKV_GROUP], hb, (((0,), (1,)), ((), ())),
                                       preferred_element_type=F32)
    kdt_ref[...] = proj_t(1).astype(BF16)
    kst_ref[...] = proj_t(4).astype(BF16)


def _proj(x2, g, w):
    s = x2.shape[0]
    rows = PROJ_ROWS
    row_spec = lambda w: pl.BlockSpec((rows, w), lambda i: (i, 0))
    full = lambda a: pl.BlockSpec(a.shape, lambda i: (0, 0))
    out_rows = jax.ShapeDtypeStruct((s, QKV_GROUP), BF16)
    out_t = jax.ShapeDtypeStruct((QKV_GROUP, s), BF16)
    return pl.pallas_call(
        _proj_kernel,
        grid=(s // rows,),
        in_specs=[row_spec(D_MODEL), full(g), full(w)],
        out_specs=[row_spec(QKV_GROUP)] * 4 + [pl.BlockSpec((QKV_GROUP, rows), lambda i: (0, i))] * 2,
        out_shape=[out_rows] * 4 + [out_t] * 2,
        compiler_params=pltpu.CompilerParams(
            dimension_semantics=("arbitrary",), vmem_limit_bytes=VMEM_LIMIT),
        name="proj",
    )(x2, g, w)


def _bias_kernel(tab_ref, o_ref):
    h = pl.program_id(0)
    w = pl.program_id(1)
    t, sub = DIFF_BLOCK, LANES

    def sub_tile(offset):
        rel = (lax.broadcasted_iota(jnp.int32, (sub, sub), 0)
               - lax.broadcasted_iota(jnp.int32, (sub, sub), 1) + offset)
        n = jnp.maximum(rel, 0)
        max_exact = N_BUCKETS // 2
        nf = jnp.maximum(n, 1).astype(F32)
        large = max_exact + jnp.floor(jnp.log(nf / max_exact) / math.log(MAX_DISTANCE / max_exact)
                                      * (N_BUCKETS - max_exact)).astype(jnp.int32)
        large = jnp.minimum(large, N_BUCKETS - 1)
        bucket = jnp.where(n < max_exact, n, large)
        b = jnp.zeros((sub, sub), F32)
        for k in range(N_BUCKETS):
            b = jnp.where(bucket == k, tab_ref[k, h], b)
        b = (b - tab_ref[N_BUCKETS - 1, h]) * LOG2E
        return jnp.where(rel >= 0, b, MASK_VALUE)

    def tile(first_offset):
        for br in range(t // sub):
            for bc in range(t // sub):
                offset = first_offset + (br - bc) * sub
                if offset + sub <= 0:
                    val = jnp.full((sub, sub), MASK_VALUE, F32)
                elif offset - sub >= MAX_DISTANCE:
                    val = jnp.zeros((sub, sub), F32)
                else:
                    val = sub_tile(offset)
                o_ref[0, 0, br * sub:(br + 1) * sub, bc * sub:(bc + 1) * sub] = val

    @pl.when(w == 0)
    def _():
        tile(0)

    @pl.when(w == 1)
    def _():
        tile(t)


def _bias_tiles(rel_bias):
    t = DIFF_BLOCK
    return pl.pallas_call(
        _bias_kernel,
        grid=(N_DIFF_HEADS, 2),
        in_specs=[pl.BlockSpec(memory_space=pltpu.SMEM)],
        out_specs=pl.BlockSpec((1, 1, t, t), lambda h, w: (h, w, 0, 0)),
        out_shape=jax.ShapeDtypeStruct((N_DIFF_HEADS, 2, t, t), F32),
        compiler_params=pltpu.CompilerParams(
            dimension_semantics=("arbitrary", "arbitrary"), vmem_limit_bytes=VMEM_LIMIT),
        name="bias",
    )(rel_bias)


def _diff_kernel(lam_ref, q_ref, kt_ref, v_ref, bias_ref, g_ref, o_ref,
                 acc_sc, stat_sc, s_sc, m_sc, l_sc, slow_acc_sc, ok_sm):
    i = pl.program_id(1)
    t = DIFF_BLOCK
    tq = 2 * t
    half_a, half_b = slice(0, t), slice(t, tq)
    lane = lax.broadcasted_iota(jnp.int32, (tq, LANES), 1)
    q = q_ref[...]
    n_pairs = jnp.maximum(i - 1, 0)

    @pl.when(i == 0)
    def _():
        def norm_body(c, ms):
            k = kt_ref[:, pl.ds(pl.multiple_of(c * t, t), t)].astype(F32)
            sq = k * k
            return (jnp.maximum(ms[0], jnp.sum(sq[:HEAD_DIM], axis=0, keepdims=True)),
                    jnp.maximum(ms[1], jnp.sum(sq[HEAD_DIM:], axis=0, keepdims=True)))

        zeros = jnp.zeros((1, t), F32)
        ms = lax.fori_loop(0, kt_ref.shape[1] // t, norm_body, (zeros, zeros))
        for mp in range(2):
            stat_sc[mp] = jnp.broadcast_to(jnp.max(ms[mp], axis=1, keepdims=True), stat_sc.shape[1:])
        bias_max = jnp.max(jnp.max(bias_ref[0, 0], axis=1, keepdims=True), axis=0, keepdims=True)
        stat_sc[2] = jnp.broadcast_to(bias_max, stat_sc.shape[1:])

    def write_out(rows, o0, o1):
        lp = lam_ref[...]
        lam = (jnp.exp(jnp.sum(lp[0:1] * lp[1:2], axis=1, keepdims=True))
               - jnp.exp(jnp.sum(lp[2:3] * lp[3:4], axis=1, keepdims=True)) + LAMBDA_INIT)
        o_ref[rows, :] = (_rms(o0 - lam * o1, g_ref[...]) * (1.0 - LAMBDA_INIT)).astype(o_ref.dtype)

    every = slice(0, tq)
    pair = 2 * t
    zero_q = jnp.zeros_like(q)
    qm = (jnp.where(lane < HEAD_DIM, q, zero_q), jnp.where(lane >= HEAD_DIM, q, zero_q))
    bias_diag, bias_prev = bias_ref[0, 0], bias_ref[0, 1]

    def fast_path(first_block, far_pairs):
        qsq = q.astype(F32) * q.astype(F32)
        bound = []
        for mp in range(2):
            own = (lane < HEAD_DIM) if mp == 0 else (lane >= HEAD_DIM)
            qn = jnp.sum(jnp.where(own, qsq, 0.0), axis=1, keepdims=True)
            b = jnp.sqrt(qn * stat_sc[mp][0:1, 0:1]) * NORM_BOUND_SLACK + stat_sc[2][0:1, 0:1]
            bound.append(jnp.broadcast_to(b, (tq, LANES)))

        acc_sc[...] = jnp.zeros(acc_sc.shape, F32)

        def scores(rows, start, width):
            kt = kt_ref[:, pl.ds(start, width)]
            return [jnp.dot(qm[mp][rows], kt, preferred_element_type=F32)
                    - _tile_lanes(bound[mp][rows], width // LANES) for mp in range(2)]

        def accumulate(rows, s, start, width):
            v_aug = jnp.concatenate([v_ref[pl.ds(start, width), :], jnp.ones((width, LANES), BF16)],
                                    axis=1)
            for mp in range(2):
                acc_sc[mp, rows, :] += jnp.dot(jnp.exp2(s[mp]).astype(BF16), v_aug,
                                               preferred_element_type=F32)

        def last_block_of_b(block):
            start = pl.multiple_of(block * t, t)
            s = [x + bias_diag for x in scores(half_b, start, t)]
            accumulate(half_b, s, start, t)

        if far_pairs is not None:
            first = scores(every, 0, pair)
            for mp in range(2):
                s_sc[mp] = first[mp]

            def far_body(r, carry):
                cur = [s_sc[mp] for mp in range(2)]
                nxt = scores(every, pl.multiple_of((r + 1) * pair, pair), pair)
                accumulate(every, cur, pl.multiple_of(r * pair, pair), pair)
                for mp in range(2):
                    s_sc[mp] = nxt[mp]
                return carry

            lax.fori_loop(0, far_pairs - 1, far_body, 0)
            accumulate(every, [s_sc[mp] for mp in range(2)],
                       pl.multiple_of((far_pairs - 1) * pair, pair), pair)

        if first_block:
            s = [x + jnp.concatenate([bias_diag, bias_prev], axis=0) for x in scores(every, 0, t)]
            accumulate(every, s, 0, t)
            last_block_of_b(1)
        else:
            start = pl.multiple_of((2 * i - 2) * t, t)
            s = scores(every, start, 3 * t)
            for mp in range(2):
                x = s[mp]
                s[mp] = jnp.concatenate([
                    x[:, :t],
                    jnp.concatenate([x[half_a, t:2 * t] + bias_prev, x[half_b, t:2 * t]], axis=0),
                    jnp.concatenate([x[half_a, 2 * t:] + bias_diag, x[half_b, 2 * t:] + bias_prev], axis=0),
                ], axis=1)
            accumulate(every, s, start, 3 * t)
            last_block_of_b(2 * i + 1)

        l0, l1 = acc_sc[0][:, LANES:], acc_sc[1][:, LANES:]
        write_out(every, acc_sc[0][:, :LANES] / l0, acc_sc[1][:, :LANES] / l1)
        ok_sm[0] = (jnp.min(jnp.minimum(l0, l1)) >= SOFTMAX_MIN_DENOMINATOR).astype(jnp.int32)

    @pl.when(i == 0)
    def _():
        fast_path(True, None)

    @pl.when(i == 1)
    def _():
        fast_path(False, None)

    @pl.when(i >= 2)
    def _():
        fast_path(False, n_pairs)

    @pl.when(ok_sm[0] == 0)
    def _():
        def online_softmax_half(rows, diag_block):
            m_sc[...] = jnp.full(m_sc.shape, MASK_VALUE, F32)
            l_sc[...] = jnp.zeros(l_sc.shape, F32)
            slow_acc_sc[...] = jnp.zeros(slow_acc_sc.shape, F32)

            def step(j, bias):
                start = pl.multiple_of(j * t, t)
                kt = kt_ref[:, pl.ds(start, t)]
                v = v_ref[pl.ds(start, t), :]
                for mp in range(2):
                    s = jnp.dot(qm[mp][rows], kt, preferred_element_type=F32)
                    if bias is not None:
                        s = s + bias
                    m_prev = m_sc[mp]
                    m_new = jnp.maximum(m_prev, jnp.max(s, axis=1, keepdims=True))
                    alpha = jnp.exp2(m_prev - m_new)
                    p = jnp.exp2(s - _tile_lanes(m_new, t // LANES))
                    l_sc[mp] = alpha * l_sc[mp] + jnp.sum(p, axis=1, keepdims=True)
                    slow_acc_sc[mp] = (alpha * slow_acc_sc[mp]
                                       + jnp.dot(p.astype(BF16), v, preferred_element_type=F32))
                    m_sc[mp] = m_new

            def slow_body(j, carry):
                step(j, None)
                return carry

            lax.fori_loop(0, jnp.maximum(diag_block - 1, 0), slow_body, 0)

            @pl.when(diag_block >= 1)
            def _():
                step(diag_block - 1, bias_prev)

            step(diag_block, bias_diag)
            write_out(rows, slow_acc_sc[0] / l_sc[0], slow_acc_sc[1] / l_sc[1])

        online_softmax_half(half_a, 2 * i)
        online_softmax_half(half_b, 2 * i + 1)


def _diff_attention(lam_params, qd, kdt, vd, bias, g):
    s = qd.shape[0]
    t = DIFF_BLOCK
    tq = 2 * t
    return pl.pallas_call(
        _diff_kernel,
        grid=(N_DIFF_HEADS, s // tq),
        in_specs=[
            pl.BlockSpec(lam_params.shape, lambda h, i: (0, 0)),
            pl.BlockSpec((tq, LANES), lambda h, i: (i, h)),
            pl.BlockSpec((LANES, s), lambda h, i: (h, 0)),
            pl.BlockSpec((s, LANES), lambda h, i: (0, h)),
            pl.BlockSpec((1, 2, t, t), lambda h, i: (h, 0, 0, 0)),
            pl.BlockSpec(g.shape, lambda h, i: (0, 0)),
        ],
        out_specs=pl.BlockSpec((tq, LANES), lambda h, i: (i, h)),
        out_shape=jax.ShapeDtypeStruct((s, DIFF_WIDTH), BF16),
        scratch_shapes=[pltpu.VMEM((2, tq, 2 * LANES), F32), pltpu.VMEM((3, 8, LANES), F32),
                        pltpu.VMEM((2, tq, 2 * t), F32)]
        + [pltpu.VMEM((2, t, LANES), F32)] * 3 + [pltpu.SMEM((1,), jnp.int32)],
        compiler_params=pltpu.CompilerParams(
            dimension_semantics=("arbitrary", "arbitrary"), vmem_limit_bytes=DIFF_VMEM_LIMIT),
        name="diff_attn",
    )(lam_params, qd, kdt, vd, bias, g)


def _sb_kernel(q_ref, kt_ref, v_ref, g_ref, o_ref, carry_sc, acc_sc, knorm_sc, thresh_sm, done_sm):
    i = pl.program_id(1)
    t = SB_K_BLOCK
    n_sub = SB_Q_BLOCK // t
    lane = lax.broadcasted_iota(jnp.int32, (t, LANES), 1)
    head0 = lane < HEAD_DIM
    lower = (lax.broadcasted_iota(jnp.int32, (t, t), 0)
             >= lax.broadcasted_iota(jnp.int32, (t, t), 1)).astype(BF16)
    strict = lax.broadcasted_iota(jnp.int32, (t, t), 0) > lax.broadcasted_iota(jnp.int32, (t, t), 1)

    @pl.when(i == 0)
    def _():
        def norm_body(c, m):
            k = kt_ref[:, pl.ds(pl.multiple_of(c * t, t), t)].astype(F32)
            sq = k * k
            return jnp.maximum(m, jnp.maximum(jnp.sum(sq[:HEAD_DIM], axis=0, keepdims=True),
                                              jnp.sum(sq[HEAD_DIM:], axis=0, keepdims=True)))

        m = lax.fori_loop(0, kt_ref.shape[1] // t, norm_body, jnp.zeros((1, t), F32))
        knorm_sc[...] = jnp.broadcast_to(jnp.max(m, axis=1, keepdims=True), knorm_sc.shape)

    def load_q(u):
        q = q_ref[pl.ds(pl.multiple_of(u * t, t), t), :]
        zero_q = jnp.zeros_like(q)
        return q, (jnp.where(head0, q, zero_q), jnp.where(head0, zero_q, q))

    def block_pass(jb, consumers):
        start = pl.multiple_of(jb * t, t)
        kt = kt_ref[:, pl.ds(start, t)]
        v = v_ref[pl.ds(start, t), :]
        zero_v = jnp.zeros_like(v)
        v2 = jnp.concatenate([jnp.where(head0, v, zero_v), jnp.where(head0, zero_v, v)], axis=0)
        q_rows = [qm[hh] for _, qm, _ in consumers for hh in range(2)]
        z = jnp.dot(jnp.concatenate(q_rows, axis=0), kt, preferred_element_type=F32)
        nl = jnp.maximum(z, 0.0) + jnp.log(1.0 + jnp.exp2(-jnp.abs(z))) * LOG2E
        segments = [slice(n * t, (n + 1) * t) for n in range(len(q_rows))]
        masked = [m for _, _, m in consumers for _ in range(2)]
        nl = jnp.concatenate([jnp.where(strict, nl[seg], 0.0) if m else nl[seg]
                              for seg, m in zip(segments, masked)], axis=0)
        csum = jnp.dot(nl.astype(BF16), lower, preferred_element_type=F32)
        probs = []
        for n, (u, _, m) in enumerate(consumers):
            heads = []
            for hh in range(2):
                seg = segments[2 * n + hh]
                if m:
                    a = jnp.where(strict, jnp.exp2(z[seg] - csum[seg]), 0.0)
                    carry_sc[u, hh] = jnp.broadcast_to(csum[seg][:, 0:1], (t, LANES))
                else:
                    carry = carry_sc[u, hh]
                    a = jnp.exp2((z[seg] - _tile_lanes(carry, t // LANES)) - csum[seg])
                    carry_sc[u, hh] = carry + csum[seg][:, 0:1]
                heads.append(a.astype(BF16))
            probs.append(jnp.concatenate(heads, axis=1))
        pv = jnp.dot(jnp.concatenate(probs, axis=0), v2, preferred_element_type=F32)
        for n, (u, _, m) in enumerate(consumers):
            if m:
                acc_sc[u] = pv[n * t:(n + 1) * t]
            else:
                acc_sc[u] += pv[n * t:(n + 1) * t]

    qs = [load_q(u) for u in range(n_sub)]

    for u in range(n_sub):
        qsq = qs[u][0].astype(F32) * qs[u][0].astype(F32)
        qn0 = jnp.sum(jnp.where(head0, qsq, 0.0), axis=1, keepdims=True)
        qn1 = jnp.sum(jnp.where(head0, 0.0, qsq), axis=1, keepdims=True)
        qnorm = jnp.max(jnp.maximum(qn0, qn1), axis=0, keepdims=True)
        thresh = jnp.sqrt(qnorm * knorm_sc[0:1, 0:1]) * NORM_BOUND_SLACK + EXP2_UNDERFLOW
        thresh_sm[u] = jnp.max(thresh)

    def near_passes(has_previous_block):
        for u in range(n_sub - 1, -1, -1):
            consumers = [(u, qs[u][1], True)]
            if u + 1 < n_sub:
                consumers.append((u + 1, qs[u + 1][1], False))
            block_pass(i * n_sub + u, consumers)
        if has_previous_block:
            block_pass(i * n_sub - 1, [(0, qs[0][1], False)])

    @pl.when(i == 0)
    def _():
        near_passes(False)

    @pl.when(i >= 1)
    def _():
        near_passes(True)

    def carry_min(u):
        return jnp.min(jnp.minimum(carry_sc[u, 0], carry_sc[u, 1]))

    for u in range(n_sub):
        done_sm[u] = (carry_min(u) >= thresh_sm[u]).astype(jnp.int32)

    def remaining_blocks(u, c):
        _, qm = load_q(u)

        def far_cond(state):
            jb, all_zero = state
            return jnp.logical_and(jb >= 0, all_zero == 0)

        def far_body(state):
            jb, _ = state
            block_pass(jb, [(u, qm, False)])
            return jb - 1, (carry_min(u) >= thresh_sm[u]).astype(jnp.int32)

        lax.while_loop(far_cond, far_body, (i * n_sub + u - 2, done_sm[u]))
        return c

    lax.fori_loop(0, n_sub, remaining_blocks, 0)

    for u in range(n_sub):
        o = acc_sc[u]
        sq = o * o
        ss0 = jnp.sum(jnp.where(head0, sq, 0.0), axis=1, keepdims=True)
        ss1 = jnp.sum(jnp.where(head0, 0.0, sq), axis=1, keepdims=True)
        ms = jnp.where(head0, ss0, ss1) * (1.0 / HEAD_DIM)
        o_ref[u * t:(u + 1) * t, :] = (o * lax.rsqrt(ms + RMS_EPS) * g_ref[...]).astype(o_ref.dtype)


def _sb_attention(qs, kst, vs, g2):
    s = qs.shape[0]
    tq, t = SB_Q_BLOCK, SB_K_BLOCK
    return pl.pallas_call(
        _sb_kernel,
        grid=(N_SB_HEADS // 2, s // tq),
        in_specs=[
            pl.BlockSpec((tq, LANES), lambda h, i: (i, h)),
            pl.BlockSpec((LANES, s), lambda h, i: (h, 0)),
            pl.BlockSpec((s, LANES), lambda h, i: (0, h)),
            pl.BlockSpec(g2.shape, lambda h, i: (0, 0)),
        ],
        out_specs=pl.BlockSpec((tq, LANES), lambda h, i: (i, h)),
        out_shape=jax.ShapeDtypeStruct((s, SB_WIDTH), BF16),
        scratch_shapes=[pltpu.VMEM((tq // t, 2, t, LANES), F32), pltpu.VMEM((tq // t, t, LANES), F32),
                        pltpu.VMEM((8, LANES), F32),
                        pltpu.SMEM((tq // t,), F32), pltpu.SMEM((tq // t,), jnp.int32)],
        compiler_params=pltpu.CompilerParams(
            dimension_semantics=("arbitrary", "arbitrary"), vmem_limit_bytes=VMEM_LIMIT),
        name="sb_attn",
    )(qs, kst, vs, g2)


def _out_kernel(d_ref, s_ref, wd_ref, ws_ref, x_ref, gpost_ref, gpre_ref, x1_ref, h2_ref):
    mix = (jnp.dot(d_ref[...], wd_ref[...], preferred_element_type=F32)
           + jnp.dot(s_ref[...], ws_ref[...], preferred_element_type=F32))
    x1 = x_ref[...] + _rms(mix, gpost_ref[...])
    x1_ref[...] = x1
    h2_ref[...] = _rms(x1, gpre_ref[...]).astype(h2_ref.dtype)


def _out_proj(diff_o, sb_o, w_od, w_os, x2, g_post, g_pre):
    s = x2.shape[0]
    rows = OUT_ROWS
    row_spec = lambda w: pl.BlockSpec((rows, w), lambda i: (i, 0))
    full = lambda a: pl.BlockSpec(a.shape, lambda i: (0, 0))
    return pl.pallas_call(
        _out_kernel,
        grid=(s // rows,),
        in_specs=[row_spec(DIFF_WIDTH), row_spec(SB_WIDTH), full(w_od), full(w_os), row_spec(D_MODEL),
                  full(g_post), full(g_pre)],
        out_specs=[row_spec(D_MODEL), row_spec(D_MODEL)],
        out_shape=[jax.ShapeDtypeStruct((s, D_MODEL), F32), jax.ShapeDtypeStruct((s, D_MODEL), BF16)],
        compiler_params=pltpu.CompilerParams(
            dimension_semantics=("arbitrary",), vmem_limit_bytes=VMEM_LIMIT),
        name="out_proj",
    )(diff_o, sb_o, w_od, w_os, x2, g_post, g_pre)


def _gelu_tanh(x):
    return 0.5 * x * (1.0 + jnp.tanh(math.sqrt(2.0 / math.pi) * (x + 0.044715 * (x * x * x))))


def _ffn_kernel(h_ref, halo_ref, wup_ref, cw_ref, cb_ref, wd_ref, x1_ref, g_ref, o_ref, u_sc):
    i = pl.program_id(0)
    rows, chunk = FFN_ROWS, FFN_CHUNK
    halo = jnp.where(i > 0, halo_ref[...], jnp.zeros_like(halo_ref[...]))
    lhs = jnp.concatenate([h_ref[...], halo], axis=0)

    def conv(col, slot):
        cols = slice(col, col + chunk)
        u = jnp.dot(lhs, wup_ref[:, cols], preferred_element_type=F32)
        u_sc[slot, 0:HALO_ROWS, :] = u[rows:]
        u_sc[slot, HALO_ROWS:, :] = u[:rows]
        y = cb_ref[:, cols]
        for k in range(CONV_WIDTH):
            off = HALO_ROWS - (CONV_WIDTH - 1) + k
            y = y + u_sc[slot, off:off + rows, :] * cw_ref[k:k + 1, cols]
        return y

    n_chunk = D_FF // chunk
    gates = [conv(c * chunk, 2 * c) for c in range(n_chunk)]
    vals = [conv(D_FF + c * chunk, 2 * c + 1) for c in range(n_chunk)]
    y = None
    for c in range(n_chunk):
        act = (_gelu_tanh(gates[c]) * vals[c]).astype(BF16)
        part = jnp.dot(act, wd_ref[c * chunk:(c + 1) * chunk, :], preferred_element_type=F32)
        y = part if y is None else y + part
    o_ref[...] = x1_ref[...] + _rms(y, g_ref[...])


def _ffn(h2, w_up, conv_w, conv_b, w_down, x1, g_post):
    s = h2.shape[0]
    rows = FFN_ROWS
    halo_blocks = rows // HALO_ROWS
    row_spec = pl.BlockSpec((rows, D_MODEL), lambda i: (i, 0))
    halo_spec = pl.BlockSpec((HALO_ROWS, D_MODEL), lambda i: (jnp.maximum(i * halo_blocks - 1, 0), 0))
    resident = lambda a: pl.BlockSpec(a.shape, lambda i: (0, 0), pipeline_mode=pl.Buffered(1))
    return pl.pallas_call(
        _ffn_kernel,
        grid=(s // rows,),
        in_specs=[row_spec, halo_spec, resident(w_up), resident(conv_w), resident(conv_b),
                  resident(w_down), row_spec, resident(g_post)],
        out_specs=row_spec,
        out_shape=jax.ShapeDtypeStruct((s, D_MODEL), F32),
        scratch_shapes=[pltpu.VMEM((2 * (D_FF // FFN_CHUNK), HALO_ROWS + rows, FFN_CHUNK), F32)],
        compiler_params=pltpu.CompilerParams(
            dimension_semantics=("arbitrary",), vmem_limit_bytes=FFN_VMEM_LIMIT),
        name="ffn",
    )(h2, h2, w_up, conv_w, conv_b, w_down, x1, g_post)


def kernel(x, attn_pre_norm, w_qkv, lambda_q1, lambda_k1, lambda_q2, lambda_k2, diff_subln, sb_norm,
           rel_bias, w_o, attn_post_norm, ffn_pre_norm, w_up, conv_w, conv_b, w_down, ffn_post_norm):
    b, s, _ = x.shape
    assert b == 1 and attn_pre_norm.shape[0] == 1, "single sequence, single layer"
    assert s % max(PROJ_ROWS, 2 * DIFF_BLOCK, SB_Q_BLOCK, OUT_ROWS, FFN_ROWS) == 0
    x2 = x[0]
    row = lambda a: a.reshape(1, -1).astype(F32)

    qd, vd, qs, vs, kdt, kst = _proj(x2, row(attn_pre_norm), w_qkv[0].astype(BF16))

    bias = _bias_tiles(rel_bias.astype(F32))
    lam_params = jnp.concatenate([lambda_q1, lambda_k1, lambda_q2, lambda_k2], axis=0).astype(F32)
    diff_o = _diff_attention(lam_params, qd, kdt, vd, bias, row(diff_subln))
    sb_o = _sb_attention(qs, kst, vs, row(jnp.concatenate([sb_norm[0], sb_norm[0]])))

    wo = w_o[0].astype(BF16)
    x1, h2 = _out_proj(diff_o, sb_o, wo[:DIFF_WIDTH], wo[DIFF_WIDTH:], x2,
                       row(attn_post_norm), row(ffn_pre_norm))
    out = _ffn(h2, w_up[0].astype(BF16), conv_w[0].astype(F32), conv_b.astype(F32),
               w_down[0].astype(BF16), x1, row(ffn_post_norm))
    return out[None]
```

```python
import math

import jax
import jax.numpy as jnp
from jax import lax
from jax.experimental import pallas as pl
from jax.experimental.pallas import tpu as pltpu

F32 = jnp.float32
BF16 = jnp.bfloat16

D_MODEL = 1024
HEAD_DIM = 64
N_DIFF_HEADS = 4
N_SB_HEADS = 8
DIFF_WIDTH = N_DIFF_HEADS * 2 * HEAD_DIM
SB_WIDTH = N_SB_HEADS * HEAD_DIM
QKV_GROUP = DIFF_WIDTH
assert SB_WIDTH == QKV_GROUP
D_FF = 2816
CONV_WIDTH = 3
N_BUCKETS = 32
MAX_DISTANCE = 128
RMS_EPS = 1e-6
LAMBDA_INIT = 0.8 - 0.6 * math.exp(-0.3 * 0)

LOG2E = 1.4426950408889634
Q_SCALE = HEAD_DIM ** -0.5 * LOG2E
MASK_VALUE = -1e30
EXP2_UNDERFLOW = 160.0
NORM_BOUND_SLACK = 1.0 + 2.0 ** -10
SOFTMAX_MIN_DENOMINATOR = 2.0 ** -60

LANES = 128
PROJ_ROWS = 1024
DIFF_BLOCK = 512
SB_Q_BLOCK = 4096
SB_K_BLOCK = 256
MLP_ROWS = 512
FFN_CHUNK = D_FF // 2
HALO_ROWS = 16
VMEM_LIMIT = 48 * 1024 * 1024
DIFF_VMEM_LIMIT = 58 * 1024 * 1024
MLP_VMEM_LIMIT = 58 * 1024 * 1024


def _tile_lanes(x, k):
    return jnp.concatenate([x] * k, axis=1) if k > 1 else x


def _rms(x, g):
    return x * lax.rsqrt(jnp.mean(x * x, axis=-1, keepdims=True) + RMS_EPS) * g


def _proj_kernel(x_ref, g_ref, w_ref, qd_ref, vd_ref, qs_ref, vs_ref, kdt_ref, kst_ref):
    hb = _rms(x_ref[...], g_ref[...]).astype(BF16)
    proj = lambda c: jnp.dot(hb, w_ref[:, c * QKV_GROUP:(c + 1) * QKV_GROUP], preferred_element_type=F32)
    qd_ref[...] = (proj(0) * Q_SCALE).astype(BF16)
    vd_ref[...] = proj(2).astype(BF16)
    qs_ref[...] = (proj(3) * Q_SCALE).astype(BF16)
    vs_ref[...] = proj(5).astype(BF16)
    proj_t = lambda c: lax.dot_general(w_ref[:, c * QKV_GROUP:(c + 1) * QKV_GROUP], hb, (((0,), (1,)), ((), ())),
                                       preferred_element_type=F32)
    kdt_ref[...] = proj_t(1).astype(BF16)
    kst_ref[...] = proj_t(4).astype(BF16)


def _proj(x2, g, w):
    s = x2.shape[0]
    rows = PROJ_ROWS
    row_spec = lambda w: pl.BlockSpec((rows, w), lambda i: (i, 0))
    full = lambda a: pl.BlockSpec(a.shape, lambda i: (0, 0))
    out_rows = jax.ShapeDtypeStruct((s, QKV_GROUP), BF16)
    out_t = jax.ShapeDtypeStruct((QKV_GROUP, s), BF16)
    return pl.pallas_call(
        _proj_kernel,
        grid=(s // rows,),
        in_specs=[row_spec(D_MODEL), full(g), full(w)],
        out_specs=[row_spec(QKV_GROUP)] * 4 + [pl.BlockSpec((QKV_GROUP, rows), lambda i: (0, i))] * 2,
        out_shape=[out_rows] * 4 + [out_t] * 2,
        compiler_params=pltpu.CompilerParams(
            dimension_semantics=("arbitrary",), vmem_limit_bytes=VMEM_LIMIT),
        name="proj",
    )(x2, g, w)


def _bias_kernel(tab_ref, o_ref):
    h = pl.program_id(0)
    w = pl.program_id(1)
    t, sub = DIFF_BLOCK, LANES

    def sub_tile(offset):
        rel = (lax.broadcasted_iota(jnp.int32, (sub, sub), 0)
               - lax.broadcasted_iota(jnp.int32, (sub, sub), 1) + offset)
        n = jnp.maximum(rel, 0)
        max_exact = N_BUCKETS // 2
        nf = jnp.maximum(n, 1).astype(F32)
        large = max_exact + jnp.floor(jnp.log(nf / max_exact) / math.log(MAX_DISTANCE / max_exact)
                                      * (N_BUCKETS - max_exact)).astype(jnp.int32)
        large = jnp.minimum(large, N_BUCKETS - 1)
        bucket = jnp.where(n < max_exact, n, large)
        b = jnp.zeros((sub, sub), F32)
        for k in range(N_BUCKETS):
            b = jnp.where(bucket == k, tab_ref[k, h], b)
        b = (b - tab_ref[N_BUCKETS - 1, h]) * LOG2E
        return jnp.where(rel >= 0, b, MASK_VALUE)

    def tile(first_offset):
        for br in range(t // sub):
            for bc in range(t // sub):
                offset = first_offset + (br - bc) * sub
                if offset + sub <= 0:
                    val = jnp.full((sub, sub), MASK_VALUE, F32)
                elif offset - sub >= MAX_DISTANCE:
                    val = jnp.zeros((sub, sub), F32)
                else:
                    val = sub_tile(offset)
                o_ref[0, 0, br * sub:(br + 1) * sub, bc * sub:(bc + 1) * sub] = val

    @pl.when(w == 0)
    def _():
        tile(0)

    @pl.when(w == 1)
    def _():
        tile(t)


def _bias_tiles(rel_bias):
    t = DIFF_BLOCK
    return pl.pallas_call(
        _bias_kernel,
        grid=(N_DIFF_HEADS, 2),
        in_specs=[pl.BlockSpec(memory_space=pltpu.SMEM)],
        out_specs=pl.BlockSpec((1, 1, t, t), lambda h, w: (h, w, 0, 0)),
        out_shape=jax.ShapeDtypeStruct((N_DIFF_HEADS, 2, t, t), F32),
        compiler_params=pltpu.CompilerParams(
            dimension_semantics=("arbitrary", "arbitrary"), vmem_limit_bytes=VMEM_LIMIT),
        name="bias",
    )(rel_bias)


def _diff_kernel(lam_ref, q_ref, kt_ref, v_ref, bias_ref, g_ref, o_ref,
                 acc_sc, stat_sc, s_sc, m_sc, l_sc, slow_acc_sc, ok_sm):
    i = pl.program_id(1)
    t = DIFF_BLOCK
    tq = 2 * t
    half_a, half_b = slice(0, t), slice(t, tq)
    lane = lax.broadcasted_iota(jnp.int32, (tq, LANES), 1)
    q = q_ref[...]
    n_pairs = jnp.maximum(i - 1, 0)

    @pl.when(i == 0)
    def _():
        def norm_body(c, ms):
            k = kt_ref[:, pl.ds(pl.multiple_of(c * t, t), t)].astype(F32)
            sq = k * k
            return (jnp.maximum(ms[0], jnp.sum(sq[:HEAD_DIM], axis=0, keepdims=True)),
                    jnp.maximum(ms[1], jnp.sum(sq[HEAD_DIM:], axis=0, keepdims=True)))

        zeros = jnp.zeros((1, t), F32)
        ms = lax.fori_loop(0, kt_ref.shape[1] // t, norm_body, (zeros, zeros))
        for mp in range(2):
            stat_sc[mp] = jnp.broadcast_to(jnp.max(ms[mp], axis=1, keepdims=True), stat_sc.shape[1:])
        bias_max = jnp.max(jnp.max(bias_ref[0, 0], axis=1, keepdims=True), axis=0, keepdims=True)
        stat_sc[2] = jnp.broadcast_to(bias_max, stat_sc.shape[1:])

    def write_out(rows, o0, o1):
        lp = lam_ref[...]
        lam = (jnp.exp(jnp.sum(lp[0:1] * lp[1:2], axis=1, keepdims=True))
               - jnp.exp(jnp.sum(lp[2:3] * lp[3:4], axis=1, keepdims=True)) + LAMBDA_INIT)
        o_ref[rows, :] = (_rms(o0 - lam * o1, g_ref[...]) * (1.0 - LAMBDA_INIT)).astype(o_ref.dtype)

    every = slice(0, tq)
    pair = 2 * t
    zero_q = jnp.zeros_like(q)
    qm = (jnp.where(lane < HEAD_DIM, q, zero_q), jnp.where(lane >= HEAD_DIM, q, zero_q))
    bias_diag, bias_prev = bias_ref[0, 0], bias_ref[0, 1]

    def fast_path(first_block, far_pairs):
        qsq = q.astype(F32) * q.astype(F32)
        bound = []
        for mp in range(2):
            own = (lane < HEAD_DIM) if mp == 0 else (lane >= HEAD_DIM)
            qn = jnp.sum(jnp.where(own, qsq, 0.0), axis=1, keepdims=True)
            b = jnp.sqrt(qn * stat_sc[mp][0:1, 0:1]) * NORM_BOUND_SLACK + stat_sc[2][0:1, 0:1]
            bound.append(jnp.broadcast_to(b, (tq, LANES)))

        acc_sc[...] = jnp.zeros(acc_sc.shape, F32)

        def scores(rows, start, width):
            kt = kt_ref[:, pl.ds(start, width)]
            return [jnp.dot(qm[mp][rows], kt, preferred_element_type=F32)
                    - _tile_lanes(bound[mp][rows], width // LANES) for mp in range(2)]

        def accumulate(rows, s, start, width):
            v_aug = jnp.concatenate([v_ref[pl.ds(start, width), :], jnp.ones((width, LANES), BF16)],
                                    axis=1)
            for mp in range(2):
                acc_sc[mp, rows, :] += jnp.dot(jnp.exp2(s[mp]).astype(BF16), v_aug,
                                               preferred_element_type=F32)

        def last_block_of_b(block):
            start = pl.multiple_of(block * t, t)
            s = [x + bias_diag for x in scores(half_b, start, t)]
            accumulate(half_b, s, start, t)

        if far_pairs is not None:
            first = scores(every, 0, pair)
            for mp in range(2):
                s_sc[mp] = first[mp]

            def far_body(r, carry):
                cur = [s_sc[mp] for mp in range(2)]
                nxt = scores(every, pl.multiple_of((r + 1) * pair, pair), pair)
                accumulate(every, cur, pl.multiple_of(r * pair, pair), pair)
                for mp in range(2):
                    s_sc[mp] = nxt[mp]
                return carry

            lax.fori_loop(0, far_pairs - 1, far_body, 0)
            accumulate(every, [s_sc[mp] for mp in range(2)],
                       pl.multiple_of((far_pairs - 1) * pair, pair), pair)

        if first_block:
            s = [x + jnp.concatenate([bias_diag, bias_prev], axis=0) for x in scores(every, 0, t)]
            accumulate(every, s, 0, t)
            last_block_of_b(1)
        else:
            start = pl.multiple_of((2 * i - 2) * t, t)
            s = scores(every, start, 3 * t)
            for mp in range(2):
                x = s[mp]
                s[mp] = jnp.concatenate([
                    x[:, :t],
                    jnp.concatenate([x[half_a, t:2 * t] + bias_prev, x[half_b, t:2 * t]], axis=0),
                    jnp.concatenate([x[half_a, 2 * t:] + bias_diag, x[half_b, 2 * t:] + bias_prev], axis=0),
                ], axis=1)
            accumulate(every, s, start, 3 * t)
            last_block_of_b(2 * i + 1)

        l0, l1 = acc_sc[0][:, LANES:], acc_sc[1][:, LANES:]
        write_out(every, acc_sc[0][:, :LANES] / l0, acc_sc[1][:, :LANES] / l1)
        ok_sm[0] = (jnp.min(jnp.minimum(l0, l1)) >= SOFTMAX_MIN_DENOMINATOR).astype(jnp.int32)

    @pl.when(i == 0)
    def _():
        fast_path(True, None)

    @pl.when(i == 1)
    def _():
        fast_path(False, None)

    @pl.when(i >= 2)
    def _():
        fast_path(False, n_pairs)

    @pl.when(ok_sm[0] == 0)
    def _():
        def online_softmax_half(rows, diag_block):
            m_sc[...] = jnp.full(m_sc.shape, MASK_VALUE, F32)
            l_sc[...] = jnp.zeros(l_sc.shape, F32)
            slow_acc_sc[...] = jnp.zeros(slow_acc_sc.shape, F32)

            def step(j, bias):
                start = pl.multiple_of(j * t, t)
                kt = kt_ref[:, pl.ds(start, t)]
                v = v_ref[pl.ds(start, t), :]
                for mp in range(2):
                    s = jnp.dot(qm[mp][rows], kt, preferred_element_type=F32)
                    if bias is not None:
                        s = s + bias
                    m_prev = m_sc[mp]
                    m_new = jnp.maximum(m_prev, jnp.max(s, axis=1, keepdims=True))
                    alpha = jnp.exp2(m_prev - m_new)
                    p = jnp.exp2(s - _tile_lanes(m_new, t // LANES))
                    l_sc[mp] = alpha * l_sc[mp] + jnp.sum(p, axis=1, keepdims=True)
                    slow_acc_sc[mp] = (alpha * slow_acc_sc[mp]
                                       + jnp.dot(p.astype(BF16), v, preferred_element_type=F32))
                    m_sc[mp] = m_new

            def slow_body(j, carry):
                step(j, None)
                return carry

            lax.fori_loop(0, jnp.maximum(diag_block - 1, 0), slow_body, 0)

            @pl.when(diag_block >= 1)
            def _():
                step(diag_block - 1, bias_prev)

            step(diag_block, bias_diag)
            write_out(rows, slow_acc_sc[0] / l_sc[0], slow_acc_sc[1] / l_sc[1])

        online_softmax_half(half_a, 2 * i)
        online_softmax_half(half_b, 2 * i + 1)


def _diff_attention(lam_params, qd, kdt, vd, bias, g):
    s = qd.shape[0]
    t = DIFF_BLOCK
    tq = 2 * t
    return pl.pallas_call(
        _diff_kernel,
        grid=(N_DIFF_HEADS, s // tq),
        in_specs=[
            pl.BlockSpec(lam_params.shape, lambda h, i: (0, 0)),
            pl.BlockSpec((tq, LANES), lambda h, i: (i, h)),
            pl.BlockSpec((LANES, s), lambda h, i: (h, 0)),
            pl.BlockSpec((s, LANES), lambda h, i: (0, h)),
            pl.BlockSpec((1, 2, t, t), lambda h, i: (h, 0, 0, 0)),
            pl.BlockSpec(g.shape, lambda h, i: (0, 0)),
        ],
        out_specs=pl.BlockSpec((tq, LANES), lambda h, i: (i, h)),
        out_shape=jax.ShapeDtypeStruct((s, DIFF_WIDTH), BF16),
        scratch_shapes=[pltpu.VMEM((2, tq, 2 * LANES), F32), pltpu.VMEM((3, 8, LANES), F32),
                        pltpu.VMEM((2, tq, 2 * t), F32)]
        + [pltpu.VMEM((2, t, LANES), F32)] * 3 + [pltpu.SMEM((1,), jnp.int32)],
        compiler_params=pltpu.CompilerParams(
            dimension_semantics=("arbitrary", "arbitrary"), vmem_limit_bytes=DIFF_VMEM_LIMIT),
        name="diff_attn",
    )(lam_params, qd, kdt, vd, bias, g)


def _sb_kernel(q_ref, kt_ref, v_ref, g_ref, o_ref, carry_sc, acc_sc, knorm_sc, thresh_sm, done_sm):
    i = pl.program_id(1)
    t = SB_K_BLOCK
    n_sub = SB_Q_BLOCK // t
    lane = lax.broadcasted_iota(jnp.int32, (t, LANES), 1)
    head0 = lane < HEAD_DIM
    lower = (lax.broadcasted_iota(jnp.int32, (t, t), 0)
             >= lax.broadcasted_iota(jnp.int32, (t, t), 1)).astype(BF16)
    strict = lax.broadcasted_iota(jnp.int32, (t, t), 0) > lax.broadcasted_iota(jnp.int32, (t, t), 1)

    @pl.when(i == 0)
    def _():
        def norm_body(c, m):
            k = kt_ref[:, pl.ds(pl.multiple_of(c * t, t), t)].astype(F32)
            sq = k * k
            return jnp.maximum(m, jnp.maximum(jnp.sum(sq[:HEAD_DIM], axis=0, keepdims=True),
                                              jnp.sum(sq[HEAD_DIM:], axis=0, keepdims=True)))

        m = lax.fori_loop(0, kt_ref.shape[1] // t, norm_body, jnp.zeros((1, t), F32))
        knorm_sc[...] = jnp.broadcast_to(jnp.max(m, axis=1, keepdims=True), knorm_sc.shape)

    def load_q(u):
        q = q_ref[pl.ds(pl.multiple_of(u * t, t), t), :]
        zero_q = jnp.zeros_like(q)
        return q, (jnp.where(head0, q, zero_q), jnp.where(head0, zero_q, q))

    def block_pass(jb, consumers):
        start = pl.multiple_of(jb * t, t)
        kt = kt_ref[:, pl.ds(start, t)]
        v = v_ref[pl.ds(start, t), :]
        zero_v = jnp.zeros_like(v)
        v2 = jnp.concatenate([jnp.where(head0, v, zero_v), jnp.where(head0, zero_v, v)], axis=0)
        q_rows = [qm[hh] for _, qm, _ in consumers for hh in range(2)]
        z = jnp.dot(jnp.concatenate(q_rows, axis=0), kt, preferred_element_type=F32)
        nl = jnp.maximum(z, 0.0) + jnp.log(1.0 + jnp.exp2(-jnp.abs(z))) * LOG2E
        segments = [slice(n * t, (n + 1) * t) for n in range(len(q_rows))]
        masked = [m for _, _, m in consumers for _ in range(2)]
        nl = jnp.concatenate([jnp.where(strict, nl[seg], 0.0) if m else nl[seg]
                              for seg, m in zip(segments, masked)], axis=0)
        csum = jnp.dot(nl.astype(BF16), lower, preferred_element_type=F32)
        probs = []
        for n, (u, _, m) in enumerate(consumers):
            heads = []
            for hh in range(2):
                seg = segments[2 * n + hh]
                if m:
                    a = jnp.where(strict, jnp.exp2(z[seg] - csum[seg]), 0.0)
                    carry_sc[u, hh] = jnp.broadcast_to(csum[seg][:, 0:1], (t, LANES))
                else:
                    carry = carry_sc[u, hh]
                    a = jnp.exp2((z[seg] - _tile_lanes(carry, t // LANES)) - csum[seg])
                    carry_sc[u, hh] = carry + csum[seg][:, 0:1]
                heads.append(a.astype(BF16))
            probs.append(jnp.concatenate(heads, axis=1))
        pv = jnp.dot(jnp.concatenate(probs, axis=0), v2, preferred_element_type=F32)
        for n, (u, _, m) in enumerate(consumers):
            if m:
                acc_sc[u] = pv[n * t:(n + 1) * t]
            else:
                acc_sc[u] += pv[n * t:(n + 1) * t]

    qs = [load_q(u) for u in range(n_sub)]

    for u in range(n_sub):
        qsq = qs[u][0].astype(F32) * qs[u][0].astype(F32)
        qn0 = jnp.sum(jnp.where(head0, qsq, 0.0), axis=1, keepdims=True)
        qn1 = jnp.sum(jnp.where(head0, 0.0, qsq), axis=1, keepdims=True)
        qnorm = jnp.max(jnp.maximum(qn0, qn1), axis=0, keepdims=True)
        thresh = jnp.sqrt(qnorm * knorm_sc[0:1, 0:1]) * NORM_BOUND_SLACK + EXP2_UNDERFLOW
        thresh_sm[u] = jnp.max(thresh)

    def near_passes(has_previous_block):
        for u in range(n_sub - 1, -1, -1):
            consumers = [(u, qs[u][1], True)]
            if u + 1 < n_sub:
                consumers.append((u + 1, qs[u + 1][1], False))
            block_pass(i * n_sub + u, consumers)
        if has_previous_block:
            block_pass(i * n_sub - 1, [(0, qs[0][1], False)])

    @pl.when(i == 0)
    def _():
        near_passes(False)

    @pl.when(i >= 1)
    def _():
        near_passes(True)

    def carry_min(u):
        return jnp.min(jnp.minimum(carry_sc[u, 0], carry_sc[u, 1]))

    for u in range(n_sub):
        done_sm[u] = (carry_min(u) >= thresh_sm[u]).astype(jnp.int32)

    def remaining_blocks(u, c):
        _, qm = load_q(u)

        def far_cond(state):
            jb, all_zero = state
            return jnp.logical_and(jb >= 0, all_zero == 0)

        def far_body(state):
            jb, _ = state
            block_pass(jb, [(u, qm, False)])
            return jb - 1, (carry_min(u) >= thresh_sm[u]).astype(jnp.int32)

        lax.while_loop(far_cond, far_body, (i * n_sub + u - 2, done_sm[u]))
        return c

    lax.fori_loop(0, n_sub, remaining_blocks, 0)

    for u in range(n_sub):
        o = acc_sc[u]
        sq = o * o
        ss0 = jnp.sum(jnp.where(head0, sq, 0.0), axis=1, keepdims=True)
        ss1 = jnp.sum(jnp.where(head0, 0.0, sq), axis=1, keepdims=True)
        ms = jnp.where(head0, ss0, ss1) * (1.0 / HEAD_DIM)
        o_ref[u * t:(u + 1) * t, :] = (o * lax.rsqrt(ms + RMS_EPS) * g_ref[...]).astype(o_ref.dtype)


def _sb_attention(qs, kst, vs, g2):
    s = qs.shape[0]
    tq, t = SB_Q_BLOCK, SB_K_BLOCK
    return pl.pallas_call(
        _sb_kernel,
        grid=(N_SB_HEADS // 2, s // tq),
        in_specs=[
            pl.BlockSpec((tq, LANES), lambda h, i: (i, h)),
            pl.BlockSpec((LANES, s), lambda h, i: (h, 0)),
            pl.BlockSpec((s, LANES), lambda h, i: (0, h)),
            pl.BlockSpec(g2.shape, lambda h, i: (0, 0)),
        ],
        out_specs=pl.BlockSpec((tq, LANES), lambda h, i: (i, h)),
        out_shape=jax.ShapeDtypeStruct((s, SB_WIDTH), BF16),
        scratch_shapes=[pltpu.VMEM((tq // t, 2, t, LANES), F32), pltpu.VMEM((tq // t, t, LANES), F32),
                        pltpu.VMEM((8, LANES), F32),
                        pltpu.SMEM((tq // t,), F32), pltpu.SMEM((tq // t,), jnp.int32)],
        compiler_params=pltpu.CompilerParams(
            dimension_semantics=("arbitrary", "arbitrary"), vmem_limit_bytes=VMEM_LIMIT),
        name="sb_attn",
    )(qs, kst, vs, g2)


def _gelu_tanh(x):
    return 0.5 * x * (1.0 + jnp.tanh(math.sqrt(2.0 / math.pi) * (x + 0.044715 * (x * x * x))))


def _mlp_kernel(d_ref, s_ref, x_ref, dh_ref, sh_ref, xh_ref, wod_ref, wos_ref, gattn_ref, gpre_ref,
                wup_ref, cw_ref, cb_ref, wd_ref, gffn_ref, o_ref, u_sc):
    i = pl.program_id(0)
    rows, chunk = MLP_ROWS, FFN_CHUNK
    stack = lambda main_ref, halo_ref: jnp.concatenate([main_ref[...], halo_ref[...]], axis=0)
    mix = (jnp.dot(stack(d_ref, dh_ref), wod_ref[...], preferred_element_type=F32)
           + jnp.dot(stack(s_ref, sh_ref), wos_ref[...], preferred_element_type=F32))
    x1 = stack(x_ref, xh_ref) + _rms(mix, gattn_ref[...])
    h = _rms(x1, gpre_ref[...])
    row = lax.broadcasted_iota(jnp.int32, h.shape, 0)
    lhs = jnp.where(jnp.logical_or(row < rows, i > 0), h, 0.0).astype(BF16)

    def conv(col, slot):
        cols = slice(col, col + chunk)
        u = jnp.dot(lhs, wup_ref[:, cols], preferred_element_type=F32)
        u_sc[slot, 0:HALO_ROWS, :] = u[rows:]
        u_sc[slot, HALO_ROWS:, :] = u[:rows]
        y = cb_ref[:, cols]
        for k in range(CONV_WIDTH):
            off = HALO_ROWS - (CONV_WIDTH - 1) + k
            y = y + u_sc[slot, off:off + rows, :] * cw_ref[k:k + 1, cols]
        return y

    n_chunk = D_FF // chunk
    gates = [conv(c * chunk, 2 * c) for c in range(n_chunk)]
    vals = [conv(D_FF + c * chunk, 2 * c + 1) for c in range(n_chunk)]
    y = None
    for c in range(n_chunk):
        act = (_gelu_tanh(gates[c]) * vals[c]).astype(BF16)
        part = jnp.dot(act, wd_ref[c * chunk:(c + 1) * chunk, :], preferred_element_type=F32)
        y = part if y is None else y + part
    o_ref[...] = x1[:rows] + _rms(y, gffn_ref[...])


def _mlp(diff_o, sb_o, x2, w_od, w_os, g_attn, g_pre, w_up, conv_w, conv_b, w_down, g_ffn):
    s = x2.shape[0]
    rows = MLP_ROWS
    halo_blocks = rows // HALO_ROWS
    row_spec = lambda w: pl.BlockSpec((rows, w), lambda i: (i, 0))
    halo_spec = lambda w: pl.BlockSpec((HALO_ROWS, w), lambda i: (jnp.maximum(i * halo_blocks - 1, 0), 0))
    resident = lambda a: pl.BlockSpec(a.shape, lambda i: (0, 0), pipeline_mode=pl.Buffered(1))
    return pl.pallas_call(
        _mlp_kernel,
        grid=(s // rows,),
        in_specs=[row_spec(DIFF_WIDTH), row_spec(SB_WIDTH), row_spec(D_MODEL),
                  halo_spec(DIFF_WIDTH), halo_spec(SB_WIDTH), halo_spec(D_MODEL),
                  resident(w_od), resident(w_os), resident(g_attn), resident(g_pre),
                  resident(w_up), resident(conv_w), resident(conv_b), resident(w_down), resident(g_ffn)],
        out_specs=row_spec(D_MODEL),
        out_shape=jax.ShapeDtypeStruct((s, D_MODEL), F32),
        scratch_shapes=[pltpu.VMEM((2 * (D_FF // FFN_CHUNK), HALO_ROWS + rows, FFN_CHUNK), F32)],
        compiler_params=pltpu.CompilerParams(
            dimension_semantics=("arbitrary",), vmem_limit_bytes=MLP_VMEM_LIMIT),
        name="mlp",
    )(diff_o, sb_o, x2, diff_o, sb_o, x2, w_od, w_os, g_attn, g_pre, w_up, conv_w, conv_b, w_down, g_ffn)


def kernel(x, attn_pre_norm, w_qkv, lambda_q1, lambda_k1, lambda_q2, lambda_k2, diff_subln, sb_norm,
           rel_bias, w_o, attn_post_norm, ffn_pre_norm, w_up, conv_w, conv_b, w_down, ffn_post_norm):
    b, s, _ = x.shape
    assert b == 1 and attn_pre_norm.shape[0] == 1, "single sequence, single layer"
    assert s % max(PROJ_ROWS, 2 * DIFF_BLOCK, SB_Q_BLOCK, MLP_ROWS) == 0
    x2 = x[0]
    row = lambda a: a.reshape(1, -1).astype(F32)

    qd, vd, qs, vs, kdt, kst = _proj(x2, row(attn_pre_norm), w_qkv[0].astype(BF16))

    bias = _bias_tiles(rel_bias.astype(F32))
    lam_params = jnp.concatenate([lambda_q1, lambda_k1, lambda_q2, lambda_k2], axis=0).astype(F32)
    diff_o = _diff_attention(lam_params, qd, kdt, vd, bias, row(diff_subln))
    sb_o = _sb_attention(qs, kst, vs, row(jnp.concatenate([sb_norm[0], sb_norm[0]])))

    wo = w_o[0].astype(BF16)
    out = _mlp(diff_o, sb_o, x2, wo[:DIFF_WIDTH], wo[DIFF_WIDTH:], row(attn_post_norm), row(ffn_pre_norm),
               w_up[0].astype(BF16), conv_w[0].astype(F32), conv_b.astype(F32), w_down[0].astype(BF16),
               row(ffn_post_norm))
    return out[None]
```

```python
import math

import jax
import jax.numpy as jnp
from jax import lax
from jax.experimental import pallas as pl
from jax.experimental.pallas import tpu as pltpu

F32 = jnp.float32
BF16 = jnp.bfloat16

D_MODEL = 1024
HEAD_DIM = 64
N_DIFF_HEADS = 4
N_SB_HEADS = 8
DIFF_WIDTH = N_DIFF_HEADS * 2 * HEAD_DIM
SB_WIDTH = N_SB_HEADS * HEAD_DIM
QKV_GROUP = DIFF_WIDTH
assert SB_WIDTH == QKV_GROUP
D_FF = 2816
CONV_WIDTH = 3
N_BUCKETS = 32
MAX_DISTANCE = 128
RMS_EPS = 1e-6
LAMBDA_INIT = 0.8 - 0.6 * math.exp(-0.3 * 0)

LOG2E = 1.4426950408889634
Q_SCALE = HEAD_DIM ** -0.5 * LOG2E
MASK_VALUE = -1e30
EXP2_UNDERFLOW = 160.0
NORM_BOUND_SLACK = 1.0 + 2.0 ** -10
SOFTMAX_MIN_DENOMINATOR = 2.0 ** -60

LANES = 128
PROJ_ROWS = 1024
DIFF_BLOCK = 512
SB_Q_BLOCK = 4096
SB_K_BLOCK = 256
MLP_ROWS = 512
FFN_CHUNK = D_FF // 2
HALO_ROWS = 16
VMEM_LIMIT = 48 * 1024 * 1024
DIFF_VMEM_LIMIT = 58 * 1024 * 1024
MLP_VMEM_LIMIT = 58 * 1024 * 1024


def _tile_lanes(x, k):
    return jnp.concatenate([x] * k, axis=1) if k > 1 else x


def _rms(x, g):
    return x * lax.rsqrt(jnp.mean(x * x, axis=-1, keepdims=True) + RMS_EPS) * g


def _proj_kernel(x_ref, g_ref, w_ref, qd_ref, vd_ref, qs_ref, vs_ref, kdt_ref, kst_ref):
    hb = _rms(x_ref[...], g_ref[...])
    proj = lambda c: jnp.dot(hb, w_ref[:, c * QKV_GROUP:(c + 1) * QKV_GROUP], preferred_element_type=F32)
    qd_ref[...] = (proj(0) * Q_SCALE).astype(BF16)
    vd_ref[...] = proj(2).astype(BF16)
    qs_ref[...] = (proj(3) * Q_SCALE).astype(BF16)
    vs_ref[...] = proj(5).astype(BF16)
    proj_t = lambda c: lax.dot_general(w_ref[:, c * QKV_GROUP:(c + 1) * QKV_GROUP], hb, (((0,), (1,)), ((), ())),
                                       preferred_element_type=F32)
    kdt_ref[...] = proj_t(1).astype(BF16)
    kst_ref[...] = proj_t(4).astype(BF16)


def _proj(x2, g, w):
    s = x2.shape[0]
    rows = PROJ_ROWS
    row_spec = lambda w: pl.BlockSpec((rows, w), lambda i: (i, 0))
    full = lambda a: pl.BlockSpec(a.shape, lambda i: (0, 0))
    out_rows = jax.ShapeDtypeStruct((s, QKV_GROUP), BF16)
    out_t = jax.ShapeDtypeStruct((QKV_GROUP, s), BF16)
    return pl.pallas_call(
        _proj_kernel,
        grid=(s // rows,),
        in_specs=[row_spec(D_MODEL), full(g),
                  pl.BlockSpec(w.shape, lambda i: (0, 0), pipeline_mode=pl.Buffered(1))],
        out_specs=[row_spec(QKV_GROUP)] * 4 + [pl.BlockSpec((QKV_GROUP, rows), lambda i: (0, i))] * 2,
        out_shape=[out_rows] * 4 + [out_t] * 2,
        compiler_params=pltpu.CompilerParams(
            dimension_semantics=("arbitrary",), vmem_limit_bytes=VMEM_LIMIT),
        name="proj",
    )(x2, g, w)


def _bias_kernel(tab_ref, o_ref):
    h = pl.program_id(0)
    w = pl.program_id(1)
    t, sub = DIFF_BLOCK, LANES

    def sub_tile(offset):
        rel = (lax.broadcasted_iota(jnp.int32, (sub, sub), 0)
               - lax.broadcasted_iota(jnp.int32, (sub, sub), 1) + offset)
        n = jnp.maximum(rel, 0)
        max_exact = N_BUCKETS // 2
        nf = jnp.maximum(n, 1).astype(F32)
        large = max_exact + jnp.floor(jnp.log(nf / max_exact) / math.log(MAX_DISTANCE / max_exact)
                                      * (N_BUCKETS - max_exact)).astype(jnp.int32)
        large = jnp.minimum(large, N_BUCKETS - 1)
        bucket = jnp.where(n < max_exact, n, large)
        b = jnp.zeros((sub, sub), F32)
        for k in range(N_BUCKETS):
            b = jnp.where(bucket == k, tab_ref[k, h], b)
        b = (b - tab_ref[N_BUCKETS - 1, h]) * LOG2E
        return jnp.where(rel >= 0, b, MASK_VALUE)

    def tile(first_offset):
        for br in range(t // sub):
            for bc in range(t // sub):
                offset = first_offset + (br - bc) * sub
                if offset + sub <= 0:
                    val = jnp.full((sub, sub), MASK_VALUE, F32)
                elif offset - sub >= MAX_DISTANCE:
                    val = jnp.zeros((sub, sub), F32)
                else:
                    val = sub_tile(offset)
                o_ref[0, 0, br * sub:(br + 1) * sub, bc * sub:(bc + 1) * sub] = val

    @pl.when(w == 0)
    def _():
        tile(0)

    @pl.when(w == 1)
    def _():
        tile(t)


def _bias_tiles(rel_bias):
    t = DIFF_BLOCK
    return pl.pallas_call(
        _bias_kernel,
        grid=(N_DIFF_HEADS, 2),
        in_specs=[pl.BlockSpec(memory_space=pltpu.SMEM)],
        out_specs=pl.BlockSpec((1, 1, t, t), lambda h, w: (h, w, 0, 0)),
        out_shape=jax.ShapeDtypeStruct((N_DIFF_HEADS, 2, t, t), F32),
        compiler_params=pltpu.CompilerParams(
            dimension_semantics=("arbitrary", "arbitrary"), vmem_limit_bytes=VMEM_LIMIT),
        name="bias",
    )(rel_bias)


def _diff_kernel(lam_ref, q_ref, kt_ref, v_ref, bias_ref, g_ref, o_ref,
                 acc_sc, stat_sc, s_sc, m_sc, l_sc, slow_acc_sc, ok_sm):
    i = pl.program_id(1)
    t = DIFF_BLOCK
    tq = 2 * t
    half_a, half_b = slice(0, t), slice(t, tq)
    lane = lax.broadcasted_iota(jnp.int32, (tq, LANES), 1)
    q = q_ref[...]
    n_pairs = jnp.maximum(i - 1, 0)

    @pl.when(i == 0)
    def _():
        def norm_body(c, ms):
            k = kt_ref[:, pl.ds(pl.multiple_of(c * t, t), t)].astype(F32)
            sq = k * k
            return (jnp.maximum(ms[0], jnp.sum(sq[:HEAD_DIM], axis=0, keepdims=True)),
                    jnp.maximum(ms[1], jnp.sum(sq[HEAD_DIM:], axis=0, keepdims=True)))

        zeros = jnp.zeros((1, t), F32)
        ms = lax.fori_loop(0, kt_ref.shape[1] // t, norm_body, (zeros, zeros))
        for mp in range(2):
            stat_sc[mp] = jnp.broadcast_to(jnp.max(ms[mp], axis=1, keepdims=True), stat_sc.shape[1:])
        bias_max = jnp.max(jnp.max(bias_ref[0, 0], axis=1, keepdims=True), axis=0, keepdims=True)
        stat_sc[2] = jnp.broadcast_to(bias_max, stat_sc.shape[1:])

    def write_out(rows, o0, o1):
        lp = lam_ref[...]
        lam = (jnp.exp(jnp.sum(lp[0:1] * lp[1:2], axis=1, keepdims=True))
               - jnp.exp(jnp.sum(lp[2:3] * lp[3:4], axis=1, keepdims=True)) + LAMBDA_INIT)
        o_ref[rows, :] = (_rms(o0 - lam * o1, g_ref[...]) * (1.0 - LAMBDA_INIT)).astype(o_ref.dtype)

    every = slice(0, tq)
    pair = 2 * t
    zero_q = jnp.zeros_like(q)
    qm = (jnp.where(lane < HEAD_DIM, q, zero_q), jnp.where(lane >= HEAD_DIM, q, zero_q))
    bias_diag, bias_prev = bias_ref[0, 0], bias_ref[0, 1]

    def fast_path(first_block, far_pairs):
        qsq = q.astype(F32) * q.astype(F32)
        bound = []
        for mp in range(2):
            own = (lane < HEAD_DIM) if mp == 0 else (lane >= HEAD_DIM)
            qn = jnp.sum(jnp.where(own, qsq, 0.0), axis=1, keepdims=True)
            b = jnp.sqrt(qn * stat_sc[mp][0:1, 0:1]) * NORM_BOUND_SLACK + stat_sc[2][0:1, 0:1]
            bound.append(jnp.broadcast_to(b, (tq, LANES)))

        acc_sc[...] = jnp.zeros(acc_sc.shape, F32)

        def scores(rows, start, width):
            kt = kt_ref[:, pl.ds(start, width)]
            return [jnp.dot(qm[mp][rows], kt, preferred_element_type=F32)
                    - _tile_lanes(bound[mp][rows], width // LANES) for mp in range(2)]

        def accumulate(rows, s, start, width):
            v_aug = jnp.concatenate([v_ref[pl.ds(start, width), :], jnp.ones((width, LANES), BF16)],
                                    axis=1)
            for mp in range(2):
                acc_sc[mp, rows, :] += jnp.dot(jnp.exp2(s[mp]).astype(BF16), v_aug,
                                               preferred_element_type=F32)

        def last_block_of_b(block):
            start = pl.multiple_of(block * t, t)
            s = [x + bias_diag for x in scores(half_b, start, t)]
            accumulate(half_b, s, start, t)

        if far_pairs is not None:
            first = scores(every, 0, pair)
            for mp in range(2):
                s_sc[mp] = first[mp]

            def far_body(r, carry):
                cur = [s_sc[mp] for mp in range(2)]
                nxt = scores(every, pl.multiple_of((r + 1) * pair, pair), pair)
                accumulate(every, cur, pl.multiple_of(r * pair, pair), pair)
                for mp in range(2):
                    s_sc[mp] = nxt[mp]
                return carry

            lax.fori_loop(0, far_pairs - 1, far_body, 0)
            accumulate(every, [s_sc[mp] for mp in range(2)],
                       pl.multiple_of((far_pairs - 1) * pair, pair), pair)

        if first_block:
            s = [x + jnp.concatenate([bias_diag, bias_prev], axis=0) for x in scores(every, 0, t)]
            accumulate(every, s, 0, t)
            last_block_of_b(1)
        else:
            start = pl.multiple_of((2 * i - 2) * t, t)
            s = scores(every, start, 3 * t)
            for mp in range(2):
                x = s[mp]
                s[mp] = jnp.concatenate([
                    x[:, :t],
                    jnp.concatenate([x[half_a, t:2 * t] + bias_prev, x[half_b, t:2 * t]], axis=0),
                    jnp.concatenate([x[half_a, 2 * t:] + bias_diag, x[half_b, 2 * t:] + bias_prev], axis=0),
                ], axis=1)
            accumulate(every, s, start, 3 * t)
            last_block_of_b(2 * i + 1)

        l0, l1 = acc_sc[0][:, LANES:], acc_sc[1][:, LANES:]
        write_out(every, acc_sc[0][:, :LANES] / l0, acc_sc[1][:, :LANES] / l1)
        ok_sm[0] = (jnp.min(jnp.minimum(l0, l1)) >= SOFTMAX_MIN_DENOMINATOR).astype(jnp.int32)

    @pl.when(i == 0)
    def _():
        fast_path(True, None)

    @pl.when(i == 1)
    def _():
        fast_path(False, None)

    @pl.when(i >= 2)
    def _():
        fast_path(False, n_pairs)

    @pl.when(ok_sm[0] == 0)
    def _():
        def online_softmax_half(rows, diag_block):
            m_sc[...] = jnp.full(m_sc.shape, MASK_VALUE, F32)
            l_sc[...] = jnp.zeros(l_sc.shape, F32)
            slow_acc_sc[...] = jnp.zeros(slow_acc_sc.shape, F32)

            def step(j, bias):
                start = pl.multiple_of(j * t, t)
                kt = kt_ref[:, pl.ds(start, t)]
                v = v_ref[pl.ds(start, t), :]
                for mp in range(2):
                    s = jnp.dot(qm[mp][rows], kt, preferred_element_type=F32)
                    if bias is not None:
                        s = s + bias
                    m_prev = m_sc[mp]
                    m_new = jnp.maximum(m_prev, jnp.max(s, axis=1, keepdims=True))
                    alpha = jnp.exp2(m_prev - m_new)
                    p = jnp.exp2(s - _tile_lanes(m_new, t // LANES))
                    l_sc[mp] = alpha * l_sc[mp] + jnp.sum(p, axis=1, keepdims=True)
                    slow_acc_sc[mp] = (alpha * slow_acc_sc[mp]
                                       + jnp.dot(p.astype(BF16), v, preferred_element_type=F32))
                    m_sc[mp] = m_new

            def slow_body(j, carry):
                step(j, None)
                return carry

            lax.fori_loop(0, jnp.maximum(diag_block - 1, 0), slow_body, 0)

            @pl.when(diag_block >= 1)
            def _():
                step(diag_block - 1, bias_prev)

            step(diag_block, bias_diag)
            write_out(rows, slow_acc_sc[0] / l_sc[0], slow_acc_sc[1] / l_sc[1])

        online_softmax_half(half_a, 2 * i)
        online_softmax_half(half_b, 2 * i + 1)


def _diff_attention(lam_params, qd, kdt, vd, bias, g):
    s = qd.shape[0]
    t = DIFF_BLOCK
    tq = 2 * t
    return pl.pallas_call(
        _diff_kernel,
        grid=(N_DIFF_HEADS, s // tq),
        in_specs=[
            pl.BlockSpec(lam_params.shape, lambda h, i: (0, 0)),
            pl.BlockSpec((tq, LANES), lambda h, i: (i, h)),
            pl.BlockSpec((LANES, s), lambda h, i: (h, 0)),
            pl.BlockSpec((s, LANES), lambda h, i: (0, h)),
            pl.BlockSpec((1, 2, t, t), lambda h, i: (h, 0, 0, 0)),
            pl.BlockSpec(g.shape, lambda h, i: (0, 0)),
        ],
        out_specs=pl.BlockSpec((tq, LANES), lambda h, i: (i, h)),
        out_shape=jax.ShapeDtypeStruct((s, DIFF_WIDTH), BF16),
        scratch_shapes=[pltpu.VMEM((2, tq, 2 * LANES), F32), pltpu.VMEM((3, 8, LANES), F32),
                        pltpu.VMEM((2, tq, 2 * t), F32)]
        + [pltpu.VMEM((2, t, LANES), F32)] * 3 + [pltpu.SMEM((1,), jnp.int32)],
        compiler_params=pltpu.CompilerParams(
            dimension_semantics=("arbitrary", "arbitrary"), vmem_limit_bytes=DIFF_VMEM_LIMIT),
        name="diff_attn",
    )(lam_params, qd, kdt, vd, bias, g)


def _sb_kernel(q_ref, kt_ref, v_ref, g_ref, o_ref, carry_sc, acc_sc, knorm_sc, thresh_sm, done_sm):
    i = pl.program_id(1)
    t = SB_K_BLOCK
    n_sub = SB_Q_BLOCK // t
    lane = lax.broadcasted_iota(jnp.int32, (t, LANES), 1)
    head0 = lane < HEAD_DIM
    lower = (lax.broadcasted_iota(jnp.int32, (t, t), 0)
             >= lax.broadcasted_iota(jnp.int32, (t, t), 1)).astype(BF16)
    strict = lax.broadcasted_iota(jnp.int32, (t, t), 0) > lax.broadcasted_iota(jnp.int32, (t, t), 1)

    @pl.when(i == 0)
    def _():
        def norm_body(c, m):
            k = kt_ref[:, pl.ds(pl.multiple_of(c * t, t), t)].astype(F32)
            sq = k * k
            return jnp.maximum(m, jnp.maximum(jnp.sum(sq[:HEAD_DIM], axis=0, keepdims=True),
                                              jnp.sum(sq[HEAD_DIM:], axis=0, keepdims=True)))

        m = lax.fori_loop(0, kt_ref.shape[1] // t, norm_body, jnp.zeros((1, t), F32))
        knorm_sc[...] = jnp.broadcast_to(jnp.max(m, axis=1, keepdims=True), knorm_sc.shape)

    def load_q(u):
        q = q_ref[pl.ds(pl.multiple_of(u * t, t), t), :]
        zero_q = jnp.zeros_like(q)
        return q, (jnp.where(head0, q, zero_q), jnp.where(head0, zero_q, q))

    def block_pass(jb, consumers):
        start = pl.multiple_of(jb * t, t)
        kt = kt_ref[:, pl.ds(start, t)]
        v = v_ref[pl.ds(start, t), :]
        zero_v = jnp.zeros_like(v)
        v2 = jnp.concatenate([jnp.where(head0, v, zero_v), jnp.where(head0, zero_v, v)], axis=0)
        q_rows = [qm[hh] for _, qm, _ in consumers for hh in range(2)]
        z = jnp.dot(jnp.concatenate(q_rows, axis=0), kt, preferred_element_type=F32)
        nl = jnp.maximum(z, 0.0) + jnp.log(1.0 + jnp.exp2(-jnp.abs(z))) * LOG2E
        segments = [slice(n * t, (n + 1) * t) for n in range(len(q_rows))]
        masked = [m for _, _, m in consumers for _ in range(2)]
        nl = jnp.concatenate([jnp.where(strict, nl[seg], 0.0) if m else nl[seg]
                              for seg, m in zip(segments, masked)], axis=0)
        csum = jnp.dot(nl.astype(BF16), lower, preferred_element_type=F32)
        probs = []
        for n, (u, _, m) in enumerate(consumers):
            heads = []
            for hh in range(2):
                seg = segments[2 * n + hh]
                if m:
                    a = jnp.where(strict, jnp.exp2(z[seg] - csum[seg]), 0.0)
                    carry_sc[u, hh] = jnp.broadcast_to(csum[seg][:, 0:1], (t, LANES))
                else:
                    carry = carry_sc[u, hh]
                    a = jnp.exp2((z[seg] - _tile_lanes(carry, t // LANES)) - csum[seg])
                    carry_sc[u, hh] = carry + csum[seg][:, 0:1]
                heads.append(a.astype(BF16))
            probs.append(jnp.concatenate(heads, axis=1))
        pv = jnp.dot(jnp.concatenate(probs, axis=0), v2, preferred_element_type=F32)
        for n, (u, _, m) in enumerate(consumers):
            if m:
                acc_sc[u] = pv[n * t:(n + 1) * t]
            else:
                acc_sc[u] += pv[n * t:(n + 1) * t]

    qs = [load_q(u) for u in range(n_sub)]

    qmax = jnp.zeros((t, LANES), F32)
    for u in range(n_sub):
        qmax = jnp.maximum(qmax, jnp.abs(qs[u][0].astype(F32)))
    qmax = jnp.max(jnp.max(qmax, axis=1, keepdims=True), axis=0, keepdims=True)
    thresh = jnp.sqrt(HEAD_DIM * qmax * qmax * knorm_sc[0:1, 0:1]) * NORM_BOUND_SLACK + EXP2_UNDERFLOW
    thresh_sm[0] = jnp.max(thresh)

    def near_passes(has_previous_block):
        for u in range(n_sub - 1, -1, -1):
            consumers = [(u, qs[u][1], True)]
            if u + 1 < n_sub:
                consumers.append((u + 1, qs[u + 1][1], False))
            block_pass(i * n_sub + u, consumers)
        if has_previous_block:
            block_pass(i * n_sub - 1, [(0, qs[0][1], False)])

    @pl.when(i == 0)
    def _():
        near_passes(False)

    @pl.when(i >= 1)
    def _():
        near_passes(True)

    def carry_min(u):
        return jnp.min(jnp.minimum(carry_sc[u, 0], carry_sc[u, 1]))

    for u in range(n_sub):
        done_sm[u] = (carry_min(u) >= thresh_sm[0]).astype(jnp.int32)

    def remaining_blocks(u, c):
        _, qm = load_q(u)

        def far_cond(state):
            jb, all_zero = state
            return jnp.logical_and(jb >= 0, all_zero == 0)

        def far_body(state):
            jb, _ = state
            block_pass(jb, [(u, qm, False)])
            return jb - 1, (carry_min(u) >= thresh_sm[0]).astype(jnp.int32)

        lax.while_loop(far_cond, far_body, (i * n_sub + u - 2, done_sm[u]))
        return c

    lax.fori_loop(0, n_sub, remaining_blocks, 0)

    for u in range(n_sub):
        o = acc_sc[u]
        sq = o * o
        ss0 = jnp.sum(jnp.where(head0, sq, 0.0), axis=1, keepdims=True)
        ss1 = jnp.sum(jnp.where(head0, 0.0, sq), axis=1, keepdims=True)
        ms = jnp.where(head0, ss0, ss1) * (1.0 / HEAD_DIM)
        o_ref[u * t:(u + 1) * t, :] = (o * lax.rsqrt(ms + RMS_EPS) * g_ref[...]).astype(o_ref.dtype)


def _sb_attention(qs, kst, vs, g2):
    s = qs.shape[0]
    tq, t = SB_Q_BLOCK, SB_K_BLOCK
    return pl.pallas_call(
        _sb_kernel,
        grid=(N_SB_HEADS // 2, s // tq),
        in_specs=[
            pl.BlockSpec((tq, LANES), lambda h, i: (i, h)),
            pl.BlockSpec((LANES, s), lambda h, i: (h, 0)),
            pl.BlockSpec((s, LANES), lambda h, i: (0, h)),
            pl.BlockSpec(g2.shape, lambda h, i: (0, 0)),
        ],
        out_specs=pl.BlockSpec((tq, LANES), lambda h, i: (i, h)),
        out_shape=jax.ShapeDtypeStruct((s, SB_WIDTH), BF16),
        scratch_shapes=[pltpu.VMEM((tq // t, 2, t, LANES), F32), pltpu.VMEM((tq // t, t, LANES), F32),
                        pltpu.VMEM((8, LANES), F32),
                        pltpu.SMEM((tq // t,), F32), pltpu.SMEM((tq // t,), jnp.int32)],
        compiler_params=pltpu.CompilerParams(
            dimension_semantics=("arbitrary", "arbitrary"), vmem_limit_bytes=VMEM_LIMIT),
        name="sb_attn",
    )(qs, kst, vs, g2)


def _gelu_tanh(x):
    return 0.5 * x * (1.0 + jnp.tanh(math.sqrt(2.0 / math.pi) * (x + 0.044715 * (x * x * x))))


def _mlp_kernel(d_ref, s_ref, x_ref, dh_ref, sh_ref, xh_ref, wo_ref, gattn_ref, gpre_ref,
                wup_ref, cw_ref, cb_ref, wd_ref, gffn_ref, o_ref, u_sc):
    i = pl.program_id(0)
    rows, chunk = MLP_ROWS, FFN_CHUNK
    stack = lambda main_ref, halo_ref: jnp.concatenate([main_ref[...], halo_ref[...]], axis=0)
    mix = (jnp.dot(stack(d_ref, dh_ref), wo_ref[:DIFF_WIDTH], preferred_element_type=F32)
           + jnp.dot(stack(s_ref, sh_ref), wo_ref[DIFF_WIDTH:], preferred_element_type=F32))
    x1 = stack(x_ref, xh_ref) + _rms(mix, gattn_ref[...])
    h = _rms(x1, gpre_ref[...])
    row = lax.broadcasted_iota(jnp.int32, h.shape, 0)
    lhs = jnp.where(jnp.logical_or(row < rows, i > 0), h, 0.0).astype(BF16)

    def conv(col, slot):
        cols = slice(col, col + chunk)
        u = jnp.dot(lhs, wup_ref[:, cols], preferred_element_type=F32)
        u_sc[slot, 0:HALO_ROWS, :] = u[rows:]
        u_sc[slot, HALO_ROWS:, :] = u[:rows]
        y = cb_ref[:, cols]
        for k in range(CONV_WIDTH):
            off = HALO_ROWS - (CONV_WIDTH - 1) + k
            y = y + u_sc[slot, off:off + rows, :] * cw_ref[k:k + 1, cols]
        return y

    n_chunk = D_FF // chunk
    gates = [conv(c * chunk, 2 * c) for c in range(n_chunk)]
    vals = [conv(D_FF + c * chunk, 2 * c + 1) for c in range(n_chunk)]
    y = None
    for c in range(n_chunk):
        act = (_gelu_tanh(gates[c]) * vals[c]).astype(BF16)
        part = jnp.dot(act, wd_ref[c * chunk:(c + 1) * chunk, :], preferred_element_type=F32)
        y = part if y is None else y + part
    o_ref[...] = x1[:rows] + _rms(y, gffn_ref[...])


def _mlp(diff_o, sb_o, x2, w_o, g_attn, g_pre, w_up, conv_w, conv_b, w_down, g_ffn):
    s = x2.shape[0]
    rows = MLP_ROWS
    halo_blocks = rows // HALO_ROWS
    row_spec = lambda w: pl.BlockSpec((rows, w), lambda i: (i, 0))
    halo_spec = lambda w: pl.BlockSpec((HALO_ROWS, w), lambda i: (jnp.maximum(i * halo_blocks - 1, 0), 0))
    resident = lambda a: pl.BlockSpec(a.shape, lambda i: (0, 0), pipeline_mode=pl.Buffered(1))
    return pl.pallas_call(
        _mlp_kernel,
        grid=(s // rows,),
        in_specs=[row_spec(DIFF_WIDTH), row_spec(SB_WIDTH), row_spec(D_MODEL),
                  halo_spec(DIFF_WIDTH), halo_spec(SB_WIDTH), halo_spec(D_MODEL),
                  resident(w_o), resident(g_attn), resident(g_pre),
                  resident(w_up), resident(conv_w), resident(conv_b), resident(w_down), resident(g_ffn)],
        out_specs=row_spec(D_MODEL),
        out_shape=jax.ShapeDtypeStruct((s, D_MODEL), F32),
        scratch_shapes=[pltpu.VMEM((2 * (D_FF // FFN_CHUNK), HALO_ROWS + rows, FFN_CHUNK), F32)],
        compiler_params=pltpu.CompilerParams(
            dimension_semantics=("arbitrary",), vmem_limit_bytes=MLP_VMEM_LIMIT),
        name="mlp",
    )(diff_o, sb_o, x2, diff_o, sb_o, x2, w_o, g_attn, g_pre, w_up, conv_w, conv_b, w_down, g_ffn)


def kernel(x, attn_pre_norm, w_qkv, lambda_q1, lambda_k1, lambda_q2, lambda_k2, diff_subln, sb_norm,
           rel_bias, w_o, attn_post_norm, ffn_pre_norm, w_up, conv_w, conv_b, w_down, ffn_post_norm):
    b, s, _ = x.shape
    assert b == 1 and attn_pre_norm.shape[0] == 1, "single sequence, single layer"
    assert s % max(PROJ_ROWS, 2 * DIFF_BLOCK, SB_Q_BLOCK, MLP_ROWS) == 0
    x2 = x[0]
    row = lambda a: a.reshape(1, -1).astype(F32)

    qd, vd, qs, vs, kdt, kst = _proj(x2, row(attn_pre_norm), w_qkv[0].astype(F32))

    bias = _bias_tiles(rel_bias.astype(F32))
    lam_params = jnp.concatenate([lambda_q1, lambda_k1, lambda_q2, lambda_k2], axis=0).astype(F32)
    diff_o = _diff_attention(lam_params, qd, kdt, vd, bias, row(diff_subln))
    sb_o = _sb_attention(qs, kst, vs, row(jnp.concatenate([sb_norm[0], sb_norm[0]])))

    out = _mlp(diff_o, sb_o, x2, w_o[0].astype(BF16), row(attn_post_norm), row(ffn_pre_norm),
               w_up[0].astype(BF16), conv_w[0].astype(F32), conv_b.astype(F32), w_down[0].astype(BF16),
               row(ffn_post_norm))
    return out[None]
```

```python
import math

import jax
import jax.numpy as jnp
from jax import lax
from jax.experimental import pallas as pl
from jax.experimental.pallas import tpu as pltpu

F32 = jnp.float32
BF16 = jnp.bfloat16

D_MODEL = 1024
HEAD_DIM = 64
N_DIFF_HEADS = 4
N_SB_HEADS = 8
DIFF_WIDTH = N_DIFF_HEADS * 2 * HEAD_DIM
SB_WIDTH = N_SB_HEADS * HEAD_DIM
QKV_GROUP = DIFF_WIDTH
assert SB_WIDTH == QKV_GROUP
D_FF = 2816
CONV_WIDTH = 3
N_BUCKETS = 32
MAX_DISTANCE = 128
RMS_EPS = 1e-6
LAMBDA_INIT = 0.8 - 0.6 * math.exp(-0.3 * 0)

LOG2E = 1.4426950408889634
Q_SCALE = HEAD_DIM ** -0.5 * LOG2E
MASK_VALUE = -1e30
EXP2_UNDERFLOW = 160.0
NORM_BOUND_SLACK = 1.0 + 2.0 ** -10
SOFTMAX_MIN_DENOMINATOR = 2.0 ** -60

LANES = 128
PROJ_ROWS = 1024
DIFF_BLOCK = 512
SB_Q_BLOCK = 4096
SB_K_BLOCK = 256
MLP_ROWS = 512
FFN_CHUNK = D_FF // 2
HALO_ROWS = 16
VMEM_LIMIT = 48 * 1024 * 1024
DIFF_VMEM_LIMIT = 58 * 1024 * 1024
MLP_VMEM_LIMIT = 58 * 1024 * 1024


def _tile_lanes(x, k):
    return jnp.concatenate([x] * k, axis=1) if k > 1 else x


def _rms(x, g):
    return x * lax.rsqrt(jnp.mean(x * x, axis=-1, keepdims=True) + RMS_EPS) * g


def _proj_kernel(x_ref, g_ref, w_ref, wup_ref, wdown_ref,
                 qd_ref, vd_ref, qs_ref, vs_ref, kdt_ref, kst_ref, wup_bf_ref, wdown_bf_ref):
    hb = _rms(x_ref[...], g_ref[...])
    proj = lambda c: jnp.dot(hb, w_ref[:, c * QKV_GROUP:(c + 1) * QKV_GROUP], preferred_element_type=F32)
    qd_ref[...] = (proj(0) * Q_SCALE).astype(BF16)
    vd_ref[...] = proj(2).astype(BF16)
    qs_ref[...] = (proj(3) * Q_SCALE).astype(BF16)
    vs_ref[...] = proj(5).astype(BF16)
    proj_t = lambda c: lax.dot_general(w_ref[:, c * QKV_GROUP:(c + 1) * QKV_GROUP], hb, (((0,), (1,)), ((), ())),
                                       preferred_element_type=F32)
    kdt_ref[...] = proj_t(1).astype(BF16)
    kst_ref[...] = proj_t(4).astype(BF16)
    wup_bf_ref[...] = wup_ref[...].astype(BF16)
    wdown_bf_ref[...] = wdown_ref[...].astype(BF16)


def _proj(x2, g, w, w_up, w_down):
    s = x2.shape[0]
    rows = PROJ_ROWS
    steps = s // rows
    row_spec = lambda w: pl.BlockSpec((rows, w), lambda i: (i, 0))
    full = lambda a: pl.BlockSpec(a.shape, lambda i: (0, 0))
    slab_spec = lambda a: pl.BlockSpec((a.shape[0] // steps, a.shape[1]), lambda i: (i, 0))
    out_rows = jax.ShapeDtypeStruct((s, QKV_GROUP), BF16)
    out_t = jax.ShapeDtypeStruct((QKV_GROUP, s), BF16)
    return pl.pallas_call(
        _proj_kernel,
        grid=(steps,),
        in_specs=[row_spec(D_MODEL), full(g),
                  pl.BlockSpec(w.shape, lambda i: (0, 0), pipeline_mode=pl.Buffered(1)),
                  slab_spec(w_up), slab_spec(w_down)],
        out_specs=[row_spec(QKV_GROUP)] * 4 + [pl.BlockSpec((QKV_GROUP, rows), lambda i: (0, i))] * 2
        + [slab_spec(w_up), slab_spec(w_down)],
        out_shape=[out_rows] * 4 + [out_t] * 2
        + [jax.ShapeDtypeStruct(w_up.shape, BF16), jax.ShapeDtypeStruct(w_down.shape, BF16)],
        compiler_params=pltpu.CompilerParams(
            dimension_semantics=("arbitrary",), vmem_limit_bytes=VMEM_LIMIT),
        name="proj",
    )(x2, g, w, w_up, w_down)


def _bias_kernel(tab_ref, o_ref):
    h = pl.program_id(0)
    w = pl.program_id(1)
    t, sub = DIFF_BLOCK, LANES

    def sub_tile(offset):
        rel = (lax.broadcasted_iota(jnp.int32, (sub, sub), 0)
               - lax.broadcasted_iota(jnp.int32, (sub, sub), 1) + offset)
        n = jnp.maximum(rel, 0)
        max_exact = N_BUCKETS // 2
        nf = jnp.maximum(n, 1).astype(F32)
        large = max_exact + jnp.floor(jnp.log(nf / max_exact) / math.log(MAX_DISTANCE / max_exact)
                                      * (N_BUCKETS - max_exact)).astype(jnp.int32)
        large = jnp.minimum(large, N_BUCKETS - 1)
        bucket = jnp.where(n < max_exact, n, large)
        b = jnp.zeros((sub, sub), F32)
        for k in range(N_BUCKETS):
            b = jnp.where(bucket == k, tab_ref[k, h], b)
        b = (b - tab_ref[N_BUCKETS - 1, h]) * LOG2E
        return jnp.where(rel >= 0, b, MASK_VALUE)

    def tile(first_offset):
        for br in range(t // sub):
            for bc in range(t // sub):
                offset = first_offset + (br - bc) * sub
                if offset + sub <= 0:
                    val = jnp.full((sub, sub), MASK_VALUE, F32)
                elif offset - sub >= MAX_DISTANCE:
                    val = jnp.zeros((sub, sub), F32)
                else:
                    val = sub_tile(offset)
                o_ref[0, 0, br * sub:(br + 1) * sub, bc * sub:(bc + 1) * sub] = val

    @pl.when(w == 0)
    def _():
        tile(0)

    @pl.when(w == 1)
    def _():
        tile(t)


def _bias_tiles(rel_bias):
    t = DIFF_BLOCK
    return pl.pallas_call(
        _bias_kernel,
        grid=(N_DIFF_HEADS, 2),
        in_specs=[pl.BlockSpec(memory_space=pltpu.SMEM)],
        out_specs=pl.BlockSpec((1, 1, t, t), lambda h, w: (h, w, 0, 0)),
        out_shape=jax.ShapeDtypeStruct((N_DIFF_HEADS, 2, t, t), F32),
        compiler_params=pltpu.CompilerParams(
            dimension_semantics=("arbitrary", "arbitrary"), vmem_limit_bytes=VMEM_LIMIT),
        name="bias",
    )(rel_bias)


def _diff_kernel(lam_ref, q_ref, kt_ref, v_ref, bias_ref, g_ref, o_ref,
                 acc_sc, stat_sc, s_sc, m_sc, l_sc, slow_acc_sc, ok_sm):
    i = pl.program_id(1)
    t = DIFF_BLOCK
    tq = 2 * t
    half_a, half_b = slice(0, t), slice(t, tq)
    lane = lax.broadcasted_iota(jnp.int32, (tq, LANES), 1)
    q = q_ref[...]
    n_pairs = jnp.maximum(i - 1, 0)

    @pl.when(i == 0)
    def _():
        def norm_body(c, ms):
            k = kt_ref[:, pl.ds(pl.multiple_of(c * t, t), t)].astype(F32)
            sq = k * k
            return (jnp.maximum(ms[0], jnp.sum(sq[:HEAD_DIM], axis=0, keepdims=True)),
                    jnp.maximum(ms[1], jnp.sum(sq[HEAD_DIM:], axis=0, keepdims=True)))

        zeros = jnp.zeros((1, t), F32)
        ms = lax.fori_loop(0, kt_ref.shape[1] // t, norm_body, (zeros, zeros))
        for mp in range(2):
            stat_sc[mp] = jnp.broadcast_to(jnp.max(ms[mp], axis=1, keepdims=True), stat_sc.shape[1:])
        bias_max = jnp.max(jnp.max(bias_ref[0, 0], axis=1, keepdims=True), axis=0, keepdims=True)
        stat_sc[2] = jnp.broadcast_to(bias_max, stat_sc.shape[1:])

    def write_out(rows, o0, o1):
        lp = lam_ref[...]
        lam = (jnp.exp(jnp.sum(lp[0:1] * lp[1:2], axis=1, keepdims=True))
               - jnp.exp(jnp.sum(lp[2:3] * lp[3:4], axis=1, keepdims=True)) + LAMBDA_INIT)
        o_ref[rows, :] = (_rms(o0 - lam * o1, g_ref[...]) * (1.0 - LAMBDA_INIT)).astype(o_ref.dtype)

    every = slice(0, tq)
    pair = 2 * t
    zero_q = jnp.zeros_like(q)
    qm = (jnp.where(lane < HEAD_DIM, q, zero_q), jnp.where(lane >= HEAD_DIM, q, zero_q))
    bias_diag, bias_prev = bias_ref[0, 0], bias_ref[0, 1]

    def fast_path(first_block, far_pairs):
        qsq = q.astype(F32) * q.astype(F32)
        bound = []
        for mp in range(2):
            own = (lane < HEAD_DIM) if mp == 0 else (lane >= HEAD_DIM)
            qn = jnp.sum(jnp.where(own, qsq, 0.0), axis=1, keepdims=True)
            b = jnp.sqrt(qn * stat_sc[mp][0:1, 0:1]) * NORM_BOUND_SLACK + stat_sc[2][0:1, 0:1]
            bound.append(jnp.broadcast_to(b, (tq, LANES)))

        acc_sc[...] = jnp.zeros(acc_sc.shape, F32)

        def scores(rows, start, width):
            kt = kt_ref[:, pl.ds(start, width)]
            return [jnp.dot(qm[mp][rows], kt, preferred_element_type=F32)
                    - _tile_lanes(bound[mp][rows], width // LANES) for mp in range(2)]

        def accumulate(rows, s, start, width):
            v_aug = jnp.concatenate([v_ref[pl.ds(start, width), :], jnp.ones((width, LANES), BF16)],
                                    axis=1)
            for mp in range(2):
                acc_sc[mp, rows, :] += jnp.dot(jnp.exp2(s[mp]).astype(BF16), v_aug,
                                               preferred_element_type=F32)

        def last_block_of_b(block):
            start = pl.multiple_of(block * t, t)
            s = [x + bias_diag for x in scores(half_b, start, t)]
            accumulate(half_b, s, start, t)

        if far_pairs is not None:
            first = scores(every, 0, pair)
            for mp in range(2):
                s_sc[mp] = first[mp]

            def far_body(r, carry):
                cur = [s_sc[mp] for mp in range(2)]
                nxt = scores(every, pl.multiple_of((r + 1) * pair, pair), pair)
                accumulate(every, cur, pl.multiple_of(r * pair, pair), pair)
                for mp in range(2):
                    s_sc[mp] = nxt[mp]
                return carry

            lax.fori_loop(0, far_pairs - 1, far_body, 0)
            accumulate(every, [s_sc[mp] for mp in range(2)],
                       pl.multiple_of((far_pairs - 1) * pair, pair), pair)

        if first_block:
            s = [x + jnp.concatenate([bias_diag, bias_prev], axis=0) for x in scores(every, 0, t)]
            accumulate(every, s, 0, t)
            last_block_of_b(1)
        else:
            start = pl.multiple_of((2 * i - 2) * t, t)
            s = scores(every, start, 3 * t)
            for mp in range(2):
                x = s[mp]
                s[mp] = jnp.concatenate([
                    x[:, :t],
                    jnp.concatenate([x[half_a, t:2 * t] + bias_prev, x[half_b, t:2 * t]], axis=0),
                    jnp.concatenate([x[half_a, 2 * t:] + bias_diag, x[half_b, 2 * t:] + bias_prev], axis=0),
                ], axis=1)
            accumulate(every, s, start, 3 * t)
            last_block_of_b(2 * i + 1)

        l0, l1 = acc_sc[0][:, LANES:], acc_sc[1][:, LANES:]
        write_out(every, acc_sc[0][:, :LANES] / l0, acc_sc[1][:, :LANES] / l1)
        ok_sm[0] = (jnp.min(jnp.minimum(l0, l1)) >= SOFTMAX_MIN_DENOMINATOR).astype(jnp.int32)

    @pl.when(i == 0)
    def _():
        fast_path(True, None)

    @pl.when(i == 1)
    def _():
        fast_path(False, None)

    @pl.when(i >= 2)
    def _():
        fast_path(False, n_pairs)

    @pl.when(ok_sm[0] == 0)
    def _():
        def online_softmax_half(rows, diag_block):
            m_sc[...] = jnp.full(m_sc.shape, MASK_VALUE, F32)
            l_sc[...] = jnp.zeros(l_sc.shape, F32)
            slow_acc_sc[...] = jnp.zeros(slow_acc_sc.shape, F32)

            def step(j, bias):
                start = pl.multiple_of(j * t, t)
                kt = kt_ref[:, pl.ds(start, t)]
                v = v_ref[pl.ds(start, t), :]
                for mp in range(2):
                    s = jnp.dot(qm[mp][rows], kt, preferred_element_type=F32)
                    if bias is not None:
                        s = s + bias
                    m_prev = m_sc[mp]
                    m_new = jnp.maximum(m_prev, jnp.max(s, axis=1, keepdims=True))
                    alpha = jnp.exp2(m_prev - m_new)
                    p = jnp.exp2(s - _tile_lanes(m_new, t // LANES))
                    l_sc[mp] = alpha * l_sc[mp] + jnp.sum(p, axis=1, keepdims=True)
                    slow_acc_sc[mp] = (alpha * slow_acc_sc[mp]
                                       + jnp.dot(p.astype(BF16), v, preferred_element_type=F32))
                    m_sc[mp] = m_new

            def slow_body(j, carry):
                step(j, None)
                return carry

            lax.fori_loop(0, jnp.maximum(diag_block - 1, 0), slow_body, 0)

            @pl.when(diag_block >= 1)
            def _():
                step(diag_block - 1, bias_prev)

            step(diag_block, bias_diag)
            write_out(rows, slow_acc_sc[0] / l_sc[0], slow_acc_sc[1] / l_sc[1])

        online_softmax_half(half_a, 2 * i)
        online_softmax_half(half_b, 2 * i + 1)


def _diff_attention(lam_params, qd, kdt, vd, bias, g):
    s = qd.shape[0]
    t = DIFF_BLOCK
    tq = 2 * t
    return pl.pallas_call(
        _diff_kernel,
        grid=(N_DIFF_HEADS, s // tq),
        in_specs=[
            pl.BlockSpec(lam_params.shape, lambda h, i: (0, 0)),
            pl.BlockSpec((tq, LANES), lambda h, i: (i, h)),
            pl.BlockSpec((LANES, s), lambda h, i: (h, 0)),
            pl.BlockSpec((s, LANES), lambda h, i: (0, h)),
            pl.BlockSpec((1, 2, t, t), lambda h, i: (h, 0, 0, 0)),
            pl.BlockSpec(g.shape, lambda h, i: (0, 0)),
        ],
        out_specs=pl.BlockSpec((tq, LANES), lambda h, i: (i, h)),
        out_shape=jax.ShapeDtypeStruct((s, DIFF_WIDTH), BF16),
        scratch_shapes=[pltpu.VMEM((2, tq, 2 * LANES), F32), pltpu.VMEM((3, 8, LANES), F32),
                        pltpu.VMEM((2, tq, 2 * t), F32)]
        + [pltpu.VMEM((2, t, LANES), F32)] * 3 + [pltpu.SMEM((1,), jnp.int32)],
        compiler_params=pltpu.CompilerParams(
            dimension_semantics=("arbitrary", "arbitrary"), vmem_limit_bytes=DIFF_VMEM_LIMIT),
        name="diff_attn",
    )(lam_params, qd, kdt, vd, bias, g)


def _sb_kernel(q_ref, kt_ref, v_ref, g_ref, o_ref, carry_sc, acc_sc, knorm_sc, thresh_sm, done_sm):
    i = pl.program_id(1)
    t = SB_K_BLOCK
    n_sub = SB_Q_BLOCK // t
    lane = lax.broadcasted_iota(jnp.int32, (t, LANES), 1)
    head0 = lane < HEAD_DIM
    lower = (lax.broadcasted_iota(jnp.int32, (t, t), 0)
             >= lax.broadcasted_iota(jnp.int32, (t, t), 1)).astype(BF16)
    strict = lax.broadcasted_iota(jnp.int32, (t, t), 0) > lax.broadcasted_iota(jnp.int32, (t, t), 1)

    @pl.when(i == 0)
    def _():
        def norm_body(c, m):
            k = kt_ref[:, pl.ds(pl.multiple_of(c * t, t), t)].astype(F32)
            sq = k * k
            return jnp.maximum(m, jnp.maximum(jnp.sum(sq[:HEAD_DIM], axis=0, keepdims=True),
                                              jnp.sum(sq[HEAD_DIM:], axis=0, keepdims=True)))

        m = lax.fori_loop(0, kt_ref.shape[1] // t, norm_body, jnp.zeros((1, t), F32))
        knorm_sc[...] = jnp.broadcast_to(jnp.max(m, axis=1, keepdims=True), knorm_sc.shape)

    def load_q(u):
        q = q_ref[pl.ds(pl.multiple_of(u * t, t), t), :]
        zero_q = jnp.zeros_like(q)
        return q, (jnp.where(head0, q, zero_q), jnp.where(head0, zero_q, q))

    def block_pass(jb, consumers):
        start = pl.multiple_of(jb * t, t)
        kt = kt_ref[:, pl.ds(start, t)]
        v = v_ref[pl.ds(start, t), :]
        zero_v = jnp.zeros_like(v)
        v2 = jnp.concatenate([jnp.where(head0, v, zero_v), jnp.where(head0, zero_v, v)], axis=0)
        q_rows = [qm[hh] for _, qm, _ in consumers for hh in range(2)]
        z = jnp.dot(jnp.concatenate(q_rows, axis=0), kt, preferred_element_type=F32)
        nl = jnp.maximum(z, 0.0) + jnp.log(1.0 + jnp.exp2(-jnp.abs(z))) * LOG2E
        segments = [slice(n * t, (n + 1) * t) for n in range(len(q_rows))]
        masked = [m for _, _, m in consumers for _ in range(2)]
        nl = jnp.concatenate([jnp.where(strict, nl[seg], 0.0) if m else nl[seg]
                              for seg, m in zip(segments, masked)], axis=0)
        csum = jnp.dot(nl.astype(BF16), lower, preferred_element_type=F32)
        probs = []
        for n, (u, _, m) in enumerate(consumers):
            heads = []
            for hh in range(2):
                seg = segments[2 * n + hh]
                if m:
                    a = jnp.where(strict, jnp.exp2(z[seg] - csum[seg]), 0.0)
                    carry_sc[u, hh] = jnp.broadcast_to(csum[seg][:, 0:1], (t, LANES))
                else:
                    carry = carry_sc[u, hh]
                    a = jnp.exp2((z[seg] - _tile_lanes(carry, t // LANES)) - csum[seg])
                    carry_sc[u, hh] = carry + csum[seg][:, 0:1]
                heads.append(a.astype(BF16))
            probs.append(jnp.concatenate(heads, axis=1))
        pv = jnp.dot(jnp.concatenate(probs, axis=0), v2, preferred_element_type=F32)
        for n, (u, _, m) in enumerate(consumers):
            if m:
                acc_sc[u] = pv[n * t:(n + 1) * t]
            else:
                acc_sc[u] += pv[n * t:(n + 1) * t]

    qs = [load_q(u) for u in range(n_sub)]

    qmax = jnp.zeros((t, LANES), F32)
    for u in range(n_sub):
        qmax = jnp.maximum(qmax, jnp.abs(qs[u][0].astype(F32)))
    qmax = jnp.max(jnp.max(qmax, axis=1, keepdims=True), axis=0, keepdims=True)
    thresh = jnp.sqrt(HEAD_DIM * qmax * qmax * knorm_sc[0:1, 0:1]) * NORM_BOUND_SLACK + EXP2_UNDERFLOW
    thresh_sm[0] = jnp.max(thresh)

    def near_passes(has_previous_block):
        for u in range(n_sub - 1, -1, -1):
            consumers = [(u, qs[u][1], True)]
            if u + 1 < n_sub:
                consumers.append((u + 1, qs[u + 1][1], False))
            block_pass(i * n_sub + u, consumers)
        if has_previous_block:
            block_pass(i * n_sub - 1, [(0, qs[0][1], False)])

    @pl.when(i == 0)
    def _():
        near_passes(False)

    @pl.when(i >= 1)
    def _():
        near_passes(True)

    def carry_min(u):
        return jnp.min(jnp.minimum(carry_sc[u, 0], carry_sc[u, 1]))

    for u in range(n_sub):
        done_sm[u] = (carry_min(u) >= thresh_sm[0]).astype(jnp.int32)

    def remaining_blocks(u, c):
        _, qm = load_q(u)

        def far_cond(state):
            jb, all_zero = state
            return jnp.logical_and(jb >= 0, all_zero == 0)

        def far_body(state):
            jb, _ = state
            block_pass(jb, [(u, qm, False)])
            return jb - 1, (carry_min(u) >= thresh_sm[0]).astype(jnp.int32)

        lax.while_loop(far_cond, far_body, (i * n_sub + u - 2, done_sm[u]))
        return c

    lax.fori_loop(0, n_sub, remaining_blocks, 0)

    for u in range(n_sub):
        o = acc_sc[u]
        sq = o * o
        ss0 = jnp.sum(jnp.where(head0, sq, 0.0), axis=1, keepdims=True)
        ss1 = jnp.sum(jnp.where(head0, 0.0, sq), axis=1, keepdims=True)
        ms = jnp.where(head0, ss0, ss1) * (1.0 / HEAD_DIM)
        o_ref[u * t:(u + 1) * t, :] = (o * lax.rsqrt(ms + RMS_EPS) * g_ref[...]).astype(o_ref.dtype)


def _sb_attention(qs, kst, vs, g2):
    s = qs.shape[0]
    tq, t = SB_Q_BLOCK, SB_K_BLOCK
    return pl.pallas_call(
        _sb_kernel,
        grid=(N_SB_HEADS // 2, s // tq),
        in_specs=[
            pl.BlockSpec((tq, LANES), lambda h, i: (i, h)),
            pl.BlockSpec((LANES, s), lambda h, i: (h, 0)),
            pl.BlockSpec((s, LANES), lambda h, i: (0, h)),
            pl.BlockSpec(g2.shape, lambda h, i: (0, 0)),
        ],
        out_specs=pl.BlockSpec((tq, LANES), lambda h, i: (i, h)),
        out_shape=jax.ShapeDtypeStruct((s, SB_WIDTH), BF16),
        scratch_shapes=[pltpu.VMEM((tq // t, 2, t, LANES), F32), pltpu.VMEM((tq // t, t, LANES), F32),
                        pltpu.VMEM((8, LANES), F32),
                        pltpu.SMEM((tq // t,), F32), pltpu.SMEM((tq // t,), jnp.int32)],
        compiler_params=pltpu.CompilerParams(
            dimension_semantics=("arbitrary", "arbitrary"), vmem_limit_bytes=VMEM_LIMIT),
        name="sb_attn",
    )(qs, kst, vs, g2)


def _gelu_tanh(x):
    return 0.5 * x * (1.0 + jnp.tanh(math.sqrt(2.0 / math.pi) * (x + 0.044715 * (x * x * x))))


def _mlp_kernel(d_ref, s_ref, x_ref, dh_ref, sh_ref, xh_ref, wo_ref, gattn_ref, gpre_ref,
                wup_ref, cw_ref, cb_ref, wd_ref, gffn_ref, o_ref, u_sc):
    i = pl.program_id(0)
    rows, chunk = MLP_ROWS, FFN_CHUNK
    stack = lambda main_ref, halo_ref: jnp.concatenate([main_ref[...], halo_ref[...]], axis=0)
    mix = (jnp.dot(stack(d_ref, dh_ref), wo_ref[:DIFF_WIDTH], preferred_element_type=F32)
           + jnp.dot(stack(s_ref, sh_ref), wo_ref[DIFF_WIDTH:], preferred_element_type=F32))
    x1 = stack(x_ref, xh_ref) + _rms(mix, gattn_ref[...])
    h = _rms(x1, gpre_ref[...])
    row = lax.broadcasted_iota(jnp.int32, h.shape, 0)
    lhs = jnp.where(jnp.logical_or(row < rows, i > 0), h, 0.0).astype(BF16)

    def conv(col, slot):
        cols = slice(col, col + chunk)
        u = jnp.dot(lhs, wup_ref[:, cols], preferred_element_type=F32)
        u_sc[slot, 0:HALO_ROWS, :] = u[rows:]
        u_sc[slot, HALO_ROWS:, :] = u[:rows]
        y = cb_ref[:, cols]
        for k in range(CONV_WIDTH):
            off = HALO_ROWS - (CONV_WIDTH - 1) + k
            y = y + u_sc[slot, off:off + rows, :] * cw_ref[k:k + 1, cols]
        return y

    n_chunk = D_FF // chunk
    gates = [conv(c * chunk, 2 * c) for c in range(n_chunk)]
    vals = [conv(D_FF + c * chunk, 2 * c + 1) for c in range(n_chunk)]
    y = None
    for c in range(n_chunk):
        act = (_gelu_tanh(gates[c]) * vals[c]).astype(BF16)
        part = jnp.dot(act, wd_ref[c * chunk:(c + 1) * chunk, :], preferred_element_type=F32)
        y = part if y is None else y + part
    o_ref[...] = x1[:rows] + _rms(y, gffn_ref[...])


def _mlp(diff_o, sb_o, x2, w_o, g_attn, g_pre, w_up, conv_w, conv_b, w_down, g_ffn):
    s = x2.shape[0]
    rows = MLP_ROWS
    halo_blocks = rows // HALO_ROWS
    row_spec = lambda w: pl.BlockSpec((rows, w), lambda i: (i, 0))
    halo_spec = lambda w: pl.BlockSpec((HALO_ROWS, w), lambda i: (jnp.maximum(i * halo_blocks - 1, 0), 0))
    resident = lambda a: pl.BlockSpec(a.shape, lambda i: (0, 0), pipeline_mode=pl.Buffered(1))
    return pl.pallas_call(
        _mlp_kernel,
        grid=(s // rows,),
        in_specs=[row_spec(DIFF_WIDTH), row_spec(SB_WIDTH), row_spec(D_MODEL),
                  halo_spec(DIFF_WIDTH), halo_spec(SB_WIDTH), halo_spec(D_MODEL),
                  resident(w_o), resident(g_attn), resident(g_pre),
                  resident(w_up), resident(conv_w), resident(conv_b), resident(w_down), resident(g_ffn)],
        out_specs=row_spec(D_MODEL),
        out_shape=jax.ShapeDtypeStruct((s, D_MODEL), F32),
        scratch_shapes=[pltpu.VMEM((2 * (D_FF // FFN_CHUNK), HALO_ROWS + rows, FFN_CHUNK), F32)],
        compiler_params=pltpu.CompilerParams(
            dimension_semantics=("arbitrary",), vmem_limit_bytes=MLP_VMEM_LIMIT),
        name="mlp",
    )(diff_o, sb_o, x2, diff_o, sb_o, x2, w_o, g_attn, g_pre, w_up, conv_w, conv_b, w_down, g_ffn)


def kernel(x, attn_pre_norm, w_qkv, lambda_q1, lambda_k1, lambda_q2, lambda_k2, diff_subln, sb_norm,
           rel_bias, w_o, attn_post_norm, ffn_pre_norm, w_up, conv_w, conv_b, w_down, ffn_post_norm):
    b, s, _ = x.shape
    assert b == 1 and attn_pre_norm.shape[0] == 1, "single sequence, single layer"
    assert s % max(PROJ_ROWS, 2 * DIFF_BLOCK, SB_Q_BLOCK, MLP_ROWS) == 0
    x2 = x[0]
    row = lambda a: a.reshape(1, -1).astype(F32)

    qd, vd, qs, vs, kdt, kst, w_up_bf, w_down_bf = _proj(
        x2, row(attn_pre_norm), w_qkv[0].astype(F32), w_up[0].astype(F32),
        w_down[0].astype(F32).reshape(D_MODEL, D_FF))

    bias = _bias_tiles(rel_bias.astype(F32))
    lam_params = jnp.concatenate([lambda_q1, lambda_k1, lambda_q2, lambda_k2], axis=0).astype(F32)
    diff_o = _diff_attention(lam_params, qd, kdt, vd, bias, row(diff_subln))
    sb_o = _sb_attention(qs, kst, vs, row(jnp.concatenate([sb_norm[0], sb_norm[0]])))

    out = _mlp(diff_o, sb_o, x2, w_o[0].astype(BF16), row(attn_post_norm), row(ffn_pre_norm),
               w_up_bf, conv_w[0].astype(F32), conv_b.astype(F32), w_down_bf.reshape(D_FF, D_MODEL),
               row(ffn_post_norm))
    return out[None]
```

```python
import math

import jax
import jax.numpy as jnp
from jax import lax
from jax.experimental import pallas as pl
from jax.experimental.pallas import tpu as pltpu

F32 = jnp.float32
BF16 = jnp.bfloat16

D_MODEL = 1024
HEAD_DIM = 64
N_DIFF_HEADS = 4
N_SB_HEADS = 8
DIFF_WIDTH = N_DIFF_HEADS * 2 * HEAD_DIM
SB_WIDTH = N_SB_HEADS * HEAD_DIM
QKV_GROUP = DIFF_WIDTH
assert SB_WIDTH == QKV_GROUP
D_FF = 2816
CONV_WIDTH = 3
N_BUCKETS = 32
MAX_DISTANCE = 128
RMS_EPS = 1e-6
LAMBDA_INIT = 0.8 - 0.6 * math.exp(-0.3 * 0)

LOG2E = 1.4426950408889634
Q_SCALE = HEAD_DIM ** -0.5 * LOG2E
MASK_VALUE = -1e30
EXP2_UNDERFLOW = 160.0
NORM_BOUND_SLACK = 1.0 + 2.0 ** -10
SOFTMAX_MIN_DENOMINATOR = 2.0 ** -60

LANES = 128
PROJ_ROWS = 1024
DIFF_BLOCK = 512
SB_Q_BLOCK = 4096
SB_K_BLOCK = 256
MLP_ROWS = 512
FFN_CHUNK = D_FF // 2
HALO_ROWS = 16
VMEM_LIMIT = 48 * 1024 * 1024
DIFF_VMEM_LIMIT = 58 * 1024 * 1024
MLP_VMEM_LIMIT = 58 * 1024 * 1024


def _tile_lanes(x, k):
    return jnp.concatenate([x] * k, axis=1) if k > 1 else x


def _rms(x, g):
    return x * lax.rsqrt(jnp.mean(x * x, axis=-1, keepdims=True) + RMS_EPS) * g


def _proj_kernel(x_ref, g_ref, w_ref, wup_ref, wdown_ref,
                 qd_ref, vd_ref, qs_ref, vs_ref, kdt_ref, kst_ref, wup_bf_ref, wdown_bf_ref):
    hb = _rms(x_ref[...], g_ref[...])
    proj = lambda c: jnp.dot(hb, w_ref[:, c * QKV_GROUP:(c + 1) * QKV_GROUP], preferred_element_type=F32)
    qd_ref[...] = (proj(0) * Q_SCALE).astype(BF16)
    vd_ref[...] = proj(2).astype(BF16)
    qs_ref[...] = (proj(3) * Q_SCALE).astype(BF16)
    vs_ref[...] = proj(5).astype(BF16)
    proj_t = lambda c: lax.dot_general(w_ref[:, c * QKV_GROUP:(c + 1) * QKV_GROUP], hb, (((0,), (1,)), ((), ())),
                                       preferred_element_type=F32)
    kdt_ref[...] = proj_t(1).astype(BF16)
    kst_ref[...] = proj_t(4).astype(BF16)
    wup_bf_ref[...] = wup_ref[...].astype(BF16)
    wdown_bf_ref[...] = wdown_ref[...].astype(BF16)


def _proj(x2, g, w, w_up, w_down):
    s = x2.shape[0]
    rows = PROJ_ROWS
    steps = s // rows
    row_spec = lambda w: pl.BlockSpec((rows, w), lambda i: (i, 0))
    full = lambda a: pl.BlockSpec(a.shape, lambda i: (0, 0))
    slab_spec = lambda a: pl.BlockSpec((None, a.shape[1] // steps, a.shape[2]), lambda i: (0, i, 0))
    out_rows = jax.ShapeDtypeStruct((s, QKV_GROUP), BF16)
    out_t = jax.ShapeDtypeStruct((QKV_GROUP, s), BF16)
    return pl.pallas_call(
        _proj_kernel,
        grid=(steps,),
        in_specs=[row_spec(D_MODEL), full(g),
                  pl.BlockSpec(w.shape, lambda i: (0, 0), pipeline_mode=pl.Buffered(1)),
                  slab_spec(w_up), slab_spec(w_down)],
        out_specs=[row_spec(QKV_GROUP)] * 4 + [pl.BlockSpec((QKV_GROUP, rows), lambda i: (0, i))] * 2
        + [slab_spec(w_up), slab_spec(w_down)],
        out_shape=[out_rows] * 4 + [out_t] * 2
        + [jax.ShapeDtypeStruct(w_up.shape, BF16), jax.ShapeDtypeStruct(w_down.shape, BF16)],
        compiler_params=pltpu.CompilerParams(
            dimension_semantics=("arbitrary",), vmem_limit_bytes=VMEM_LIMIT),
        name="proj",
    )(x2, g, w, w_up, w_down)


def _bias_kernel(tab_ref, o_ref):
    h = pl.program_id(0)
    w = pl.program_id(1)
    t, sub = DIFF_BLOCK, LANES

    def sub_tile(offset):
        rel = (lax.broadcasted_iota(jnp.int32, (sub, sub), 0)
               - lax.broadcasted_iota(jnp.int32, (sub, sub), 1) + offset)
        n = jnp.maximum(rel, 0)
        max_exact = N_BUCKETS // 2
        nf = jnp.maximum(n, 1).astype(F32)
        large = max_exact + jnp.floor(jnp.log(nf / max_exact) / math.log(MAX_DISTANCE / max_exact)
                                      * (N_BUCKETS - max_exact)).astype(jnp.int32)
        large = jnp.minimum(large, N_BUCKETS - 1)
        bucket = jnp.where(n < max_exact, n, large)
        b = jnp.zeros((sub, sub), F32)
        for k in range(N_BUCKETS):
            b = jnp.where(bucket == k, tab_ref[k, h], b)
        b = (b - tab_ref[N_BUCKETS - 1, h]) * LOG2E
        return jnp.where(rel >= 0, b, MASK_VALUE)

    def tile(first_offset):
        for br in range(t // sub):
            for bc in range(t // sub):
                offset = first_offset + (br - bc) * sub
                if offset + sub <= 0:
                    val = jnp.full((sub, sub), MASK_VALUE, F32)
                elif offset - sub >= MAX_DISTANCE:
                    val = jnp.zeros((sub, sub), F32)
                else:
                    val = sub_tile(offset)
                o_ref[0, 0, br * sub:(br + 1) * sub, bc * sub:(bc + 1) * sub] = val

    @pl.when(w == 0)
    def _():
        tile(0)

    @pl.when(w == 1)
    def _():
        tile(t)


def _bias_tiles(rel_bias):
    t = DIFF_BLOCK
    return pl.pallas_call(
        _bias_kernel,
        grid=(N_DIFF_HEADS, 2),
        in_specs=[pl.BlockSpec(memory_space=pltpu.SMEM)],
        out_specs=pl.BlockSpec((1, 1, t, t), lambda h, w: (h, w, 0, 0)),
        out_shape=jax.ShapeDtypeStruct((N_DIFF_HEADS, 2, t, t), F32),
        compiler_params=pltpu.CompilerParams(
            dimension_semantics=("arbitrary", "arbitrary"), vmem_limit_bytes=VMEM_LIMIT),
        name="bias",
    )(rel_bias)


def _diff_kernel(lam_ref, q_ref, kt_ref, v_ref, bias_ref, g_ref, o_ref,
                 acc_sc, stat_sc, s_sc, m_sc, l_sc, slow_acc_sc, ok_sm):
    i = pl.program_id(1)
    t = DIFF_BLOCK
    tq = 2 * t
    half_a, half_b = slice(0, t), slice(t, tq)
    lane = lax.broadcasted_iota(jnp.int32, (tq, LANES), 1)
    q = q_ref[...]
    n_pairs = jnp.maximum(i - 1, 0)

    @pl.when(i == 0)
    def _():
        def norm_body(c, ms):
            k = kt_ref[:, pl.ds(pl.multiple_of(c * t, t), t)].astype(F32)
            sq = k * k
            return (jnp.maximum(ms[0], jnp.sum(sq[:HEAD_DIM], axis=0, keepdims=True)),
                    jnp.maximum(ms[1], jnp.sum(sq[HEAD_DIM:], axis=0, keepdims=True)))

        zeros = jnp.zeros((1, t), F32)
        ms = lax.fori_loop(0, kt_ref.shape[1] // t, norm_body, (zeros, zeros))
        for mp in range(2):
            stat_sc[mp] = jnp.broadcast_to(jnp.max(ms[mp], axis=1, keepdims=True), stat_sc.shape[1:])
        bias_max = jnp.max(jnp.max(bias_ref[0, 0], axis=1, keepdims=True), axis=0, keepdims=True)
        stat_sc[2] = jnp.broadcast_to(bias_max, stat_sc.shape[1:])

    def write_out(rows, o0, o1):
        lp = lam_ref[...]
        lam = (jnp.exp(jnp.sum(lp[0:1] * lp[1:2], axis=1, keepdims=True))
               - jnp.exp(jnp.sum(lp[2:3] * lp[3:4], axis=1, keepdims=True)) + LAMBDA_INIT)
        o_ref[rows, :] = (_rms(o0 - lam * o1, g_ref[...]) * (1.0 - LAMBDA_INIT)).astype(o_ref.dtype)

    every = slice(0, tq)
    pair = 2 * t
    zero_q = jnp.zeros_like(q)
    qm = (jnp.where(lane < HEAD_DIM, q, zero_q), jnp.where(lane >= HEAD_DIM, q, zero_q))
    bias_diag, bias_prev = bias_ref[0, 0], bias_ref[0, 1]

    def fast_path(first_block, far_pairs):
        qsq = q.astype(F32) * q.astype(F32)
        bound = []
        for mp in range(2):
            own = (lane < HEAD_DIM) if mp == 0 else (lane >= HEAD_DIM)
            qn = jnp.sum(jnp.where(own, qsq, 0.0), axis=1, keepdims=True)
            b = jnp.sqrt(qn * stat_sc[mp][0:1, 0:1]) * NORM_BOUND_SLACK + stat_sc[2][0:1, 0:1]
            bound.append(jnp.broadcast_to(b, (tq, LANES)))

        acc_sc[...] = jnp.zeros(acc_sc.shape, F32)

        def scores(rows, start, width):
            kt = kt_ref[:, pl.ds(start, width)]
            return [jnp.dot(qm[mp][rows], kt, preferred_element_type=F32)
                    - _tile_lanes(bound[mp][rows], width // LANES) for mp in range(2)]

        def accumulate(rows, s, start, width):
            v_aug = jnp.concatenate([v_ref[pl.ds(start, width), :], jnp.ones((width, LANES), BF16)],
                                    axis=1)
            for mp in range(2):
                acc_sc[mp, rows, :] += jnp.dot(jnp.exp2(s[mp]).astype(BF16), v_aug,
                                               preferred_element_type=F32)

        def last_block_of_b(block):
            start = pl.multiple_of(block * t, t)
            s = [x + bias_diag for x in scores(half_b, start, t)]
            accumulate(half_b, s, start, t)

        if far_pairs is not None:
            first = scores(every, 0, pair)
            for mp in range(2):
                s_sc[mp] = first[mp]

            def far_body(r, carry):
                cur = [s_sc[mp] for mp in range(2)]
                nxt = scores(every, pl.multiple_of((r + 1) * pair, pair), pair)
                accumulate(every, cur, pl.multiple_of(r * pair, pair), pair)
                for mp in range(2):
                    s_sc[mp] = nxt[mp]
                return carry

            lax.fori_loop(0, far_pairs - 1, far_body, 0)
            accumulate(every, [s_sc[mp] for mp in range(2)],
                       pl.multiple_of((far_pairs - 1) * pair, pair), pair)

        if first_block:
            s = [x + jnp.concatenate([bias_diag, bias_prev], axis=0) for x in scores(every, 0, t)]
            accumulate(every, s, 0, t)
            last_block_of_b(1)
        else:
            start = pl.multiple_of((2 * i - 2) * t, t)
            s = scores(every, start, 3 * t)
            for mp in range(2):
                x = s[mp]
                s[mp] = jnp.concatenate([
                    x[:, :t],
                    jnp.concatenate([x[half_a, t:2 * t] + bias_prev, x[half_b, t:2 * t]], axis=0),
                    jnp.concatenate([x[half_a, 2 * t:] + bias_diag, x[half_b, 2 * t:] + bias_prev], axis=0),
                ], axis=1)
            accumulate(every, s, start, 3 * t)
            last_block_of_b(2 * i + 1)

        l0, l1 = acc_sc[0][:, LANES:], acc_sc[1][:, LANES:]
        write_out(every, acc_sc[0][:, :LANES] / l0, acc_sc[1][:, :LANES] / l1)
        ok_sm[0] = (jnp.min(jnp.minimum(l0, l1)) >= SOFTMAX_MIN_DENOMINATOR).astype(jnp.int32)

    @pl.when(i == 0)
    def _():
        fast_path(True, None)

    @pl.when(i == 1)
    def _():
        fast_path(False, None)

    @pl.when(i >= 2)
    def _():
        fast_path(False, n_pairs)

    @pl.when(ok_sm[0] == 0)
    def _():
        def online_softmax_half(rows, diag_block):
            m_sc[...] = jnp.full(m_sc.shape, MASK_VALUE, F32)
            l_sc[...] = jnp.zeros(l_sc.shape, F32)
            slow_acc_sc[...] = jnp.zeros(slow_acc_sc.shape, F32)

            def step(j, bias):
                start = pl.multiple_of(j * t, t)
                kt = kt_ref[:, pl.ds(start, t)]
                v = v_ref[pl.ds(start, t), :]
                for mp in range(2):
                    s = jnp.dot(qm[mp][rows], kt, preferred_element_type=F32)
                    if bias is not None:
                        s = s + bias
                    m_prev = m_sc[mp]
                    m_new = jnp.maximum(m_prev, jnp.max(s, axis=1, keepdims=True))
                    alpha = jnp.exp2(m_prev - m_new)
                    p = jnp.exp2(s - _tile_lanes(m_new, t // LANES))
                    l_sc[mp] = alpha * l_sc[mp] + jnp.sum(p, axis=1, keepdims=True)
                    slow_acc_sc[mp] = (alpha * slow_acc_sc[mp]
                                       + jnp.dot(p.astype(BF16), v, preferred_element_type=F32))
                    m_sc[mp] = m_new

            def slow_body(j, carry):
                step(j, None)
                return carry

            lax.fori_loop(0, jnp.maximum(diag_block - 1, 0), slow_body, 0)

            @pl.when(diag_block >= 1)
            def _():
                step(diag_block - 1, bias_prev)

            step(diag_block, bias_diag)
            write_out(rows, slow_acc_sc[0] / l_sc[0], slow_acc_sc[1] / l_sc[1])

        online_softmax_half(half_a, 2 * i)
        online_softmax_half(half_b, 2 * i + 1)


def _diff_attention(lam_params, qd, kdt, vd, bias, g):
    s = qd.shape[0]
    t = DIFF_BLOCK
    tq = 2 * t
    return pl.pallas_call(
        _diff_kernel,
        grid=(N_DIFF_HEADS, s // tq),
        in_specs=[
            pl.BlockSpec(lam_params.shape, lambda h, i: (0, 0)),
            pl.BlockSpec((tq, LANES), lambda h, i: (i, h)),
            pl.BlockSpec((LANES, s), lambda h, i: (h, 0)),
            pl.BlockSpec((s, LANES), lambda h, i: (0, h)),
            pl.BlockSpec((1, 2, t, t), lambda h, i: (h, 0, 0, 0)),
            pl.BlockSpec(g.shape, lambda h, i: (0, 0)),
        ],
        out_specs=pl.BlockSpec((tq, LANES), lambda h, i: (i, h)),
        out_shape=jax.ShapeDtypeStruct((s, DIFF_WIDTH), BF16),
        scratch_shapes=[pltpu.VMEM((2, tq, 2 * LANES), F32), pltpu.VMEM((3, 8, LANES), F32),
                        pltpu.VMEM((2, tq, 2 * t), F32)]
        + [pltpu.VMEM((2, t, LANES), F32)] * 3 + [pltpu.SMEM((1,), jnp.int32)],
        compiler_params=pltpu.CompilerParams(
            dimension_semantics=("arbitrary", "arbitrary"), vmem_limit_bytes=DIFF_VMEM_LIMIT),
        name="diff_attn",
    )(lam_params, qd, kdt, vd, bias, g)


def _sb_kernel(q_ref, kt_ref, v_ref, g_ref, o_ref, carry_sc, acc_sc, knorm_sc, thresh_sm, done_sm):
    i = pl.program_id(1)
    t = SB_K_BLOCK
    n_sub = SB_Q_BLOCK // t
    lane = lax.broadcasted_iota(jnp.int32, (t, LANES), 1)
    head0 = lane < HEAD_DIM
    lower = (lax.broadcasted_iota(jnp.int32, (t, t), 0)
             >= lax.broadcasted_iota(jnp.int32, (t, t), 1)).astype(BF16)
    strict = lax.broadcasted_iota(jnp.int32, (t, t), 0) > lax.broadcasted_iota(jnp.int32, (t, t), 1)

    @pl.when(i == 0)
    def _():
        def norm_body(c, m):
            k = kt_ref[:, pl.ds(pl.multiple_of(c * t, t), t)].astype(F32)
            sq = k * k
            return jnp.maximum(m, jnp.maximum(jnp.sum(sq[:HEAD_DIM], axis=0, keepdims=True),
                                              jnp.sum(sq[HEAD_DIM:], axis=0, keepdims=True)))

        m = lax.fori_loop(0, kt_ref.shape[1] // t, norm_body, jnp.zeros((1, t), F32))
        knorm_sc[...] = jnp.broadcast_to(jnp.max(m, axis=1, keepdims=True), knorm_sc.shape)

    def load_q(u):
        q = q_ref[pl.ds(pl.multiple_of(u * t, t), t), :]
        zero_q = jnp.zeros_like(q)
        return q, (jnp.where(head0, q, zero_q), jnp.where(head0, zero_q, q))

    def block_pass(jb, consumers):
        start = pl.multiple_of(jb * t, t)
        kt = kt_ref[:, pl.ds(start, t)]
        v = v_ref[pl.ds(start, t), :]
        zero_v = jnp.zeros_like(v)
        v2 = jnp.concatenate([jnp.where(head0, v, zero_v), jnp.where(head0, zero_v, v)], axis=0)
        q_rows = [qm[hh] for _, qm, _ in consumers for hh in range(2)]
        z = jnp.dot(jnp.concatenate(q_rows, axis=0), kt, preferred_element_type=F32)
        nl = jnp.maximum(z, 0.0) + jnp.log(1.0 + jnp.exp2(-jnp.abs(z))) * LOG2E
        segments = [slice(n * t, (n + 1) * t) for n in range(len(q_rows))]
        masked = [m for _, _, m in consumers for _ in range(2)]
        nl = jnp.concatenate([jnp.where(strict, nl[seg], 0.0) if m else nl[seg]
                              for seg, m in zip(segments, masked)], axis=0)
        csum = jnp.dot(nl.astype(BF16), lower, preferred_element_type=F32)
        probs = []
        for n, (u, _, m) in enumerate(consumers):
            heads = []
            for hh in range(2):
                seg = segments[2 * n + hh]
                if m:
                    a = jnp.where(strict, jnp.exp2(z[seg] - csum[seg]), 0.0)
                    carry_sc[u, hh] = jnp.broadcast_to(csum[seg][:, 0:1], (t, LANES))
                else:
                    carry = carry_sc[u, hh]
                    a = jnp.exp2((z[seg] - _tile_lanes(carry, t // LANES)) - csum[seg])
                    carry_sc[u, hh] = carry + csum[seg][:, 0:1]
                heads.append(a.astype(BF16))
            probs.append(jnp.concatenate(heads, axis=1))
        pv = jnp.dot(jnp.concatenate(probs, axis=0), v2, preferred_element_type=F32)
        for n, (u, _, m) in enumerate(consumers):
            if m:
                acc_sc[u] = pv[n * t:(n + 1) * t]
            else:
                acc_sc[u] += pv[n * t:(n + 1) * t]

    qs = [load_q(u) for u in range(n_sub)]

    qmax = jnp.zeros((t, LANES), F32)
    for u in range(n_sub):
        qmax = jnp.maximum(qmax, jnp.abs(qs[u][0].astype(F32)))
    qmax = jnp.max(jnp.max(qmax, axis=1, keepdims=True), axis=0, keepdims=True)
    thresh = jnp.sqrt(HEAD_DIM * qmax * qmax * knorm_sc[0:1, 0:1]) * NORM_BOUND_SLACK + EXP2_UNDERFLOW
    thresh_sm[0] = jnp.max(thresh)

    def near_passes(has_previous_block):
        for u in range(n_sub - 1, -1, -1):
            consumers = [(u, qs[u][1], True)]
            if u + 1 < n_sub:
                consumers.append((u + 1, qs[u + 1][1], False))
            block_pass(i * n_sub + u, consumers)
        if has_previous_block:
            block_pass(i * n_sub - 1, [(0, qs[0][1], False)])

    @pl.when(i == 0)
    def _():
        near_passes(False)

    @pl.when(i >= 1)
    def _():
        near_passes(True)

    def carry_min(u):
        return jnp.min(jnp.minimum(carry_sc[u, 0], carry_sc[u, 1]))

    for u in range(n_sub):
        done_sm[u] = (carry_min(u) >= thresh_sm[0]).astype(jnp.int32)

    def remaining_blocks(u, c):
        _, qm = load_q(u)

        def far_cond(state):
            jb, all_zero = state
            return jnp.logical_and(jb >= 0, all_zero == 0)

        def far_body(state):
            jb, _ = state
            block_pass(jb, [(u, qm, False)])
            return jb - 1, (carry_min(u) >= thresh_sm[0]).astype(jnp.int32)

        lax.while_loop(far_cond, far_body, (i * n_sub + u - 2, done_sm[u]))
        return c

    lax.fori_loop(0, n_sub, remaining_blocks, 0)

    for u in range(n_sub):
        o = acc_sc[u]
        sq = o * o
        ss0 = jnp.sum(jnp.where(head0, sq, 0.0), axis=1, keepdims=True)
        ss1 = jnp.sum(jnp.where(head0, 0.0, sq), axis=1, keepdims=True)
        ms = jnp.where(head0, ss0, ss1) * (1.0 / HEAD_DIM)
        o_ref[u * t:(u + 1) * t, :] = (o * lax.rsqrt(ms + RMS_EPS) * g_ref[...]).astype(o_ref.dtype)


def _sb_attention(qs, kst, vs, g2):
    s = qs.shape[0]
    tq, t = SB_Q_BLOCK, SB_K_BLOCK
    return pl.pallas_call(
        _sb_kernel,
        grid=(N_SB_HEADS // 2, s // tq),
        in_specs=[
            pl.BlockSpec((tq, LANES), lambda h, i: (i, h)),
            pl.BlockSpec((LANES, s), lambda h, i: (h, 0)),
            pl.BlockSpec((s, LANES), lambda h, i: (0, h)),
            pl.BlockSpec(g2.shape, lambda h, i: (0, 0)),
        ],
        out_specs=pl.BlockSpec((tq, LANES), lambda h, i: (i, h)),
        out_shape=jax.ShapeDtypeStruct((s, SB_WIDTH), BF16),
        scratch_shapes=[pltpu.VMEM((tq // t, 2, t, LANES), F32), pltpu.VMEM((tq // t, t, LANES), F32),
                        pltpu.VMEM((8, LANES), F32),
                        pltpu.SMEM((tq // t,), F32), pltpu.SMEM((tq // t,), jnp.int32)],
        compiler_params=pltpu.CompilerParams(
            dimension_semantics=("arbitrary", "arbitrary"), vmem_limit_bytes=VMEM_LIMIT),
        name="sb_attn",
    )(qs, kst, vs, g2)


def _gelu_tanh(x):
    return 0.5 * x * (1.0 + jnp.tanh(math.sqrt(2.0 / math.pi) * (x + 0.044715 * (x * x * x))))


def _mlp_kernel(d_ref, s_ref, x_ref, dh_ref, sh_ref, xh_ref, wo_ref, gattn_ref, gpre_ref,
                wup_ref, cw_ref, cb_ref, wd_ref, gffn_ref, o_ref, u_sc):
    i = pl.program_id(0)
    rows, chunk = MLP_ROWS, FFN_CHUNK
    stack = lambda main_ref, halo_ref: jnp.concatenate([main_ref[...], halo_ref[...]], axis=0)
    mix = (jnp.dot(stack(d_ref, dh_ref), wo_ref[:DIFF_WIDTH], preferred_element_type=F32)
           + jnp.dot(stack(s_ref, sh_ref), wo_ref[DIFF_WIDTH:], preferred_element_type=F32))
    x1 = stack(x_ref, xh_ref) + _rms(mix, gattn_ref[...])
    h = _rms(x1, gpre_ref[...])
    row = lax.broadcasted_iota(jnp.int32, h.shape, 0)
    lhs = jnp.where(jnp.logical_or(row < rows, i > 0), h, 0.0).astype(BF16)

    def conv(col, slot):
        cols = slice(col, col + chunk)
        u = jnp.dot(lhs, wup_ref[:, cols], preferred_element_type=F32)
        u_sc[slot, 0:HALO_ROWS, :] = u[rows:]
        u_sc[slot, HALO_ROWS:, :] = u[:rows]
        y = cb_ref[:, cols]
        for k in range(CONV_WIDTH):
            off = HALO_ROWS - (CONV_WIDTH - 1) + k
            y = y + u_sc[slot, off:off + rows, :] * cw_ref[k:k + 1, cols]
        return y

    n_chunk = D_FF // chunk
    gates = [conv(c * chunk, 2 * c) for c in range(n_chunk)]
    vals = [conv(D_FF + c * chunk, 2 * c + 1) for c in range(n_chunk)]
    y = None
    for c in range(n_chunk):
        act = (_gelu_tanh(gates[c]) * vals[c]).astype(BF16)
        part = jnp.dot(act, wd_ref[c * chunk:(c + 1) * chunk, :], preferred_element_type=F32)
        y = part if y is None else y + part
    o_ref[...] = x1[:rows] + _rms(y, gffn_ref[...])


def _mlp(diff_o, sb_o, x2, w_o, g_attn, g_pre, w_up, conv_w, conv_b, w_down, g_ffn):
    s = x2.shape[0]
    rows = MLP_ROWS
    halo_blocks = rows // HALO_ROWS
    row_spec = lambda w: pl.BlockSpec((rows, w), lambda i: (i, 0))
    halo_spec = lambda w: pl.BlockSpec((HALO_ROWS, w), lambda i: (jnp.maximum(i * halo_blocks - 1, 0), 0))
    resident = lambda a: pl.BlockSpec(a.shape, lambda i: (0, 0), pipeline_mode=pl.Buffered(1))
    return pl.pallas_call(
        _mlp_kernel,
        grid=(s // rows,),
        in_specs=[row_spec(DIFF_WIDTH), row_spec(SB_WIDTH), row_spec(D_MODEL),
                  halo_spec(DIFF_WIDTH), halo_spec(SB_WIDTH), halo_spec(D_MODEL),
                  resident(w_o), resident(g_attn), resident(g_pre),
                  resident(w_up), resident(conv_w), resident(conv_b), resident(w_down), resident(g_ffn)],
        out_specs=row_spec(D_MODEL),
        out_shape=jax.ShapeDtypeStruct((s, D_MODEL), F32),
        scratch_shapes=[pltpu.VMEM((2 * (D_FF // FFN_CHUNK), HALO_ROWS + rows, FFN_CHUNK), F32)],
        compiler_params=pltpu.CompilerParams(
            dimension_semantics=("arbitrary",), vmem_limit_bytes=MLP_VMEM_LIMIT),
        name="mlp",
    )(diff_o, sb_o, x2, diff_o, sb_o, x2, w_o, g_attn, g_pre, w_up, conv_w, conv_b, w_down, g_ffn)


def kernel(x, attn_pre_norm, w_qkv, lambda_q1, lambda_k1, lambda_q2, lambda_k2, diff_subln, sb_norm,
           rel_bias, w_o, attn_post_norm, ffn_pre_norm, w_up, conv_w, conv_b, w_down, ffn_post_norm):
    b, s, _ = x.shape
    assert b == 1 and attn_pre_norm.shape[0] == 1, "single sequence, single layer"
    assert s % max(PROJ_ROWS, 2 * DIFF_BLOCK, SB_Q_BLOCK, MLP_ROWS) == 0
    x2 = x[0]
    row = lambda a: a.reshape(1, -1).astype(F32)

    qd, vd, qs, vs, kdt, kst, w_up_bf, w_down_bf = _proj(
        x2, row(attn_pre_norm), w_qkv[0].astype(F32), w_up.astype(F32), w_down.astype(F32))

    bias = _bias_tiles(rel_bias.astype(F32))
    lam_params = jnp.concatenate([lambda_q1, lambda_k1, lambda_q2, lambda_k2], axis=0).astype(F32)
    diff_o = _diff_attention(lam_params, qd, kdt, vd, bias, row(diff_subln))
    sb_o = _sb_attention(qs, kst, vs, row(jnp.concatenate([sb_norm[0], sb_norm[0]])))

    out = _mlp(diff_o, sb_o, x2, w_o[0].astype(BF16), row(attn_post_norm), row(ffn_pre_norm),
               w_up_bf[0], conv_w[0].astype(F32), conv_b.astype(F32), w_down_bf[0],
               row(ffn_post_norm))
    return out[None]
```

```python
import math

import jax
import jax.numpy as jnp
from jax import lax
from jax.experimental import pallas as pl
from jax.experimental.pallas import tpu as pltpu

F32 = jnp.float32
BF16 = jnp.bfloat16

D_MODEL = 1024
HEAD_DIM = 64
N_DIFF_HEADS = 4
N_SB_HEADS = 8
DIFF_WIDTH = N_DIFF_HEADS * 2 * HEAD_DIM
SB_WIDTH = N_SB_HEADS * HEAD_DIM
QKV_GROUP = DIFF_WIDTH
assert SB_WIDTH == QKV_GROUP
D_FF = 2816
CONV_WIDTH = 3
N_BUCKETS = 32
MAX_DISTANCE = 128
RMS_EPS = 1e-6
LAMBDA_INIT = 0.8 - 0.6 * math.exp(-0.3 * 0)

LOG2E = 1.4426950408889634
Q_SCALE = HEAD_DIM ** -0.5 * LOG2E
MASK_VALUE = -1e30
EXP2_UNDERFLOW = 160.0
NORM_BOUND_SLACK = 1.0 + 2.0 ** -10
SOFTMAX_MIN_DENOMINATOR = 2.0 ** -60

LANES = 128
PROJ_ROWS = 1024
DIFF_BLOCK = 512
SB_Q_BLOCK = 4096
SB_K_BLOCK = 256
MLP_ROWS = 512
FFN_CHUNK = D_FF // 2
HALO_ROWS = 16
VMEM_LIMIT = 48 * 1024 * 1024
DIFF_VMEM_LIMIT = 58 * 1024 * 1024
MLP_VMEM_LIMIT = 58 * 1024 * 1024


def _tile_lanes(x, k):
    return jnp.concatenate([x] * k, axis=1) if k > 1 else x


def _rms(x, g):
    return x * lax.rsqrt(jnp.mean(x * x, axis=-1, keepdims=True) + RMS_EPS) * g


def _proj_kernel(x_ref, g_ref, w_ref, wup_ref, wdown_ref,
                 qd_ref, vd_ref, qs_ref, vs_ref, kdt_ref, kst_ref, wup_bf_ref, wdown_bf_ref):
    hb = _rms(x_ref[...], g_ref[...])
    proj = lambda c: jnp.dot(hb, w_ref[:, c * QKV_GROUP:(c + 1) * QKV_GROUP], preferred_element_type=F32)
    qd_ref[...] = (proj(0) * Q_SCALE).astype(BF16)
    vd_ref[...] = proj(2).astype(BF16)
    qs_ref[...] = (proj(3) * Q_SCALE).astype(BF16)
    vs_ref[...] = proj(5).astype(BF16)
    proj_t = lambda c: lax.dot_general(w_ref[:, c * QKV_GROUP:(c + 1) * QKV_GROUP], hb, (((0,), (1,)), ((), ())),
                                       preferred_element_type=F32)
    kdt_ref[...] = proj_t(1).astype(BF16)
    kst_ref[...] = proj_t(4).astype(BF16)
    wup_bf_ref[...] = wup_ref[...].astype(BF16)
    wdown_bf_ref[...] = wdown_ref[...].astype(BF16)


def _proj(x2, g, w, w_up, w_down):
    s = x2.shape[0]
    rows = PROJ_ROWS
    steps = s // rows
    row_spec = lambda w: pl.BlockSpec((rows, w), lambda i: (i, 0))
    full = lambda a: pl.BlockSpec(a.shape, lambda i: (0, 0))
    slab_spec = lambda a: pl.BlockSpec((None, a.shape[1] // steps, a.shape[2]), lambda i: (0, i, 0))
    out_rows = jax.ShapeDtypeStruct((s, QKV_GROUP), BF16)
    out_t = jax.ShapeDtypeStruct((QKV_GROUP, s), BF16)
    return pl.pallas_call(
        _proj_kernel,
        grid=(steps,),
        in_specs=[row_spec(D_MODEL), full(g),
                  pl.BlockSpec(w.shape, lambda i: (0, 0), pipeline_mode=pl.Buffered(1)),
                  slab_spec(w_up), slab_spec(w_down)],
        out_specs=[row_spec(QKV_GROUP)] * 4 + [pl.BlockSpec((QKV_GROUP, rows), lambda i: (0, i))] * 2
        + [slab_spec(w_up), slab_spec(w_down)],
        out_shape=[out_rows] * 4 + [out_t] * 2
        + [jax.ShapeDtypeStruct(w_up.shape, BF16), jax.ShapeDtypeStruct(w_down.shape, BF16)],
        compiler_params=pltpu.CompilerParams(
            dimension_semantics=("arbitrary",), vmem_limit_bytes=VMEM_LIMIT),
        name="proj",
    )(x2, g, w, w_up, w_down)


def _write_bias_tiles(tab_ref, h, bias_sc):
    t, sub = DIFF_BLOCK, LANES

    def sub_tile(offset):
        rel = (lax.broadcasted_iota(jnp.int32, (sub, sub), 0)
               - lax.broadcasted_iota(jnp.int32, (sub, sub), 1) + offset)
        n = jnp.maximum(rel, 0)
        max_exact = N_BUCKETS // 2
        nf = jnp.maximum(n, 1).astype(F32)
        large = max_exact + jnp.floor(jnp.log(nf / max_exact) / math.log(MAX_DISTANCE / max_exact)
                                      * (N_BUCKETS - max_exact)).astype(jnp.int32)
        large = jnp.minimum(large, N_BUCKETS - 1)
        bucket = jnp.where(n < max_exact, n, large)
        b = jnp.zeros((sub, sub), F32)
        for k in range(N_BUCKETS):
            b = jnp.where(bucket == k, tab_ref[k, h], b)
        b = (b - tab_ref[N_BUCKETS - 1, h]) * LOG2E
        return jnp.where(rel >= 0, b, MASK_VALUE)

    for w in range(2):
        for br in range(t // sub):
            for bc in range(t // sub):
                offset = w * t + (br - bc) * sub
                if offset + sub <= 0:
                    val = jnp.full((sub, sub), MASK_VALUE, F32)
                elif offset - sub >= MAX_DISTANCE:
                    val = jnp.zeros((sub, sub), F32)
                else:
                    val = sub_tile(offset)
                bias_sc[w, br * sub:(br + 1) * sub, bc * sub:(bc + 1) * sub] = val


def _diff_kernel(tab_ref, lam_ref, q_ref, kt_ref, v_ref, g_ref, o_ref,
                 acc_sc, stat_sc, s_sc, m_sc, l_sc, slow_acc_sc, ok_sm, bias_sc):
    i = pl.program_id(1)
    t = DIFF_BLOCK
    tq = 2 * t
    half_a, half_b = slice(0, t), slice(t, tq)
    lane = lax.broadcasted_iota(jnp.int32, (tq, LANES), 1)
    q = q_ref[...]
    n_pairs = jnp.maximum(i - 1, 0)

    @pl.when(i == 0)
    def _():
        def norm_body(c, ms):
            k = kt_ref[:, pl.ds(pl.multiple_of(c * t, t), t)].astype(F32)
            sq = k * k
            return (jnp.maximum(ms[0], jnp.sum(sq[:HEAD_DIM], axis=0, keepdims=True)),
                    jnp.maximum(ms[1], jnp.sum(sq[HEAD_DIM:], axis=0, keepdims=True)))

        zeros = jnp.zeros((1, t), F32)
        ms = lax.fori_loop(0, kt_ref.shape[1] // t, norm_body, (zeros, zeros))
        for mp in range(2):
            stat_sc[mp] = jnp.broadcast_to(jnp.max(ms[mp], axis=1, keepdims=True), stat_sc.shape[1:])
        _write_bias_tiles(tab_ref, pl.program_id(0), bias_sc)
        bias_max = jnp.max(jnp.max(bias_sc[0], axis=1, keepdims=True), axis=0, keepdims=True)
        stat_sc[2] = jnp.broadcast_to(bias_max, stat_sc.shape[1:])

    def write_out(rows, o0, o1):
        lp = lam_ref[...]
        lam = (jnp.exp(jnp.sum(lp[0:1] * lp[1:2], axis=1, keepdims=True))
               - jnp.exp(jnp.sum(lp[2:3] * lp[3:4], axis=1, keepdims=True)) + LAMBDA_INIT)
        o_ref[rows, :] = (_rms(o0 - lam * o1, g_ref[...]) * (1.0 - LAMBDA_INIT)).astype(o_ref.dtype)

    every = slice(0, tq)
    pair = 2 * t
    zero_q = jnp.zeros_like(q)
    qm = (jnp.where(lane < HEAD_DIM, q, zero_q), jnp.where(lane >= HEAD_DIM, q, zero_q))
    bias_diag, bias_prev = bias_sc[0], bias_sc[1]

    def fast_path(first_block, far_pairs):
        qsq = q.astype(F32) * q.astype(F32)
        bound = []
        for mp in range(2):
            own = (lane < HEAD_DIM) if mp == 0 else (lane >= HEAD_DIM)
            qn = jnp.sum(jnp.where(own, qsq, 0.0), axis=1, keepdims=True)
            b = jnp.sqrt(qn * stat_sc[mp][0:1, 0:1]) * NORM_BOUND_SLACK + stat_sc[2][0:1, 0:1]
            bound.append(jnp.broadcast_to(b, (tq, LANES)))

        acc_sc[...] = jnp.zeros(acc_sc.shape, F32)

        def scores(rows, start, width):
            kt = kt_ref[:, pl.ds(start, width)]
            return [jnp.dot(qm[mp][rows], kt, preferred_element_type=F32)
                    - _tile_lanes(bound[mp][rows], width // LANES) for mp in range(2)]

        def accumulate(rows, s, start, width):
            v_aug = jnp.concatenate([v_ref[pl.ds(start, width), :], jnp.ones((width, LANES), BF16)],
                                    axis=1)
            for mp in range(2):
                acc_sc[mp, rows, :] += jnp.dot(jnp.exp2(s[mp]).astype(BF16), v_aug,
                                               preferred_element_type=F32)

        def last_block_of_b(block):
            start = pl.multiple_of(block * t, t)
            s = [x + bias_diag for x in scores(half_b, start, t)]
            accumulate(half_b, s, start, t)

        if far_pairs is not None:
            first = scores(every, 0, pair)
            for mp in range(2):
                s_sc[mp] = first[mp]

            def far_body(r, carry):
                cur = [s_sc[mp] for mp in range(2)]
                nxt = scores(every, pl.multiple_of((r + 1) * pair, pair), pair)
                accumulate(every, cur, pl.multiple_of(r * pair, pair), pair)
                for mp in range(2):
                    s_sc[mp] = nxt[mp]
                return carry

            lax.fori_loop(0, far_pairs - 1, far_body, 0)
            accumulate(every, [s_sc[mp] for mp in range(2)],
                       pl.multiple_of((far_pairs - 1) * pair, pair), pair)

        if first_block:
            s = [x + jnp.concatenate([bias_diag, bias_prev], axis=0) for x in scores(every, 0, t)]
            accumulate(every, s, 0, t)
            last_block_of_b(1)
        else:
            start = pl.multiple_of((2 * i - 2) * t, t)
            s = scores(every, start, 3 * t)
            for mp in range(2):
                x = s[mp]
                s[mp] = jnp.concatenate([
                    x[:, :t],
                    jnp.concatenate([x[half_a, t:2 * t] + bias_prev, x[half_b, t:2 * t]], axis=0),
                    jnp.concatenate([x[half_a, 2 * t:] + bias_diag, x[half_b, 2 * t:] + bias_prev], axis=0),
                ], axis=1)
            accumulate(every, s, start, 3 * t)
            last_block_of_b(2 * i + 1)

        l0, l1 = acc_sc[0][:, LANES:], acc_sc[1][:, LANES:]
        write_out(every, acc_sc[0][:, :LANES] / l0, acc_sc[1][:, :LANES] / l1)
        ok_sm[0] = (jnp.min(jnp.minimum(l0, l1)) >= SOFTMAX_MIN_DENOMINATOR).astype(jnp.int32)

    @pl.when(i == 0)
    def _():
        fast_path(True, None)

    @pl.when(i == 1)
    def _():
        fast_path(False, None)

    @pl.when(i >= 2)
    def _():
        fast_path(False, n_pairs)

    @pl.when(ok_sm[0] == 0)
    def _():
        def online_softmax_half(rows, diag_block):
            m_sc[...] = jnp.full(m_sc.shape, MASK_VALUE, F32)
            l_sc[...] = jnp.zeros(l_sc.shape, F32)
            slow_acc_sc[...] = jnp.zeros(slow_acc_sc.shape, F32)

            def step(j, bias):
                start = pl.multiple_of(j * t, t)
                kt = kt_ref[:, pl.ds(start, t)]
                v = v_ref[pl.ds(start, t), :]
                for mp in range(2):
                    s = jnp.dot(qm[mp][rows], kt, preferred_element_type=F32)
                    if bias is not None:
                        s = s + bias
                    m_prev = m_sc[mp]
                    m_new = jnp.maximum(m_prev, jnp.max(s, axis=1, keepdims=True))
                    alpha = jnp.exp2(m_prev - m_new)
                    p = jnp.exp2(s - _tile_lanes(m_new, t // LANES))
                    l_sc[mp] = alpha * l_sc[mp] + jnp.sum(p, axis=1, keepdims=True)
                    slow_acc_sc[mp] = (alpha * slow_acc_sc[mp]
                                       + jnp.dot(p.astype(BF16), v, preferred_element_type=F32))
                    m_sc[mp] = m_new

            def slow_body(j, carry):
                step(j, None)
                return carry

            lax.fori_loop(0, jnp.maximum(diag_block - 1, 0), slow_body, 0)

            @pl.when(diag_block >= 1)
            def _():
                step(diag_block - 1, bias_prev)

            step(diag_block, bias_diag)
            write_out(rows, slow_acc_sc[0] / l_sc[0], slow_acc_sc[1] / l_sc[1])

        online_softmax_half(half_a, 2 * i)
        online_softmax_half(half_b, 2 * i + 1)


def _diff_attention(rel_bias, lam_params, qd, kdt, vd, g):
    s = qd.shape[0]
    t = DIFF_BLOCK
    tq = 2 * t
    return pl.pallas_call(
        _diff_kernel,
        grid=(N_DIFF_HEADS, s // tq),
        in_specs=[
            pl.BlockSpec(memory_space=pltpu.SMEM),
            pl.BlockSpec(lam_params.shape, lambda h, i: (0, 0)),
            pl.BlockSpec((tq, LANES), lambda h, i: (i, h)),
            pl.BlockSpec((LANES, s), lambda h, i: (h, 0)),
            pl.BlockSpec((s, LANES), lambda h, i: (0, h)),
            pl.BlockSpec(g.shape, lambda h, i: (0, 0)),
        ],
        out_specs=pl.BlockSpec((tq, LANES), lambda h, i: (i, h)),
        out_shape=jax.ShapeDtypeStruct((s, DIFF_WIDTH), BF16),
        scratch_shapes=[pltpu.VMEM((2, tq, 2 * LANES), F32), pltpu.VMEM((3, 8, LANES), F32),
                        pltpu.VMEM((2, tq, 2 * t), F32)]
        + [pltpu.VMEM((2, t, LANES), F32)] * 3 + [pltpu.SMEM((1,), jnp.int32), pltpu.VMEM((2, t, t), F32)],
        compiler_params=pltpu.CompilerParams(
            dimension_semantics=("arbitrary", "arbitrary"), vmem_limit_bytes=DIFF_VMEM_LIMIT),
        name="diff_attn",
    )(rel_bias, lam_params, qd, kdt, vd, g)


def _sb_kernel(q_ref, kt_ref, v_ref, g_ref, o_ref, carry_sc, acc_sc, knorm_sc, thresh_sm, done_sm):
    i = pl.program_id(1)
    t = SB_K_BLOCK
    n_sub = SB_Q_BLOCK // t
    lane = lax.broadcasted_iota(jnp.int32, (t, LANES), 1)
    head0 = lane < HEAD_DIM
    lower = (lax.broadcasted_iota(jnp.int32, (t, t), 0)
             >= lax.broadcasted_iota(jnp.int32, (t, t), 1)).astype(BF16)
    strict = lax.broadcasted_iota(jnp.int32, (t, t), 0) > lax.broadcasted_iota(jnp.int32, (t, t), 1)

    @pl.when(i == 0)
    def _():
        def norm_body(c, m):
            k = kt_ref[:, pl.ds(pl.multiple_of(c * t, t), t)].astype(F32)
            sq = k * k
            return jnp.maximum(m, jnp.maximum(jnp.sum(sq[:HEAD_DIM], axis=0, keepdims=True),
                                              jnp.sum(sq[HEAD_DIM:], axis=0, keepdims=True)))

        m = lax.fori_loop(0, kt_ref.shape[1] // t, norm_body, jnp.zeros((1, t), F32))
        knorm_sc[...] = jnp.broadcast_to(jnp.max(m, axis=1, keepdims=True), knorm_sc.shape)

    def load_q(u):
        q = q_ref[pl.ds(pl.multiple_of(u * t, t), t), :]
        zero_q = jnp.zeros_like(q)
        return q, (jnp.where(head0, q, zero_q), jnp.where(head0, zero_q, q))

    def block_pass(jb, consumers):
        start = pl.multiple_of(jb * t, t)
        kt = kt_ref[:, pl.ds(start, t)]
        v = v_ref[pl.ds(start, t), :]
        zero_v = jnp.zeros_like(v)
        v2 = jnp.concatenate([jnp.where(head0, v, zero_v), jnp.where(head0, zero_v, v)], axis=0)
        q_rows = [qm[hh] for _, qm, _ in consumers for hh in range(2)]
        z = jnp.dot(jnp.concatenate(q_rows, axis=0), kt, preferred_element_type=F32)
        nl = jnp.maximum(z, 0.0) + jnp.log(1.0 + jnp.exp2(-jnp.abs(z))) * LOG2E
        segments = [slice(n * t, (n + 1) * t) for n in range(len(q_rows))]
        masked = [m for _, _, m in consumers for _ in range(2)]
        nl = jnp.concatenate([jnp.where(strict, nl[seg], 0.0) if m else nl[seg]
                              for seg, m in zip(segments, masked)], axis=0)
        csum = jnp.dot(nl.astype(BF16), lower, preferred_element_type=F32)
        probs = []
        for n, (u, _, m) in enumerate(consumers):
            heads = []
            for hh in range(2):
                seg = segments[2 * n + hh]
                if m:
                    a = jnp.where(strict, jnp.exp2(z[seg] - csum[seg]), 0.0)
                    carry_sc[u, hh] = jnp.broadcast_to(csum[seg][:, 0:1], (t, LANES))
                else:
                    carry = carry_sc[u, hh]
                    a = jnp.exp2((z[seg] - _tile_lanes(carry, t // LANES)) - csum[seg])
                    carry_sc[u, hh] = carry + csum[seg][:, 0:1]
                heads.append(a.astype(BF16))
            probs.append(jnp.concatenate(heads, axis=1))
        pv = jnp.dot(jnp.concatenate(probs, axis=0), v2, preferred_element_type=F32)
        for n, (u, _, m) in enumerate(consumers):
            if m:
                acc_sc[u] = pv[n * t:(n + 1) * t]
            else:
                acc_sc[u] += pv[n * t:(n + 1) * t]

    qs = [load_q(u) for u in range(n_sub)]

    qmax = jnp.zeros((t, LANES), F32)
    for u in range(n_sub):
        qmax = jnp.maximum(qmax, jnp.abs(qs[u][0].astype(F32)))
    qmax = jnp.max(jnp.max(qmax, axis=1, keepdims=True), axis=0, keepdims=True)
    thresh = jnp.sqrt(HEAD_DIM * qmax * qmax * knorm_sc[0:1, 0:1]) * NORM_BOUND_SLACK + EXP2_UNDERFLOW
    thresh_sm[0] = jnp.max(thresh)

    def near_passes(has_previous_block):
        for u in range(n_sub - 1, -1, -1):
            consumers = [(u, qs[u][1], True)]
            if u + 1 < n_sub:
                consumers.append((u + 1, qs[u + 1][1], False))
            block_pass(i * n_sub + u, consumers)
        if has_previous_block:
            block_pass(i * n_sub - 1, [(0, qs[0][1], False)])

    @pl.when(i == 0)
    def _():
        near_passes(False)

    @pl.when(i >= 1)
    def _():
        near_passes(True)

    def carry_min(u):
        return jnp.min(jnp.minimum(carry_sc[u, 0], carry_sc[u, 1]))

    for u in range(n_sub):
        done_sm[u] = (carry_min(u) >= thresh_sm[0]).astype(jnp.int32)

    def remaining_blocks(u, c):
        _, qm = load_q(u)

        def far_cond(state):
            jb, all_zero = state
            return jnp.logical_and(jb >= 0, all_zero == 0)

        def far_body(state):
            jb, _ = state
            block_pass(jb, [(u, qm, False)])
            return jb - 1, (carry_min(u) >= thresh_sm[0]).astype(jnp.int32)

        lax.while_loop(far_cond, far_body, (i * n_sub + u - 2, done_sm[u]))
        return c

    lax.fori_loop(0, n_sub, remaining_blocks, 0)

    for u in range(n_sub):
        o = acc_sc[u]
        sq = o * o
        ss0 = jnp.sum(jnp.where(head0, sq, 0.0), axis=1, keepdims=True)
        ss1 = jnp.sum(jnp.where(head0, 0.0, sq), axis=1, keepdims=True)
        ms = jnp.where(head0, ss0, ss1) * (1.0 / HEAD_DIM)
        o_ref[u * t:(u + 1) * t, :] = (o * lax.rsqrt(ms + RMS_EPS) * g_ref[...]).astype(o_ref.dtype)


def _sb_attention(qs, kst, vs, g2):
    s = qs.shape[0]
    tq, t = SB_Q_BLOCK, SB_K_BLOCK
    return pl.pallas_call(
        _sb_kernel,
        grid=(N_SB_HEADS // 2, s // tq),
        in_specs=[
            pl.BlockSpec((tq, LANES), lambda h, i: (i, h)),
            pl.BlockSpec((LANES, s), lambda h, i: (h, 0)),
            pl.BlockSpec((s, LANES), lambda h, i: (0, h)),
            pl.BlockSpec(g2.shape, lambda h, i: (0, 0)),
        ],
        out_specs=pl.BlockSpec((tq, LANES), lambda h, i: (i, h)),
        out_shape=jax.ShapeDtypeStruct((s, SB_WIDTH), BF16),
        scratch_shapes=[pltpu.VMEM((tq // t, 2, t, LANES), F32), pltpu.VMEM((tq // t, t, LANES), F32),
                        pltpu.VMEM((8, LANES), F32),
                        pltpu.SMEM((tq // t,), F32), pltpu.SMEM((tq // t,), jnp.int32)],
        compiler_params=pltpu.CompilerParams(
            dimension_semantics=("arbitrary", "arbitrary"), vmem_limit_bytes=VMEM_LIMIT),
        name="sb_attn",
    )(qs, kst, vs, g2)


def _gelu_tanh(x):
    return 0.5 * x * (1.0 + jnp.tanh(math.sqrt(2.0 / math.pi) * (x + 0.044715 * (x * x * x))))


def _mlp_kernel(d_ref, s_ref, x_ref, dh_ref, sh_ref, xh_ref, wo_ref, gattn_ref, gpre_ref,
                wup_ref, cw_ref, cb_ref, wd_ref, gffn_ref, o_ref, u_sc):
    i = pl.program_id(0)
    rows, chunk = MLP_ROWS, FFN_CHUNK
    stack = lambda main_ref, halo_ref: jnp.concatenate([main_ref[...], halo_ref[...]], axis=0)
    mix = (jnp.dot(stack(d_ref, dh_ref), wo_ref[:DIFF_WIDTH], preferred_element_type=F32)
           + jnp.dot(stack(s_ref, sh_ref), wo_ref[DIFF_WIDTH:], preferred_element_type=F32))
    x1 = stack(x_ref, xh_ref) + _rms(mix, gattn_ref[...])
    h = _rms(x1, gpre_ref[...])
    row = lax.broadcasted_iota(jnp.int32, h.shape, 0)
    lhs = jnp.where(jnp.logical_or(row < rows, i > 0), h, 0.0).astype(BF16)

    def conv(col, slot):
        cols = slice(col, col + chunk)
        u = jnp.dot(lhs, wup_ref[:, cols], preferred_element_type=F32)
        u_sc[slot, 0:HALO_ROWS, :] = u[rows:]
        u_sc[slot, HALO_ROWS:, :] = u[:rows]
        y = cb_ref[:, cols]
        for k in range(CONV_WIDTH):
            off = HALO_ROWS - (CONV_WIDTH - 1) + k
            y = y + u_sc[slot, off:off + rows, :] * cw_ref[k:k + 1, cols]
        return y

    n_chunk = D_FF // chunk
    gates = [conv(c * chunk, 2 * c) for c in range(n_chunk)]
    vals = [conv(D_FF + c * chunk, 2 * c + 1) for c in range(n_chunk)]
    y = None
    for c in range(n_chunk):
        act = (_gelu_tanh(gates[c]) * vals[c]).astype(BF16)
        part = jnp.dot(act, wd_ref[c * chunk:(c + 1) * chunk, :], preferred_element_type=F32)
        y = part if y is None else y + part
    o_ref[...] = x1[:rows] + _rms(y, gffn_ref[...])


def _mlp(diff_o, sb_o, x2, w_o, g_attn, g_pre, w_up, conv_w, conv_b, w_down, g_ffn):
    s = x2.shape[0]
    rows = MLP_ROWS
    halo_blocks = rows // HALO_ROWS
    row_spec = lambda w: pl.BlockSpec((rows, w), lambda i: (i, 0))
    halo_spec = lambda w: pl.BlockSpec((HALO_ROWS, w), lambda i: (jnp.maximum(i * halo_blocks - 1, 0), 0))
    resident = lambda a: pl.BlockSpec(a.shape, lambda i: (0, 0), pipeline_mode=pl.Buffered(1))
    return pl.pallas_call(
        _mlp_kernel,
        grid=(s // rows,),
        in_specs=[row_spec(DIFF_WIDTH), row_spec(SB_WIDTH), row_spec(D_MODEL),
                  halo_spec(DIFF_WIDTH), halo_spec(SB_WIDTH), halo_spec(D_MODEL),
                  resident(w_o), resident(g_attn), resident(g_pre),
                  resident(w_up), resident(conv_w), resident(conv_b), resident(w_down), resident(g_ffn)],
        out_specs=row_spec(D_MODEL),
        out_shape=jax.ShapeDtypeStruct((s, D_MODEL), F32),
        scratch_shapes=[pltpu.VMEM((2 * (D_FF // FFN_CHUNK), HALO_ROWS + rows, FFN_CHUNK), F32)],
        compiler_params=pltpu.CompilerParams(
            dimension_semantics=("arbitrary",), vmem_limit_bytes=MLP_VMEM_LIMIT),
        name="mlp",
    )(diff_o, sb_o, x2, diff_o, sb_o, x2, w_o, g_attn, g_pre, w_up, conv_w, conv_b, w_down, g_ffn)


def kernel(x, attn_pre_norm, w_qkv, lambda_q1, lambda_k1, lambda_q2, lambda_k2, diff_subln, sb_norm,
           rel_bias, w_o, attn_post_norm, ffn_pre_norm, w_up, conv_w, conv_b, w_down, ffn_post_norm):
    b, s, _ = x.shape
    assert b == 1 and attn_pre_norm.shape[0] == 1, "single sequence, single layer"
    assert s % max(PROJ_ROWS, 2 * DIFF_BLOCK, SB_Q_BLOCK, MLP_ROWS) == 0
    x2 = x[0]
    row = lambda a: a.reshape(1, -1).astype(F32)

    qd, vd, qs, vs, kdt, kst, w_up_bf, w_down_bf = _proj(
        x2, row(attn_pre_norm), w_qkv[0].astype(F32), w_up.astype(F32), w_down.astype(F32))

    lam_params = jnp.concatenate([lambda_q1, lambda_k1, lambda_q2, lambda_k2], axis=0).astype(F32)
    diff_o = _diff_attention(rel_bias.astype(F32), lam_params, qd, kdt, vd, row(diff_subln))
    sb_o = _sb_attention(qs, kst, vs, row(jnp.concatenate([sb_norm[0], sb_norm[0]])))

    out = _mlp(diff_o, sb_o, x2, w_o[0].astype(BF16), row(attn_post_norm), row(ffn_pre_norm),
               w_up_bf[0], conv_w[0].astype(F32), conv_b.astype(F32), w_down_bf[0],
               row(ffn_post_norm))
    return out[None]
```
